```python
import math
import jax, jax.numpy as jnp
from jax import lax
import numpy as np

D_MODEL = 1024
BATCH = 16
SEQ = 4096
DEPTH = 1
DEC_BATCH = 128
DEC_SEQ = 8
PAST_LEN = 8192
PAGE_SIZE = 128

GLA_HEADS = 4
GLA_DK = 64
GLA_DV = 128
GLA_QK = GLA_HEADS * GLA_DK
GLA_V = GLA_HEADS * GLA_DV
GLA_GATE_RANK = 16
GLA_GATE_NORM = 16.0
GLA_CHUNK = 64
DIL_HEADS = 8
DIL_HD = 64
DIL_W = DIL_HEADS * DIL_HD
DIL_PATTERNS = ((128, 1), (512, 4), (2048, 16))
DIL_WMAX = 2048
Q_BLOCK = 128
D_MIX = GLA_V + DIL_W
SPLIT_POINTS = (GLA_QK, 2 * GLA_QK, 2 * GLA_QK + GLA_V, 2 * GLA_QK + GLA_V + GLA_GATE_RANK,
                2 * GLA_QK + 2 * GLA_V + GLA_GATE_RANK, 2 * GLA_QK + 2 * GLA_V + GLA_GATE_RANK + DIL_W,
                2 * GLA_QK + 2 * GLA_V + GLA_GATE_RANK + 2 * DIL_W)
D_IN = 2 * GLA_QK + 2 * GLA_V + GLA_GATE_RANK + 3 * DIL_W
N_MEM = 256
MEM_HEADS = 4
MEM_HD = D_MODEL // MEM_HEADS
N_GROUPS = 4
EXP_PER_GROUP = 8
N_EXPERTS = N_GROUPS * EXP_PER_GROUP
TOP_K = 2
D_EXPERT = 512
MOE_BLOCK = 128
EPS = 1e-6

kernel_name = 'hybrid_gla_dilated_hmoe_decode_step'


def rmsnorm(x, g):
    xf = x.astype(jnp.float32)
    y = xf * lax.rsqrt(jnp.mean(xf * xf, axis=-1, keepdims=True) + EPS)
    return (y * g.astype(jnp.float32)).astype(x.dtype)


def alibi_slopes():
    return jnp.asarray([2.0 ** (-8.0 * (h + 1) / DIL_HEADS) for h in range(DIL_HEADS)], dtype=jnp.float32)


def gla_chunked(q, k, v, log_a, s0):
    B, T, H, DK = q.shape
    DV = v.shape[-1]
    C = math.gcd(T, GLA_CHUNK)
    n = T // C

    def to_chunks(a):
        return a.reshape(B, n, C, H, a.shape[-1]).transpose(1, 0, 3, 2, 4).astype(jnp.float32)

    qc, kc, vc, gc = to_chunks(q), to_chunks(k), to_chunks(v), to_chunks(log_a)
    causal = jnp.tril(jnp.ones((C, C), dtype=bool))

    def step(S, inp):
        qi, ki, vi, gi = inp
        b = jnp.cumsum(gi, axis=2)
        diff = jnp.where(causal[:, :, None], b[:, :, :, None, :] - b[:, :, None, :, :], -jnp.inf)
        A = jnp.einsum('bhik,bhjk,bhijk->bhij', qi, ki, jnp.exp(diff))
        o = jnp.einsum('bhij,bhjv->bhiv', A, vi) + jnp.einsum('bhik,bhkv->bhiv', qi * jnp.exp(b), S)
        bC = b[:, :, -1, :]
        S_new = jnp.exp(bC)[..., None] * S + jnp.einsum('bhjk,bhjv->bhkv', ki * jnp.exp(bC[:, :, None, :] - b), vi)
        return S_new, o

    S_fin, oc = lax.scan(step, s0.astype(jnp.float32), (qc, kc, vc, gc))
    o = oc.transpose(1, 0, 3, 2, 4).reshape(B, T, H, DV)
    return o, S_fin


def dilated_block(q, k_ext, v_ext, qidx):
    qf = q.astype(jnp.float32) * (DIL_HD ** -0.5)
    slopes = alibi_slopes()
    lses, outs = [], []
    for (w, d) in DIL_PATTERNS:
        nk = w // d + 1
        dist = jnp.arange(nk, dtype=jnp.int32) * d
        idx = qidx[:, None] - dist[None, :]
        valid = idx >= 0
        idx = jnp.maximum(idx, 0)
        kg = k_ext[:, idx].astype(jnp.float32)
        vg = v_ext[:, idx].astype(jnp.float32)
        s = jnp.einsum('bqhd,bqkhd->bqhk', qf, kg) - slopes[:, None] * dist.astype(jnp.float32)
        s = jnp.where(valid[None, :, None, :], s, -jnp.inf)
        lse = jax.nn.logsumexp(s, axis=-1)
        p = jnp.exp(s - lse[..., None])
        outs.append(jnp.einsum('bqhk,bqkhd->bqhd', p, vg))
        lses.append(lse)
    wts = jax.nn.softmax(jnp.stack(lses, axis=0), axis=0)
    return jnp.sum(wts[..., None] * jnp.stack(outs, axis=0), axis=0)


def dilated_attention(q, k_ext, v_ext, offset):
    B, T, H, Dh = q.shape
    blk = math.gcd(T, Q_BLOCK)
    nb = T // blk
    qb = q.reshape(B, nb, blk, H, Dh).transpose(1, 0, 2, 3, 4)
    qidx = (offset + jnp.arange(T, dtype=jnp.int32)).reshape(nb, blk)
    ob = lax.map(lambda a: dilated_block(a[0], k_ext, v_ext, a[1]), (qb, qidx))
    return ob.transpose(1, 0, 2, 3, 4).reshape(B, T, H, Dh)


def hybrid_mixer(n, s0, k_past, v_past, w_in, w_gk2, b_gk, g_gla, w_out):
    B, T, _ = n.shape
    proj = n @ w_in
    q_a, k_a, v_a, gk_lr, r_a, q_b, k_b, v_b = jnp.split(proj, SPLIT_POINTS, axis=-1)
    q_a = q_a.reshape(B, T, GLA_HEADS, GLA_DK) * (GLA_DK ** -0.5)
    k_a = k_a.reshape(B, T, GLA_HEADS, GLA_DK)
    v_a = v_a.reshape(B, T, GLA_HEADS, GLA_DV)
    gk = (gk_lr @ w_gk2 + b_gk).astype(jnp.float32)
    log_a = (jax.nn.log_sigmoid(gk) / GLA_GATE_NORM).reshape(B, T, GLA_HEADS, GLA_DK)
    o_a, s_new = gla_chunked(q_a, k_a, v_a, log_a, s0)
    o_a = rmsnorm(o_a, g_gla) * jax.nn.silu(r_a.astype(jnp.float32)).reshape(B, T, GLA_HEADS, GLA_DV)
    o_a = o_a.reshape(B, T, GLA_V).astype(n.dtype)
    q_b = q_b.reshape(B, T, DIL_HEADS, DIL_HD)
    k_b = k_b.reshape(B, T, DIL_HEADS, DIL_HD)
    v_b = v_b.reshape(B, T, DIL_HEADS, DIL_HD)
    k_ext = jnp.concatenate([k_past.astype(k_b.dtype), k_b], axis=1)
    v_ext = jnp.concatenate([v_past.astype(v_b.dtype), v_b], axis=1)
    o_b = dilated_attention(q_b, k_ext, v_ext, k_past.shape[1]).reshape(B, T, DIL_W).astype(n.dtype)
    y = jnp.concatenate([o_a, o_b], axis=-1) @ w_out
    return y, s_new.astype(n.dtype), k_b, v_b


def memory_kv(mem, g_mem, w_mk, w_mv):
    B = mem.shape[0]
    m = rmsnorm(mem, g_mem)
    return ((m @ w_mk).reshape(B, N_MEM, MEM_HEADS, MEM_HD), (m @ w_mv).reshape(B, N_MEM, MEM_HEADS, MEM_HD))


def cross_attend(h, mk, mv, w_cq, w_co):
    B, T, D = h.shape
    q = (h @ w_cq).reshape(B, T, MEM_HEADS, MEM_HD).astype(jnp.float32) * (MEM_HD ** -0.5)
    s = jnp.einsum('bthd,bshd->bhts', q, mk.astype(jnp.float32))
    p = jax.nn.softmax(s, axis=-1)
    o = jnp.einsum('bhts,bshd->bthd', p, mv.astype(jnp.float32)).reshape(B, T, D).astype(h.dtype)
    return o @ w_co


def hier_moe(x, w_gr, b_gr, w_er, b_er, w_e1, w_e3, w_e2):
    N, D = x.shape
    xf = x.astype(jnp.float32)
    g_logits = xf @ w_gr.astype(jnp.float32) + b_gr.astype(jnp.float32)
    g_idx = jnp.argmax(g_logits, axis=-1)
    p_g = jnp.take_along_axis(jax.nn.softmax(g_logits, axis=-1), g_idx[:, None], axis=1)[:, 0]
    e_logits = (xf @ w_er.astype(jnp.float32) + b_er.astype(jnp.float32)).reshape(N, N_GROUPS, EXP_PER_GROUP)
    sel = jnp.take_along_axis(e_logits, g_idx[:, None, None], axis=1)[:, 0]
    top_l, top_i = lax.top_k(sel, TOP_K)
    gate = p_g[:, None] * jax.nn.softmax(top_l, axis=-1)
    expert = (g_idx[:, None] * EXP_PER_GROUP + top_i).astype(jnp.int32)
    M = N * TOP_K
    flat_e = expert.reshape(M)
    flat_tok = jnp.repeat(jnp.arange(N, dtype=jnp.int32), TOP_K)
    flat_w = gate.reshape(M)
    order = jnp.argsort(flat_e)
    se = flat_e[order]
    counts = jnp.bincount(flat_e, length=N_EXPERTS).astype(jnp.int32)
    padded = (counts + MOE_BLOCK - 1) // MOE_BLOCK * MOE_BLOCK
    pad_end = jnp.cumsum(padded)
    pad_start = pad_end - padded
    start = jnp.cumsum(counts) - counts
    dest = pad_start[se] + jnp.arange(M, dtype=jnp.int32) - start[se]
    nb = -(-M // MOE_BLOCK) + N_EXPERTS
    R = nb * MOE_BLOCK
    row_tok = jnp.full((R,), N, dtype=jnp.int32).at[dest].set(flat_tok[order])
    row_w = jnp.zeros((R,), dtype=jnp.float32).at[dest].set(flat_w[order])
    block_e = jnp.minimum(jnp.searchsorted(pad_end, jnp.arange(nb, dtype=jnp.int32) * MOE_BLOCK, side='right'),
                          N_EXPERTS - 1)
    xs = jnp.concatenate([x, jnp.zeros((1, D), x.dtype)], axis=0)[row_tok].reshape(nb, MOE_BLOCK, D)

    def expert_block(a):
        xb, e = a
        return (jax.nn.silu(xb @ w_e1[e]) * (xb @ w_e3[e])) @ w_e2[e]

    yb = lax.map(expert_block, (xs, block_e)).reshape(R, D)
    y = jax.ops.segment_sum(yb.astype(jnp.float32) * row_w[:, None], row_tok, num_segments=N + 1)[:N]
    return y.astype(x.dtype)


def layer(x, mk, mv, s0, k_past, v_past, g1, w_in, w_gk2, b_gk, g_gla, w_out, g2, w_cq, w_co,
          g3, w_gr, b_gr, w_er, b_er, w_e1, w_e3, w_e2):
    B, T, D = x.shape
    mix, s_new, k_new, v_new = hybrid_mixer(rmsnorm(x, g1), s0, k_past, v_past, w_in, w_gk2, b_gk, g_gla, w_out)
    h = x + mix
    h = h + cross_attend(rmsnorm(h, g2), mk, mv, w_cq, w_co)
    h = h + hier_moe(rmsnorm(h, g3).reshape(B * T, D), w_gr, b_gr, w_er, b_er, w_e1, w_e3, w_e2).reshape(B, T, D)
    return h, s_new, k_new, v_new


def setup_inputs(seed: int = 0) -> dict:
    key = jax.random.key(seed)
    ks = jax.random.split(key, 32)

    def nrm(i, shape, scale):
        return jax.random.normal(ks[i], shape, jnp.float32) * scale

    L = DEPTH
    buf = min(DIL_WMAX, PAST_LEN)
    return {
        'x_prompt': nrm(0, (BATCH, SEQ, D_MODEL), 1.0),
        'x_sample': nrm(1, (DEC_BATCH, DEC_SEQ, D_MODEL), 1.0),
        'cache_swa_k': nrm(2, (L, DEC_BATCH, buf, DIL_HEADS, DIL_HD), 1.0),
        'cache_swa_v': nrm(3, (L, DEC_BATCH, buf, DIL_HEADS, DIL_HD), 1.0),
        'state_gla': nrm(4, (L, DEC_BATCH, GLA_HEADS, GLA_DK, GLA_DV), 0.5),
        'cache_mem_k': nrm(5, (L, DEC_BATCH, N_MEM, MEM_HEADS, MEM_HD), 1.0),
        'cache_mem_v': nrm(6, (L, DEC_BATCH, N_MEM, MEM_HEADS, MEM_HD), 1.0),
        'mem_prompt': nrm(7, (BATCH, N_MEM, D_MODEL), 1.0),
        'g_norm1': 1.0 + nrm(8, (L, D_MODEL), 0.02),
        'w_in': nrm(9, (L, D_MODEL, D_IN), D_MODEL ** -0.5),
        'w_gk2': nrm(10, (L, GLA_GATE_RANK, GLA_QK), GLA_GATE_RANK ** -0.5),
        'b_gk': nrm(11, (L, GLA_QK), 0.1),
        'g_gla_out': 1.0 + nrm(12, (L, GLA_DV), 0.02),
        'w_out': nrm(13, (L, D_MIX, D_MODEL), D_MIX ** -0.5),
        'g_norm2': 1.0 + nrm(14, (L, D_MODEL), 0.02),
        'g_mem': 1.0 + nrm(15, (L, D_MODEL), 0.02),
        'w_cq': nrm(16, (L, D_MODEL, D_MODEL), D_MODEL ** -0.5),
        'w_mk': nrm(17, (L, D_MODEL, D_MODEL), D_MODEL ** -0.5),
        'w_mv': nrm(18, (L, D_MODEL, D_MODEL), D_MODEL ** -0.5),
        'w_co': nrm(19, (L, D_MODEL, D_MODEL), D_MODEL ** -0.5),
        'g_norm3': 1.0 + nrm(20, (L, D_MODEL), 0.02),
        'w_gr': nrm(21, (L, D_MODEL, N_GROUPS), D_MODEL ** -0.5),
        'b_gr': nrm(22, (L, N_GROUPS), 0.01),
        'w_er': nrm(23, (L, D_MODEL, N_EXPERTS), D_MODEL ** -0.5),
        'b_er': nrm(24, (L, N_EXPERTS), 0.01),
        'w_e1': nrm(25, (L, N_EXPERTS, D_MODEL, D_EXPERT), D_MODEL ** -0.5),
        'w_e3': nrm(26, (L, N_EXPERTS, D_MODEL, D_EXPERT), D_MODEL ** -0.5),
        'w_e2': nrm(27, (L, N_EXPERTS, D_EXPERT, D_MODEL), D_EXPERT ** -0.5),
        'g_final': 1.0 + nrm(28, (D_MODEL,), 0.02),
    }


def reference(x_prompt, x_sample, cache_swa_k, cache_swa_v, state_gla, cache_mem_k, cache_mem_v, mem_prompt,
              g_norm1, w_in, w_gk2, b_gk, g_gla_out, w_out, g_norm2, g_mem, w_cq, w_mk, w_mv, w_co,
              g_norm3, w_gr, b_gr, w_er, b_er, w_e1, w_e3, w_e2, g_final):
    hp, hs = x_prompt, x_sample
    B, T, _ = x_prompt.shape
    keep = min(DIL_WMAX, T)
    kp_l, vp_l, sp_l, mk_l, mv_l, ks_l, vs_l, ss_l = [], [], [], [], [], [], [], []
    for l in range(DEPTH):
        lw = (g_norm1[l], w_in[l], w_gk2[l], b_gk[l], g_gla_out[l], w_out[l], g_norm2[l], w_cq[l], w_co[l],
              g_norm3[l], w_gr[l], b_gr[l], w_er[l], b_er[l], w_e1[l], w_e3[l], w_e2[l])
        mk, mv = memory_kv(mem_prompt, g_mem[l], w_mk[l], w_mv[l])
        s0 = jnp.zeros((B, GLA_HEADS, GLA_DK, GLA_DV), hp.dtype)
        empty = jnp.zeros((B, 0, DIL_HEADS, DIL_HD), hp.dtype)
        hp, sp, kp, vp = layer(hp, mk, mv, s0, empty, empty, *lw)
        kp_l.append(kp[:, T - keep:])
        vp_l.append(vp[:, T - keep:])
        sp_l.append(sp)
        mk_l.append(mk)
        mv_l.append(mv)
        hs, ss, kn, vn = layer(hs, cache_mem_k[l], cache_mem_v[l], state_gla[l], cache_swa_k[l], cache_swa_v[l], *lw)
        ks_l.append(kn)
        vs_l.append(vn)
        ss_l.append(ss)
    y_prompt = rmsnorm(hp, g_final)
    y_sample = rmsnorm(hs, g_final)
    return (y_prompt, y_sample, jnp.stack(kp_l), jnp.stack(vp_l), jnp.stack(sp_l), jnp.stack(mk_l), jnp.stack(mv_l),
            jnp.stack(ks_l), jnp.stack(vs_l), jnp.stack(ss_l))
```

```python
import functools
import math

import numpy as np
import jax
import jax.numpy as jnp
from jax import lax
from jax.experimental import pallas as pl
from jax.experimental.pallas import tpu as pltpu

F32, BF16, I32 = jnp.float32, jnp.bfloat16, jnp.int32

GLA_HEADS, GLA_DK, GLA_DV = 4, 64, 128
GLA_QK, GLA_V = GLA_HEADS * GLA_DK, GLA_HEADS * GLA_DV
GATE_RANK, GATE_NORM, GLA_CHUNK = 16, 16.0, 64
DIL_HEADS, DIL_HD = 8, 64
DIL_W = DIL_HEADS * DIL_HD
DIL_PATTERNS = ((128, 1), (512, 4), (2048, 16))
DIL_KEYS = 128
MEM_HEADS = 4
N_GROUPS, EXP_PER_GROUP, TOP_K = 4, 8, 2
N_EXPERTS = N_GROUPS * EXP_PER_GROUP
EPS = 1e-6

LANE = 128
VMEM_LIMIT_BYTES = 56 * 1024 * 1024

PA_W = 2 * GLA_QK + 2 * GLA_V + LANE
PB_W = 3 * DIL_W

NT = (((1,), (1,)), ((), ()))
TN = (((0,), (0,)), ((), ()))


def _params(*sem):
    return pltpu.CompilerParams(dimension_semantics=sem, vmem_limit_bytes=VMEM_LIMIT_BYTES)


def _rms(x, g):
    y = x * lax.rsqrt(jnp.mean(x * x, axis=-1, keepdims=True) + EPS)
    return y * g


def _silu(x):
    return x / (1.0 + jnp.exp(-x))


def _iota_div(shape, dim, n):
    assert n & (n - 1) == 0
    return lax.broadcasted_iota(I32, shape, dim) >> int(math.log2(n))


def _norm_proj_kernel(x_ref, g_ref, w_ref, pa_ref, pb_ref, kv_ref):
    n = _rms(x_ref[...], g_ref[...]).astype(BF16)
    for c0 in range(0, PA_W, DIL_W):
        c1 = min(c0 + DIL_W, PA_W)
        pa_ref[:, c0:c1] = jnp.dot(n, w_ref[:, c0:c1], preferred_element_type=F32).astype(pa_ref.dtype)
    for j in range(3):
        r = jnp.dot(n, w_ref[:, PA_W + j * DIL_W:PA_W + (j + 1) * DIL_W], preferred_element_type=F32)
        pb_ref[:, j * DIL_W:(j + 1) * DIL_W] = r.astype(pb_ref.dtype)
        if j >= 1:
            kv_ref[:, (j - 1) * DIL_W:j * DIL_W] = r


def _norm_proj(x, g, w, out_dtype, tm):
    n, d = x.shape
    return pl.pallas_call(
        _norm_proj_kernel,
        grid=(n // tm,),
        in_specs=[pl.BlockSpec((tm, d), lambda i: (i, 0)),
                  pl.BlockSpec((1, d), lambda i: (0, 0)),
                  pl.BlockSpec((d, PA_W + PB_W), lambda i: (0, 0))],
        out_specs=[pl.BlockSpec((tm, PA_W), lambda i: (i, 0)),
                   pl.BlockSpec((tm, PB_W), lambda i: (i, 0)),
                   pl.BlockSpec((tm, 2 * DIL_W), lambda i: (i, 0))],
        out_shape=[jax.ShapeDtypeStruct((n, PA_W), out_dtype),
                   jax.ShapeDtypeStruct((n, PB_W), out_dtype),
                   jax.ShapeDtypeStruct((n, 2 * DIL_W), F32)],
        compiler_params=_params("parallel"),
        name="norm_proj",
    )(x, g, w)


def _norm_matmul_kernel(x_ref, g_ref, w_ref, o_ref):
    n = _rms(x_ref[...], g_ref[...]).astype(BF16)
    for c0 in range(0, o_ref.shape[1], DIL_W):
        o_ref[:, c0:c0 + DIL_W] = jnp.dot(n, w_ref[:, c0:c0 + DIL_W], preferred_element_type=F32)


def _norm_matmul(x, g, w, tm):
    n, d = x.shape
    m = w.shape[1]
    return pl.pallas_call(
        _norm_matmul_kernel,
        grid=(n // tm,),
        in_specs=[pl.BlockSpec((tm, d), lambda i: (i, 0)),
                  pl.BlockSpec((1, d), lambda i: (0, 0)),
                  pl.BlockSpec((d, m), lambda i: (0, 0))],
        out_specs=pl.BlockSpec((tm, m), lambda i: (i, 0)),
        out_shape=jax.ShapeDtypeStruct((n, m), F32),
        compiler_params=_params("parallel"),
        name="norm_matmul",
    )(x, g, w)


def _gla_tables(chunk, seg):
    idx = np.arange(chunk)
    tril = ((idx[None, :] <= idx[:, None]) & (idx[None, :] // seg == idx[:, None] // seg)).astype(np.float32)
    mats, masks, levels = [tril], [], []
    s = seg // 2
    while s >= 1:
        mats.append(tril[(idx // (2 * s)) * (2 * s) + s - 1])
        same = (idx[:, None] // (2 * s)) == (idx[None, :] // (2 * s))
        masks.append(same & ((idx[:, None] // s) % 2 == 1) & ((idx[None, :] // s) % 2 == 0))
        levels.append(s)
        s //= 2
    mats.append(tril[(idx // seg) * seg + seg - 1])
    masks.append(idx[:, None] == idx[None, :])
    return (jnp.asarray(np.concatenate(mats, 0), BF16), jnp.asarray(np.stack(masks).astype(np.float32)),
            tuple(levels))


def _gla_chunk(q, k, v, glr, wgk, bgk, cm_ref, pm_ref, levels):
    c = q.shape[0]
    nl = len(levels)
    gk = jnp.dot(glr, wgk, preferred_element_type=F32) + bgk
    la = (jnp.minimum(gk, 0.0) - jnp.log1p(jnp.exp(-jnp.abs(gk)))) * (1.0 / GATE_NORM)
    hi = la.astype(BF16)
    r1 = la - hi.astype(F32)
    mid = r1.astype(BF16)
    lo = (r1 - mid.astype(F32)).astype(BF16)
    br = jnp.dot(cm_ref[...], jnp.concatenate([hi, mid, lo], axis=1), preferred_element_type=F32)
    br = br[:, :GLA_QK] + br[:, GLA_QK:2 * GLA_QK] + br[:, 2 * GLA_QK:]
    b = br[0:c]
    b_end = br[(nl + 1) * c:(nl + 2) * c]

    row = lax.broadcasted_iota(I32, (c, GLA_QK), 0)
    head = _iota_div((c, GLA_QK), 1, GLA_DK)
    acc = [None] * GLA_HEADS

    def add_level(qe, ke, pm):
        for h in range(GLA_HEADS):
            a = lax.dot_general(jnp.where(head == h, qe, jnp.zeros_like(qe)), ke, NT, preferred_element_type=F32)
            a = jnp.where(pm > 0.0, a, 0.0)
            acc[h] = a if acc[h] is None else acc[h] + a

    add_level(q.astype(BF16), k.astype(BF16), pm_ref[nl])
    for l, s in enumerate(levels):
        ref = br[(l + 1) * c:(l + 2) * c]
        right = ((row >> int(math.log2(s))) & 1) == 1
        qe = (q * jnp.exp(jnp.where(right, b - ref, 0.0))).astype(BF16)
        ke = (k * jnp.exp(jnp.where(right, 0.0, ref - b))).astype(BF16)
        add_level(qe, ke, pm_ref[l])

    o = jnp.concatenate(
        [jnp.dot(acc[h].astype(BF16), v[:, h * GLA_DV:(h + 1) * GLA_DV], preferred_element_type=F32)
         for h in range(GLA_HEADS)], axis=1)
    qb = (q * jnp.exp(b)).astype(BF16)
    kd = (k * jnp.exp(b_end - b)).astype(BF16)
    return o, qb, kd, b_end


def _gla_finish(o, r, gg):
    outs = []
    for h in range(GLA_HEADS):
        sl = slice(h * GLA_DV, (h + 1) * GLA_DV)
        outs.append(_rms(o[:, sl], gg) * _silu(r[:, sl]))
    return jnp.concatenate(outs, axis=1)


def _state_mask():
    return _iota_div((GLA_V, GLA_QK), 0, GLA_DV) == _iota_div((GLA_V, GLA_QK), 1, GLA_DK)


def _gla_prompt_kernel(q_ref, k_ref, v_ref, r_ref, glr_ref, wgk_ref, bgk_ref, gg_ref, cm_ref, pm_ref, s0_ref,
                       o_ref, sout_ref, st_ref, *, chunk, levels):
    i = pl.program_id(1)

    @pl.when(i == 0)
    def _():
        st_ref[...] = s0_ref[...]

    smask = _state_mask()

    def body(c, carry):
        rows = pl.ds(pl.multiple_of(c * chunk, chunk), chunk)
        q = q_ref[rows, :].astype(F32) * (GLA_DK ** -0.5)
        k = k_ref[rows, :].astype(F32)
        v = v_ref[rows, :].astype(BF16)
        o, qb, kd, b_end = _gla_chunk(q, k, v, glr_ref[rows, :].astype(BF16), wgk_ref[...], bgk_ref[...],
                                      cm_ref, pm_ref, levels)
        st = st_ref[...]
        o = o + lax.dot_general(qb, st.astype(BF16), NT, preferred_element_type=F32)
        u = lax.dot_general(v, kd, TN, preferred_element_type=F32)
        st_ref[...] = st * jnp.exp(b_end[0:1, :]) + jnp.where(smask, u, 0.0)
        o_ref[rows, :] = _gla_finish(o, r_ref[rows, :].astype(F32), gg_ref[...]).astype(o_ref.dtype)
        return carry

    lax.fori_loop(0, q_ref.shape[0] // chunk, body, 0)

    @pl.when(i == pl.num_programs(1) - 1)
    def _():
        sout_ref[...] = st_ref[...]


def _gla_prompt(pa, s0t, wgk, bgk, gg, batch, seq, tb):
    chunk = math.gcd(seq, GLA_CHUNK)
    cm, pm, levels = _gla_tables(chunk, chunk)
    nt = seq // tb
    row = lambda b, i: (b * nt + i, 0)
    const2 = lambda b, i: (0, 0)
    return pl.pallas_call(
        functools.partial(_gla_prompt_kernel, chunk=chunk, levels=levels),
        grid=(batch, nt),
        in_specs=[pl.BlockSpec((tb, GLA_QK), row),
                  pl.BlockSpec((tb, GLA_QK), lambda b, i: (b * nt + i, 1)),
                  pl.BlockSpec((tb, GLA_V), lambda b, i: (b * nt + i, 1)),
                  pl.BlockSpec((tb, GLA_V), lambda b, i: (b * nt + i, 2)),
                  pl.BlockSpec((tb, LANE), lambda b, i: (b * nt + i, (PA_W - LANE) // LANE)),
                  pl.BlockSpec(wgk.shape, const2),
                  pl.BlockSpec(bgk.shape, const2),
                  pl.BlockSpec(gg.shape, const2),
                  pl.BlockSpec(cm.shape, const2),
                  pl.BlockSpec(pm.shape, lambda b, i: (0, 0, 0)),
                  pl.BlockSpec((None, GLA_V, GLA_QK), lambda b, i: (b, 0, 0))],
        out_specs=[pl.BlockSpec((tb, GLA_V), row),
                   pl.BlockSpec((None, GLA_V, GLA_QK), lambda b, i: (b, 0, 0))],
        out_shape=[jax.ShapeDtypeStruct((batch * seq, GLA_V), BF16),
                   jax.ShapeDtypeStruct((batch, GLA_V, GLA_QK), F32)],
        scratch_shapes=[pltpu.VMEM((GLA_V, GLA_QK), F32)],
        compiler_params=_params("parallel", "arbitrary"),
        name="gla_prompt",
    )(pa, pa, pa, pa, pa, wgk, bgk, gg, cm, pm, s0t)


def _gla_sample_kernel(q_ref, k_ref, v_ref, r_ref, glr_ref, wgk_ref, bgk_ref, gg_ref, cm_ref, pm_ref, s0_ref,
                       o_ref, sout_ref, *, seg, levels):
    rows = q_ref.shape[0]
    q = q_ref[...].astype(F32) * (GLA_DK ** -0.5)
    k = k_ref[...].astype(F32)
    v = v_ref[...].astype(BF16)
    o, qb, kd, b_end = _gla_chunk(q, k, v, glr_ref[...].astype(BF16), wgk_ref[...], bgk_ref[...],
                                  cm_ref, pm_ref, levels)
    smask = _state_mask()
    seq_o = _iota_div((rows, GLA_V), 0, seg)
    for j in range(rows // seg):
        st = s0_ref[j]
        oj = lax.dot_general(qb, st.astype(BF16), NT, preferred_element_type=F32)
        o = o + jnp.where(seq_o == j, oj, 0.0)
        u = lax.dot_general(jnp.where(seq_o == j, v, jnp.zeros_like(v)), kd, TN, preferred_element_type=F32)
        sout_ref[j] = st * jnp.exp(b_end[j * seg:j * seg + 1, :]) + jnp.where(smask, u, 0.0)
    o_ref[...] = _gla_finish(o, r_ref[...].astype(F32), gg_ref[...]).astype(o_ref.dtype)


def _gla_sample(pa, s0t, wgk, bgk, gg, batch, seq, rows):
    seg = math.gcd(seq, GLA_CHUNK)
    assert seg == seq, "sample sequences longer than one chunk are not supported"
    cm, pm, levels = _gla_tables(rows, seg)
    nseq = rows // seg
    row = lambda i: (i, 0)
    const2 = lambda i: (0, 0)
    return pl.pallas_call(
        functools.partial(_gla_sample_kernel, seg=seg, levels=levels),
        grid=(batch * seq // rows,),
        in_specs=[pl.BlockSpec((rows, GLA_QK), row),
                  pl.BlockSpec((rows, GLA_QK), lambda i: (i, 1)),
                  pl.BlockSpec((rows, GLA_V), lambda i: (i, 1)),
                  pl.BlockSpec((rows, GLA_V), lambda i: (i, 2)),
                  pl.BlockSpec((rows, LANE), lambda i: (i, (PA_W - LANE) // LANE)),
                  pl.BlockSpec(wgk.shape, const2),
                  pl.BlockSpec(bgk.shape, const2),
                  pl.BlockSpec(gg.shape, const2),
                  pl.BlockSpec(cm.shape, const2),
                  pl.BlockSpec(pm.shape, lambda i: (0, 0, 0)),
                  pl.BlockSpec((nseq, GLA_V, GLA_QK), lambda i: (i, 0, 0))],
        out_specs=[pl.BlockSpec((rows, GLA_V), row),
                   pl.BlockSpec((nseq, GLA_V, GLA_QK), lambda i: (i, 0, 0))],
        out_shape=[jax.ShapeDtypeStruct((batch * seq, GLA_V), F32),
                   jax.ShapeDtypeStruct((batch, GLA_V, GLA_QK), F32)],
        compiler_params=_params("parallel"),
        name="gla_sample",
    )(pa, pa, pa, pa, pa, wgk, bgk, gg, cm, pm, s0t)


def _state_to_blockdiag_t(s):
    b = s.shape[0]
    st = jnp.swapaxes(s, 2, 3)
    eye = jnp.eye(GLA_HEADS, dtype=s.dtype)
    return (st[:, :, :, None, :] * eye[None, :, None, :, None]).reshape(b, GLA_V, GLA_QK)


def _blockdiag_t_to_state(st):
    b = st.shape[0]
    s5 = st.reshape(b, GLA_HEADS, GLA_DV, GLA_HEADS, GLA_DK)
    return jnp.stack([jnp.swapaxes(s5[:, h, :, h, :], 1, 2) for h in range(GLA_HEADS)], axis=1)


def _alibi_slopes():
    return np.asarray([2.0 ** (-8.0 * (h + 1) / DIL_HEADS) for h in range(DIL_HEADS)], np.float64)


def _dil_bias(d):
    i = np.arange(DIL_KEYS)[:, None]
    c = np.arange(2 * DIL_KEYS)[None, :]
    dist = DIL_KEYS + i - c
    ok = (dist >= 0) & (dist <= DIL_KEYS)
    out = np.empty((2, DIL_HEADS, DIL_KEYS, 2 * DIL_KEYS), np.float32)
    for h, sl in enumerate(_alibi_slopes()):
        full = np.where(ok, -sl * (dist * d), -np.inf)
        out[1, h] = full
        out[0, h] = np.where(c >= DIL_KEYS, full, -np.inf)
    return jnp.asarray(out)


def _dil_prompt_kernel(*refs, qb, first, last):
    q_ref, kp_ref, kc_ref, vp_ref, vc_ref, b0_ref, b1_ref = refs[:7]
    refs = refs[7:]
    if not first:
        acc_in, st_in = refs[:2]
        refs = refs[2:]
    if last:
        (o_ref,) = refs
    else:
        acc_out, st_out = refs
    blk = DIL_KEYS
    gw = 4 * DIL_HD
    head = _iota_div((blk, gw), 1, DIL_HD)
    lane = lax.broadcasted_iota(I32, (blk, LANE), 1)
    for sub in range(qb // blk):
        rs = slice(sub * blk, (sub + 1) * blk)
        qs = q_ref[rs, :] * (DIL_HD ** -0.5)
        if sub == 0:
            k2 = jnp.concatenate([kp_ref[...], kc_ref[0:blk, :]], axis=0)
            v2 = jnp.concatenate([vp_ref[...], vc_ref[0:blk, :]], axis=0)
            bias_ref = b0_ref
        else:
            k2 = kc_ref[(sub - 1) * blk:(sub + 1) * blk, :]
            v2 = vc_ref[(sub - 1) * blk:(sub + 1) * blk, :]
            bias_ref = b1_ref
        st_old = None if first else st_in[rs, :]
        st_new = jnp.zeros((blk, LANE), F32)
        for g in range(DIL_HEADS * DIL_HD // gw):
            gs = slice(g * gw, (g + 1) * gw)
            qg, kg, vg = qs[:, gs], k2[:, gs], v2[:, gs]
            acc_old = None if first else acc_in[rs, gs]
            acc_new = jnp.zeros((blk, gw), F32)
            for hh in range(gw // DIL_HD):
                h = g * (gw // DIL_HD) + hh
                qm = jnp.where(head == hh, qg, jnp.zeros_like(qg))
                s = lax.dot_general(qm, kg, NT, preferred_element_type=F32) + bias_ref[h]
                m_new = jnp.max(s, axis=-1, keepdims=True)
                if not first:
                    m_old = st_old[:, h:h + 1]
                    m_new = jnp.maximum(m_old, m_new)
                    alpha = jnp.exp(m_old - m_new)
                p = jnp.exp(s - m_new)
                l_new = jnp.sum(p, axis=-1, keepdims=True)
                contrib = jnp.dot(p.astype(BF16), vg, preferred_element_type=F32)
                if not first:
                    l_new = alpha * st_old[:, DIL_HEADS + h:DIL_HEADS + h + 1] + l_new
                    contrib = alpha * acc_old + contrib
                if last:
                    contrib = contrib * (1.0 / l_new)
                else:
                    st_new = jnp.where(lane == h, m_new, st_new)
                    st_new = jnp.where(lane == DIL_HEADS + h, l_new, st_new)
                acc_new = jnp.where(head == hh, contrib, acc_new)
            if last:
                o_ref[rs, gs] = acc_new.astype(o_ref.dtype)
            else:
                acc_out[rs, gs] = acc_new
        if not last:
            st_out[rs, :] = st_new


def _dil_prompt_pass(pb, bias, state, batch, seq, d, qb, first, last):
    td = seq // d
    nq = td // qb
    r = qb // DIL_KEYS
    pbv = pb.reshape(batch, td, d * PB_W)
    qkv = lambda col: (lambda b, c, i: (b, i, c * 3 + col))
    prev = lambda col: (lambda b, c, i: (b, jnp.maximum(i * r - 1, 0), c * 3 + col))
    in_specs = [pl.BlockSpec((None, qb, DIL_W), qkv(0)),
                pl.BlockSpec((None, DIL_KEYS, DIL_W), prev(1)),
                pl.BlockSpec((None, qb, DIL_W), qkv(1)),
                pl.BlockSpec((None, DIL_KEYS, DIL_W), prev(2)),
                pl.BlockSpec((None, qb, DIL_W), qkv(2)),
                pl.BlockSpec((None, DIL_HEADS, DIL_KEYS, 2 * DIL_KEYS),
                             lambda b, c, i: (jnp.minimum(i, 1), 0, 0, 0)),
                pl.BlockSpec((None, DIL_HEADS, DIL_KEYS, 2 * DIL_KEYS), lambda b, c, i: (1, 0, 0, 0))]
    args = [pbv, pbv, pbv, pbv, pbv, bias, bias]
    acc_spec = pl.BlockSpec((None, qb, DIL_W), lambda b, c, i: (b, i, c))
    st_spec = pl.BlockSpec((None, qb, LANE), lambda b, c, i: (b, i, c))
    aliases = {}
    if not first:
        acc, st = state
        in_specs += [acc_spec, st_spec]
        args += [acc.reshape(batch, td, d * DIL_W), st.reshape(batch, td, d * LANE)]
        if not last:
            aliases = {7: 0, 8: 1}
    if last:
        out_specs = acc_spec
        out_shape = jax.ShapeDtypeStruct((batch, td, d * DIL_W), BF16)
    else:
        out_specs = [acc_spec, st_spec]
        out_shape = [jax.ShapeDtypeStruct((batch, td, d * DIL_W), F32),
                     jax.ShapeDtypeStruct((batch, td, d * LANE), F32)]
    out = pl.pallas_call(
        functools.partial(_dil_prompt_kernel, qb=qb, first=first, last=last),
        grid=(batch, d, nq),
        in_specs=in_specs, out_specs=out_specs, out_shape=out_shape,
        input_output_aliases=aliases,
        compiler_params=_params("parallel", "parallel", "arbitrary"),
        name=f"dil_prompt_d{d}",
    )(*args)
    if last:
        return out.reshape(batch * seq, DIL_W)
    return out[0].reshape(batch * seq, DIL_W), out[1].reshape(batch * seq, LANE)


def _dil_prompt(pb, batch, seq):
    state = None
    for n, (w, d) in enumerate(DIL_PATTERNS):
        assert w // d == DIL_KEYS and seq % (d * DIL_KEYS) == 0
        qb = min(2 * DIL_KEYS, seq // d)
        state = _dil_prompt_pass(pb, _dil_bias(d), state, batch, seq, d, qb,
                                 first=n == 0, last=n == len(DIL_PATTERNS) - 1)
    return state


def _dil_sample_bias(past, seq):
    t = np.arange(seq)[:, None]

    def table(dist, valid):
        mult = np.zeros(dist.shape, np.float64)
        for w, d in DIL_PATTERNS:
            mult += valid & (dist >= 0) & (dist <= w) & (dist % d == 0)
        with np.errstate(divide="ignore"):
            logm = np.log(mult)
        return np.concatenate([-sl * dist + logm for sl in _alibi_slopes()], axis=0).astype(np.float32)

    dist_c = past + t - np.arange(past)[None, :]
    c = np.arange(LANE)[None, :]
    return (jnp.asarray(table(dist_c, np.ones_like(dist_c, bool))),
            jnp.asarray(table(t - c, np.broadcast_to(c < seq, (seq, LANE)))))


def _dil_sample_kernel(q_ref, kn_ref, vn_ref, kc_ref, vc_ref, bc_ref, bn_ref, o_ref, *, seq):
    rows = DIL_HEADS * seq
    q = q_ref[...].astype(F32) * (DIL_HD ** -0.5)
    qrep = jnp.concatenate([q] * DIL_HEADS, axis=0)
    own = _iota_div((rows, DIL_W), 0, seq) == _iota_div((rows, DIL_W), 1, DIL_HD)
    qbd = jnp.where(own, qrep, 0.0).astype(BF16)
    pad = jnp.zeros((LANE - seq, DIL_W), F32)
    kn = jnp.concatenate([kn_ref[...], pad], axis=0).astype(BF16)
    vn = jnp.concatenate([vn_ref[...], pad], axis=0).astype(BF16)
    s_c = lax.dot_general(qbd, kc_ref[...].astype(BF16), NT, preferred_element_type=F32) + bc_ref[...]
    s_n = lax.dot_general(qbd, kn, NT, preferred_element_type=F32) + bn_ref[...]
    m = jnp.maximum(jnp.max(s_c, axis=-1, keepdims=True), jnp.max(s_n, axis=-1, keepdims=True))
    p_c = jnp.exp(s_c - m)
    p_n = jnp.exp(s_n - m)
    l = jnp.sum(p_c, axis=-1, keepdims=True) + jnp.sum(p_n, axis=-1, keepdims=True)
    o = (jnp.dot(p_c.astype(BF16), vc_ref[...].astype(BF16), preferred_element_type=F32)
         + jnp.dot(p_n.astype(BF16), vn, preferred_element_type=F32)) * (1.0 / l)
    o = jnp.where(own, o, 0.0)
    res = o[0:seq]
    for h in range(1, DIL_HEADS):
        res = res + o[h * seq:(h + 1) * seq]
    o_ref[...] = res


def _dil_sample(pb, kv, cache_k, cache_v, batch, seq):
    past = cache_k.shape[1]
    bias_c, bias_n = _dil_sample_bias(past, seq)
    rows = DIL_HEADS * seq
    return pl.pallas_call(
        functools.partial(_dil_sample_kernel, seq=seq),
        grid=(batch,),
        in_specs=[pl.BlockSpec((seq, DIL_W), lambda b: (b, 0)),
                  pl.BlockSpec((seq, DIL_W), lambda b: (b, 0)),
                  pl.BlockSpec((seq, DIL_W), lambda b: (b, 1)),
                  pl.BlockSpec((None, past, DIL_W), lambda b: (b, 0, 0)),
                  pl.BlockSpec((None, past, DIL_W), lambda b: (b, 0, 0)),
                  pl.BlockSpec((rows, past), lambda b: (0, 0)),
                  pl.BlockSpec((rows, LANE), lambda b: (0, 0))],
        out_specs=pl.BlockSpec((seq, DIL_W), lambda b: (b, 0)),
        out_shape=jax.ShapeDtypeStruct((batch * seq, DIL_W), F32),
        compiler_params=_params("parallel"),
        name="dil_sample",
    )(pb, kv, kv, cache_k, cache_v, bias_c, bias_n)


def _out_proj_kernel(oa_ref, ob_ref, x_ref, wo_ref, g_ref, wq_ref, h_ref, q_ref, *, qscale):
    o = jnp.concatenate([oa_ref[...].astype(BF16), ob_ref[...].astype(BF16)], axis=1)
    h = x_ref[...] + jnp.dot(o, wo_ref[...], preferred_element_type=F32)
    h_ref[...] = h
    n = _rms(h, g_ref[...]).astype(BF16)
    q_ref[...] = (jnp.dot(n, wq_ref[...], preferred_element_type=F32) * qscale).astype(q_ref.dtype)


def _out_proj(oa, ob, x, wo, g, wq, q_dtype, tm):
    n, d = x.shape
    row = lambda i: (i, 0)
    const = lambda i: (0, 0)
    return pl.pallas_call(
        functools.partial(_out_proj_kernel, qscale=(d // MEM_HEADS) ** -0.5),
        grid=(n // tm,),
        in_specs=[pl.BlockSpec((tm, GLA_V), row), pl.BlockSpec((tm, DIL_W), row), pl.BlockSpec((tm, d), row),
                  pl.BlockSpec(wo.shape, const), pl.BlockSpec((1, d), const), pl.BlockSpec(wq.shape, const)],
        out_specs=[pl.BlockSpec((tm, d), row), pl.BlockSpec((tm, d), row)],
        out_shape=[jax.ShapeDtypeStruct((n, d), F32), jax.ShapeDtypeStruct((n, d), q_dtype)],
        compiler_params=_params("parallel"),
        name="out_proj",
    )(oa, ob, x, wo, g, wq)


def _cross_kernel(q_ref, mk_ref, mv_ref, o_ref):
    hd = q_ref.shape[1] // MEM_HEADS
    for h in range(MEM_HEADS):
        sl = slice(h * hd, (h + 1) * hd)
        s = lax.dot_general(q_ref[:, sl].astype(BF16), mk_ref[:, sl].astype(BF16), NT, preferred_element_type=F32)
        p = jnp.exp(s - jnp.max(s, axis=-1, keepdims=True))
        l = jnp.sum(p, axis=-1, keepdims=True)
        o = jnp.dot(p.astype(BF16), mv_ref[:, sl].astype(BF16), preferred_element_type=F32) * (1.0 / l)
        o_ref[:, sl] = o.astype(o_ref.dtype)


def _cross(q, mk, mv, batch, seq, tq, out_dtype):
    n, d = q.shape
    nt = seq // tq
    nm = mk.shape[1]
    return pl.pallas_call(
        _cross_kernel,
        grid=(batch, nt),
        in_specs=[pl.BlockSpec((tq, d), lambda b, i: (b * nt + i, 0)),
                  pl.BlockSpec((None, nm, d), lambda b, i: (b, 0, 0)),
                  pl.BlockSpec((None, nm, d), lambda b, i: (b, 0, 0))],
        out_specs=pl.BlockSpec((tq, d), lambda b, i: (b * nt + i, 0)),
        out_shape=jax.ShapeDtypeStruct((n, d), out_dtype),
        compiler_params=_params("parallel", "parallel"),
        name="cross_attn",
    )(q, mk, mv)


def _co_proj_kernel(o_ref, h_ref, wo_ref, g_ref, wrh_ref, wrl_ref, br_ref, h2_ref, n3_ref, lg_ref):
    h2 = h_ref[...] + jnp.dot(o_ref[...].astype(BF16), wo_ref[...], preferred_element_type=F32)
    h2_ref[...] = h2
    n3 = _rms(h2, g_ref[...])
    n3_ref[...] = n3
    hi = n3.astype(BF16)
    lo = (n3 - hi.astype(F32)).astype(BF16)
    lg_ref[...] = (jnp.dot(hi, wrh_ref[...], preferred_element_type=F32)
                   + jnp.dot(lo, wrh_ref[...], preferred_element_type=F32)
                   + jnp.dot(hi, wrl_ref[...], preferred_element_type=F32) + br_ref[...])


def _co_proj(o, h, wo, g, wrh, wrl, br, tm):
    n, d = h.shape
    row = lambda i: (i, 0)
    const = lambda i: (0, 0)
    return pl.pallas_call(
        _co_proj_kernel,
        grid=(n // tm,),
        in_specs=[pl.BlockSpec((tm, d), row), pl.BlockSpec((tm, d), row), pl.BlockSpec(wo.shape, const),
                  pl.BlockSpec((1, d), const), pl.BlockSpec(wrh.shape, const), pl.BlockSpec(wrl.shape, const),
                  pl.BlockSpec((1, LANE), const)],
        out_specs=[pl.BlockSpec((tm, d), row), pl.BlockSpec((tm, d), row), pl.BlockSpec((tm, LANE), row)],
        out_shape=[jax.ShapeDtypeStruct((n, d), F32), jax.ShapeDtypeStruct((n, d), F32),
                   jax.ShapeDtypeStruct((n, LANE), F32)],
        compiler_params=_params("parallel"),
        name="co_proj",
    )(o, h, wo, g, wrh, wrl, br)


def _router_kernel(lg_ref, tri_ref, meta_ref, cnt_ref, run_ref):
    @pl.when(pl.program_id(0) == 0)
    def _():
        run_ref[...] = jnp.zeros_like(run_ref)

    lg = lg_ref[...]
    tm = lg.shape[0]
    lane = lax.broadcasted_iota(I32, (tm, LANE), 1)
    lane_f = lane.astype(F32)
    ninf = -jnp.inf
    first = lambda hit: jnp.min(jnp.where(hit, lane_f, float(LANE)), axis=-1, keepdims=True).astype(I32)

    gl = jnp.where(lane < N_GROUPS, lg, ninf)
    gmax = jnp.max(gl, axis=-1, keepdims=True)
    gidx = first(gl == gmax)
    pg = 1.0 / jnp.sum(jnp.exp(gl - gmax), axis=-1, keepdims=True)

    ex = lane - N_GROUPS
    el = jnp.where((ex >= 0) & (ex < N_EXPERTS) & ((ex >> int(math.log2(EXP_PER_GROUP))) == gidx), lg, ninf)
    t1 = jnp.max(el, axis=-1, keepdims=True)
    i1 = first(el == t1)
    el2 = jnp.where(lane == i1, ninf, el)
    t2 = jnp.max(el2, axis=-1, keepdims=True)
    i2 = first(el2 == t2)
    e = jnp.exp(t2 - t1)
    w0 = pg / (1.0 + e)
    w1 = pg * e / (1.0 + e)
    e0 = i1 - N_GROUPS
    e1 = i2 - N_GROUPS

    oh0 = lane == e0
    oh1 = lane == e1
    onehot = jnp.where(oh0 | oh1, 1.0, 0.0)
    before = jnp.dot(tri_ref[...], onehot.astype(BF16), preferred_element_type=F32) + run_ref[0:1, :]
    r0 = jnp.sum(jnp.where(oh0, before, 0.0), axis=-1, keepdims=True)
    r1 = jnp.sum(jnp.where(oh1, before, 0.0), axis=-1, keepdims=True)
    run_ref[...] = run_ref[...] + jnp.sum(onehot, axis=0, keepdims=True)
    cnt_ref[...] = run_ref[...]

    meta = jnp.zeros((tm, LANE), F32)
    for col, val in enumerate((e0.astype(F32), e1.astype(F32), r0, r1, w0, w1)):
        meta = jnp.where(lane == col, val, meta)
    meta_ref[...] = meta


def _router(logits, tm):
    n = logits.shape[0]
    tri = jnp.asarray(np.tril(np.ones((tm, tm), np.float32), -1), BF16)
    return pl.pallas_call(
        _router_kernel,
        grid=(n // tm,),
        in_specs=[pl.BlockSpec((tm, LANE), lambda i: (i, 0)), pl.BlockSpec((tm, tm), lambda i: (0, 0))],
        out_specs=[pl.BlockSpec((tm, LANE), lambda i: (i, 0)), pl.BlockSpec((8, LANE), lambda i: (0, 0))],
        out_shape=[jax.ShapeDtypeStruct((n, LANE), F32), jax.ShapeDtypeStruct((8, LANE), F32)],
        scratch_shapes=[pltpu.VMEM((8, LANE), F32)],
        compiler_params=_params("arbitrary"),
        name="router",
    )(logits, tri)


def _row_copy(src, s, dst, d, sem):
    return pltpu.make_async_copy(src.at[pl.ds(s, 1), :], dst.at[pl.ds(d, 1), :], sem)


def _dispatch_kernel(dest_ref, x_ref, xs_in_ref, xs_ref, sem):
    del xs_in_ref
    tm = x_ref.shape[0]

    def issue(t, carry):
        for j in range(TOP_K):
            _row_copy(x_ref, t, xs_ref, dest_ref[0, TOP_K * t + j], sem).start()
        return carry

    lax.fori_loop(0, tm, issue, 0)

    def drain(t, carry):
        for j in range(TOP_K):
            _row_copy(x_ref, 0, xs_ref, 0, sem).wait()
        return carry

    lax.fori_loop(0, tm, drain, 0)


def _dispatch(x, dest, rows, tm):
    n, d = x.shape
    return pl.pallas_call(
        _dispatch_kernel,
        grid=(n // tm,),
        in_specs=[pl.BlockSpec((None, 1, TOP_K * tm), lambda i: (i, 0, 0), memory_space=pltpu.SMEM),
                  pl.BlockSpec((tm, d), lambda i: (i, 0)),
                  pl.BlockSpec(memory_space=pl.ANY)],
        out_specs=pl.BlockSpec(memory_space=pl.ANY),
        out_shape=jax.ShapeDtypeStruct((rows, d), F32),
        scratch_shapes=[pltpu.SemaphoreType.DMA(())],
        input_output_aliases={2: 0},
        compiler_params=_params("arbitrary"),
        name="moe_dispatch",
    )(dest.reshape(n // tm, 1, TOP_K * tm), x, jnp.zeros((rows, d), F32))


def _expert_kernel(be_ref, nu_ref, x_ref, w1_ref, w3_ref, w2_ref, y_ref):
    del be_ref
    live = pl.program_id(0) < nu_ref[0]

    @pl.when(live)
    def _():
        x = x_ref[...].astype(BF16)
        a = jnp.dot(x, w1_ref[...], preferred_element_type=F32)
        b = jnp.dot(x, w3_ref[...], preferred_element_type=F32)
        y_ref[...] = jnp.dot((_silu(a) * b).astype(BF16), w2_ref[...], preferred_element_type=F32)

    @pl.when(jnp.logical_not(live))
    def _():
        y_ref[...] = jnp.zeros_like(y_ref)


def _experts(xs, block_e, n_used, w1, w3, w2, bm):
    rows, d = xs.shape
    de = w1.shape[2]
    return pl.pallas_call(
        _expert_kernel,
        grid_spec=pltpu.PrefetchScalarGridSpec(
            num_scalar_prefetch=2,
            grid=(rows // bm,),
            in_specs=[pl.BlockSpec((bm, d), lambda i, be, nu: (i, 0)),
                      pl.BlockSpec((None, d, de), lambda i, be, nu: (be[i], 0, 0)),
                      pl.BlockSpec((None, d, de), lambda i, be, nu: (be[i], 0, 0)),
                      pl.BlockSpec((None, de, d), lambda i, be, nu: (be[i], 0, 0))],
            out_specs=pl.BlockSpec((bm, d), lambda i, be, nu: (i, 0))),
        out_shape=jax.ShapeDtypeStruct((rows, d), F32),
        compiler_params=_params("arbitrary"),
        name="moe_experts",
    )(block_e, n_used, xs, w1, w3, w2)


def _combine_kernel(dest_ref, h_ref, meta_ref, g_ref, yb_ref, y_ref, buf_ref, sem):
    tm = h_ref.shape[0]

    def issue(t, carry):
        for j in range(TOP_K):
            _row_copy(yb_ref, dest_ref[0, TOP_K * t + j], buf_ref.at[j], t, sem).start()
        return carry

    lax.fori_loop(0, tm, issue, 0)

    def drain(t, carry):
        for j in range(TOP_K):
            _row_copy(yb_ref, 0, buf_ref.at[j], 0, sem).wait()
        return carry

    lax.fori_loop(0, tm, drain, 0)
    meta = meta_ref[...]
    moe = meta[:, 4:5] * buf_ref[0] + meta[:, 5:6] * buf_ref[1]
    y_ref[...] = _rms(h_ref[...] + moe, g_ref[...])


def _combine(h, meta, dest, yb, g, tm):
    n, d = h.shape
    return pl.pallas_call(
        _combine_kernel,
        grid=(n // tm,),
        in_specs=[pl.BlockSpec((None, 1, TOP_K * tm), lambda i: (i, 0, 0), memory_space=pltpu.SMEM),
                  pl.BlockSpec((tm, d), lambda i: (i, 0)),
                  pl.BlockSpec((tm, LANE), lambda i: (i, 0)),
                  pl.BlockSpec((1, d), lambda i: (0, 0)),
                  pl.BlockSpec(memory_space=pl.ANY)],
        out_specs=pl.BlockSpec((tm, d), lambda i: (i, 0)),
        out_shape=jax.ShapeDtypeStruct((n, d), F32),
        scratch_shapes=[pltpu.VMEM((TOP_K, tm, d), F32), pltpu.SemaphoreType.DMA(())],
        compiler_params=_params("arbitrary"),
        name="moe_combine",
    )(dest.reshape(n // tm, 1, TOP_K * tm), h, meta, g, yb)


def _moe_final(h2, n3, logits, w1, w3, w2, g_final, tm, bm):
    n = h2.shape[0]
    meta, cnt = _router(logits, tm)
    counts = cnt[0, :N_EXPERTS].astype(I32)
    padded = (counts + bm - 1) // bm * bm
    pad_end = jnp.cumsum(padded)
    pad_start = pad_end - padded
    expert = meta[:, 0:TOP_K].astype(I32)
    dest = (pad_start[expert] + meta[:, TOP_K:2 * TOP_K].astype(I32)).reshape(n * TOP_K)
    nb = -(-n * TOP_K // bm) + N_EXPERTS
    block_e = jnp.minimum(jnp.searchsorted(pad_end, jnp.arange(nb, dtype=I32) * bm, side="right"),
                          N_EXPERTS - 1).astype(I32)
    n_used = (pad_end[-1:] // bm).astype(I32)
    xs = _dispatch(n3, dest, nb * bm, tm)
    yb = _experts(xs, block_e, n_used, w1, w3, w2, bm)
    return _combine(h2, meta, dest, yb, g_final, tm)


def _tile(n, want):
    t = min(n, want)
    assert n % t == 0
    return t


def _group(x3, mk, mv, w, *, cache=None, state=None):
    batch, seq, d = x3.shape
    n = batch * seq
    x = x3.reshape(n, d)
    tm = _tile(n, 512)
    sample = cache is not None
    pa, pb, kv = _norm_proj(x, w["g1"], w["w_in"], F32 if sample else BF16, tm)
    if sample:
        oa, st = _gla_sample(pa, _state_to_blockdiag_t(state), w["wgk"], w["bgk"], w["gg"], batch, seq,
                             rows=GLA_CHUNK)
        ob = _dil_sample(pb, kv, cache[0].reshape(batch, -1, DIL_W), cache[1].reshape(batch, -1, DIL_W),
                         batch, seq)
    else:
        zeros = jnp.zeros((batch, GLA_V, GLA_QK), F32)
        oa, st = _gla_prompt(pa, zeros, w["wgk"], w["bgk"], w["gg"], batch, seq, tb=_tile(seq, 512))
        ob = _dil_prompt(pb, batch, seq)
    h1, qc = _out_proj(oa, ob, x, w["w_out"], w["g2"], w["w_cq"], F32 if sample else BF16, tm)
    oc = _cross(qc, mk, mv, batch, seq, _tile(seq, 512), F32 if sample else BF16)
    h2, n3, logits = _co_proj(oc, h1, w["w_co"], w["g3"], w["wr_hi"], w["wr_lo"], w["br"], tm)
    y = _moe_final(h2, n3, logits, w["w_e1"], w["w_e3"], w["w_e2"], w["g_final"], _tile(n, 256), 512)
    k_new = kv[:, :DIL_W].reshape(batch, seq, DIL_HEADS, DIL_HD)
    v_new = kv[:, DIL_W:].reshape(batch, seq, DIL_HEADS, DIL_HD)
    return y.reshape(batch, seq, d), _blockdiag_t_to_state(st), k_new, v_new


def _pack_weights(l, g_norm1, w_in, w_gk2, b_gk, g_gla_out, w_out, g_norm2, w_cq, w_co, g_norm3, w_gr, b_gr,
                  w_er, b_er, w_e1, w_e3, w_e2, g_final):
    d = w_in.shape[1]
    sp = np.cumsum([GLA_QK, GLA_QK, GLA_V, GATE_RANK, GLA_V, DIL_W, DIL_W, DIL_W])
    wi = w_in[l]
    q_a, k_a, v_a, glr, r_a, q_b, k_b, v_b = (wi[:, a:b] for a, b in zip([0, *sp[:-1]], sp))
    glr = jnp.pad(glr, ((0, 0), (0, LANE - GATE_RANK)))
    wr = jnp.pad(jnp.concatenate([w_gr[l], w_er[l]], axis=1), ((0, 0), (0, LANE - N_GROUPS - N_EXPERTS)))
    wr_hi = wr.astype(BF16)
    return dict(
        g1=g_norm1[l].reshape(1, d), g2=g_norm2[l].reshape(1, d), g3=g_norm3[l].reshape(1, d),
        g_final=g_final.reshape(1, d),
        w_in=jnp.concatenate([q_a, k_a, v_a, r_a, glr, q_b, k_b, v_b], axis=1).astype(BF16),
        wgk=jnp.pad(w_gk2[l], ((0, LANE - GATE_RANK), (0, 0))).astype(BF16),
        bgk=b_gk[l].reshape(1, GLA_QK), gg=g_gla_out[l].reshape(1, GLA_DV),
        w_out=w_out[l].astype(BF16), w_cq=w_cq[l].astype(BF16), w_co=w_co[l].astype(BF16),
        wr_hi=wr_hi, wr_lo=(wr - wr_hi.astype(F32)).astype(BF16),
        br=jnp.pad(jnp.concatenate([b_gr[l], b_er[l]]), (0, LANE - N_GROUPS - N_EXPERTS)).reshape(1, LANE),
        w_e1=w_e1[l].astype(BF16), w_e3=w_e3[l].astype(BF16), w_e2=w_e2[l].astype(BF16))


def kernel(x_prompt, x_sample, cache_swa_k, cache_swa_v, state_gla, cache_mem_k, cache_mem_v, mem_prompt,
           g_norm1, w_in, w_gk2, b_gk, g_gla_out, w_out, g_norm2, g_mem, w_cq, w_mk, w_mv, w_co,
           g_norm3, w_gr, b_gr, w_er, b_er, w_e1, w_e3, w_e2, g_final):
    depth = w_in.shape[0]
    assert depth == 1, "the final norm is fused into the last MoE stage; stacked layers are not supported"
    batch, seq, d = x_prompt.shape
    keep = min(DIL_PATTERNS[-1][0], seq)
    hp, hs = x_prompt, x_sample
    outs = [[] for _ in range(8)]
    for l in range(depth):
        w = _pack_weights(l, g_norm1, w_in, w_gk2, b_gk, g_gla_out, w_out, g_norm2, w_cq, w_co, g_norm3,
                          w_gr, b_gr, w_er, b_er, w_e1, w_e3, w_e2, g_final)
        nm = mem_prompt.shape[1]
        mkv = _norm_matmul(mem_prompt.reshape(batch * nm, d), g_mem[l].reshape(1, d),
                           jnp.concatenate([w_mk[l], w_mv[l]], axis=1).astype(BF16), _tile(batch * nm, 512))
        mk = mkv[:, :d].reshape(batch, nm, d)
        mv = mkv[:, d:].reshape(batch, nm, d)
        hp, sp, kp, vp = _group(hp, mk, mv, w)
        sb = x_sample.shape[0]
        hs, ss, kn, vn = _group(hs, cache_mem_k[l].reshape(sb, nm, d), cache_mem_v[l].reshape(sb, nm, d), w,
                                cache=(cache_swa_k[l], cache_swa_v[l]), state=state_gla[l])
        for lst, val in zip(outs, (kp[:, seq - keep:], vp[:, seq - keep:], sp,
                                   mk.reshape(batch, nm, MEM_HEADS, d // MEM_HEADS),
                                   mv.reshape(batch, nm, MEM_HEADS, d // MEM_HEADS), kn, vn, ss)):
            lst.append(val)
    return (hp, hs, *(jnp.stack(o) for o in outs))
```

```python
import functools
import math

import numpy as np
import jax
import jax.numpy as jnp
from jax import lax
from jax.experimental import pallas as pl
from jax.experimental.pallas import tpu as pltpu

F32, BF16, I32 = jnp.float32, jnp.bfloat16, jnp.int32

GLA_HEADS, GLA_DK, GLA_DV = 4, 64, 128
GLA_QK, GLA_V = GLA_HEADS * GLA_DK, GLA_HEADS * GLA_DV
GATE_RANK, GATE_NORM, GLA_CHUNK = 16, 16.0, 64
DIL_HEADS, DIL_HD = 8, 64
DIL_W = DIL_HEADS * DIL_HD
DIL_PATTERNS = ((128, 1), (512, 4), (2048, 16))
DIL_KEYS = 128
MEM_HEADS = 4
N_GROUPS, EXP_PER_GROUP, TOP_K = 4, 8, 2
N_EXPERTS = N_GROUPS * EXP_PER_GROUP
EPS = 1e-6

LANE = 128
SUBLANE = 8
VMEM_LIMIT_BYTES = 56 * 1024 * 1024

PA_W = 2 * GLA_QK + 2 * GLA_V + LANE
PB_W = 3 * DIL_W
HALF_W = DIL_W // 2
HI_MASK = -65536

TOKEN_TILE = 512
MOE_TILE = 256
MOE_BLOCK = 512
MOE_CHUNK = SUBLANE

NT = (((1,), (1,)), ((), ()))
TN = (((0,), (0,)), ((), ()))


def _params(*sem):
    return pltpu.CompilerParams(dimension_semantics=sem, vmem_limit_bytes=VMEM_LIMIT_BYTES)


def _tile(n, want):
    t = min(n, want)
    assert n % t == 0
    return t


def _rms(x, g):
    y = x * lax.rsqrt(jnp.mean(x * x, axis=-1, keepdims=True) + EPS)
    return y * g


def _silu(x):
    return x / (1.0 + jnp.exp(-x))


def _iota_div(shape, dim, n):
    assert n & (n - 1) == 0
    return lax.broadcasted_iota(I32, shape, dim) >> int(math.log2(n))


def _pack_halves(r):
    u = lax.bitcast_convert_type(r.astype(BF16).astype(F32), I32)
    lo = u[:, :HALF_W]
    return lax.shift_right_logical(lo, jnp.full_like(lo, 16)) | (u[:, HALF_W:] & HI_MASK)


def _unpack_halves(w):
    lo = lax.bitcast_convert_type(w << 16, F32).astype(BF16)
    hi = lax.bitcast_convert_type(w & HI_MASK, F32).astype(BF16)
    return lo, hi


def _norm_proj_kernel(x_ref, g_ref, w_ref, pa_ref, pb_ref, k_ref, v_ref, *, pack):
    n = _rms(x_ref[...], g_ref[...]).astype(BF16)
    for c0 in range(0, PA_W, DIL_W):
        c1 = min(c0 + DIL_W, PA_W)
        pa_ref[:, c0:c1] = jnp.dot(n, w_ref[:, c0:c1], preferred_element_type=F32).astype(pa_ref.dtype)
    for j, kv_ref in enumerate((None, k_ref, v_ref)):
        r = jnp.dot(n, w_ref[:, PA_W + j * DIL_W:PA_W + (j + 1) * DIL_W], preferred_element_type=F32)
        if pack:
            words = _pack_halves(r)
            for part in range(HALF_W // LANE):
                pb_ref[j * (HALF_W // LANE) + part] = words[:, part * LANE:(part + 1) * LANE]
        else:
            pb_ref[:, j * DIL_W:(j + 1) * DIL_W] = r
        if kv_ref is not None:
            kv_ref[...] = r


def _norm_proj(x, g, w, batch, seq, keep, pack, tm):
    n, d = x.shape
    if keep == seq:
        kv_map = lambda i: (i, 0)
    else:
        nt, nk = seq // tm, keep // tm
        assert keep % tm == 0
        kv_map = lambda i: ((i // nt) * nk + jnp.maximum(i % nt - (nt - nk), 0), 0)
    if pack:
        nparts = 3 * HALF_W // LANE
        pb_spec = pl.BlockSpec((nparts, tm, LANE), lambda i: (0, i, 0))
        pb_shape = jax.ShapeDtypeStruct((nparts, n, LANE), I32)
    else:
        pb_spec = pl.BlockSpec((tm, PB_W), lambda i: (i, 0))
        pb_shape = jax.ShapeDtypeStruct((n, PB_W), F32)
    pa_dtype = BF16 if pack else F32
    return pl.pallas_call(
        functools.partial(_norm_proj_kernel, pack=pack),
        grid=(n // tm,),
        in_specs=[pl.BlockSpec((tm, d), lambda i: (i, 0)),
                  pl.BlockSpec((1, d), lambda i: (0, 0)),
                  pl.BlockSpec((d, PA_W + PB_W), lambda i: (0, 0))],
        out_specs=[pl.BlockSpec((tm, PA_W), lambda i: (i, 0)),
                   pb_spec,
                   pl.BlockSpec((tm, DIL_W), kv_map),
                   pl.BlockSpec((tm, DIL_W), kv_map)],
        out_shape=[jax.ShapeDtypeStruct((n, PA_W), pa_dtype),
                   pb_shape,
                   jax.ShapeDtypeStruct((batch * keep, DIL_W), F32),
                   jax.ShapeDtypeStruct((batch * keep, DIL_W), F32)],
        compiler_params=_params("arbitrary"),
        name="norm_proj",
    )(x, g, w)


def _norm_matmul_kernel(x_ref, g_ref, w_ref, o_ref):
    n = _rms(x_ref[...], g_ref[...]).astype(BF16)
    for c0 in range(0, o_ref.shape[1], DIL_W):
        o_ref[:, c0:c0 + DIL_W] = jnp.dot(n, w_ref[:, c0:c0 + DIL_W], preferred_element_type=F32)


def _norm_matmul(x, g, w, tm):
    n, d = x.shape
    m = w.shape[1]
    return pl.pallas_call(
        _norm_matmul_kernel,
        grid=(n // tm,),
        in_specs=[pl.BlockSpec((tm, d), lambda i: (i, 0)),
                  pl.BlockSpec((1, d), lambda i: (0, 0)),
                  pl.BlockSpec((d, m), lambda i: (0, 0))],
        out_specs=pl.BlockSpec((tm, m), lambda i: (i, 0)),
        out_shape=jax.ShapeDtypeStruct((n, m), F32),
        compiler_params=_params("parallel"),
        name="norm_matmul",
    )(x, g, w)


def _gla_tables(chunk, seg):
    idx = np.arange(chunk)
    tril = ((idx[None, :] <= idx[:, None]) & (idx[None, :] // seg == idx[:, None] // seg)).astype(np.float32)
    masks, levels = [], []
    s = seg // 2
    while s >= 1:
        same = (idx[:, None] // (2 * s)) == (idx[None, :] // (2 * s))
        masks.append(same & ((idx[:, None] // s) % 2 == 1) & ((idx[None, :] // s) % 2 == 0))
        levels.append(s)
        s //= 2
    masks.append(idx[:, None] == idx[None, :])
    pm = np.tile(np.stack(masks).astype(np.float32), (1, 1, GLA_HEADS))
    return jnp.asarray(tril, BF16), jnp.asarray(pm), tuple(levels)


def _block_row(b, blk, idx):
    c, w = b.shape
    b3 = b.reshape(c // blk, blk, w)
    return jnp.broadcast_to(b3[:, idx:idx + 1, :], (c // blk, blk, w)).reshape(c, w)


def _level_ref(b, s, row):
    c = b.shape[0]
    if 2 * s >= SUBLANE:
        return _block_row(b, 2 * s, s - 1)
    down = lambda n: pltpu.roll(b, n, 0)
    if s == 2:
        m = row & 3
        return jnp.where(m == 0, pltpu.roll(b, c - 1, 0), jnp.where(m == 1, b, jnp.where(m == 2, down(1), down(2))))
    assert s == 1
    return jnp.where((row & 1) == 1, down(1), b)


def _gla_chunk(q, k, v, glr, wgk, bgk, tril_ref, pm_ref, levels, seg):
    c = q.shape[0]
    nl = len(levels)
    gk = jnp.dot(glr, wgk, preferred_element_type=F32) + bgk
    la = (jnp.minimum(gk, 0.0) - jnp.log1p(jnp.exp(-jnp.abs(gk)))) * (1.0 / GATE_NORM)
    hi = la.astype(BF16)
    r1 = la - hi.astype(F32)
    mid = r1.astype(BF16)
    lo = (r1 - mid.astype(F32)).astype(BF16)
    b3 = jnp.dot(tril_ref[...], jnp.concatenate([hi, mid, lo], axis=1), preferred_element_type=F32)
    b = b3[:, :GLA_QK] + b3[:, GLA_QK:2 * GLA_QK] + b3[:, 2 * GLA_QK:]
    b_end = _block_row(b, seg, seg - 1)

    row = lax.broadcasted_iota(I32, (c, GLA_QK), 0)
    khead = _iota_div((c, GLA_QK), 1, GLA_DK)
    vhead = _iota_div((c, GLA_V), 1, GLA_DV)

    def by_head(x, head):
        return jnp.concatenate([jnp.where(head == h, x, jnp.zeros_like(x)) for h in range(GLA_HEADS)], axis=0)

    def level(qe, ke, pm):
        a = lax.dot_general(qe, by_head(ke, khead), NT, preferred_element_type=F32)
        return jnp.where(pm > 0.0, a, 0.0)

    acc = level(q.astype(BF16), k.astype(BF16), pm_ref[nl])
    for l, s in enumerate(levels):
        ref = _level_ref(b, s, row)
        right = ((row >> int(math.log2(s))) & 1) == 1
        qe = (q * jnp.exp(jnp.where(right, b - ref, 0.0))).astype(BF16)
        ke = (k * jnp.exp(jnp.where(right, 0.0, ref - b))).astype(BF16)
        acc = acc + level(qe, ke, pm_ref[l])

    o = jnp.dot(acc.astype(BF16), by_head(v, vhead), preferred_element_type=F32)
    qb = (q * jnp.exp(b)).astype(BF16)
    kd = (k * jnp.exp(b_end - b)).astype(BF16)
    return o, qb, kd, b_end


def _gla_finish(o, r, gg):
    outs = []
    for h in range(GLA_HEADS):
        sl = slice(h * GLA_DV, (h + 1) * GLA_DV)
        outs.append(_rms(o[:, sl], gg) * _silu(r[:, sl]))
    return jnp.concatenate(outs, axis=1)


def _state_mask():
    return _iota_div((GLA_V, GLA_QK), 0, GLA_DV) == _iota_div((GLA_V, GLA_QK), 1, GLA_DK)


def _gla_prompt_kernel(q_ref, k_ref, v_ref, r_ref, glr_ref, wgk_ref, bgk_ref, gg_ref, tril_ref, pm_ref, s0_ref,
                       o_ref, sout_ref, st_ref, *, chunk, levels):
    i = pl.program_id(1)

    @pl.when(i == 0)
    def _():
        st_ref[...] = s0_ref[...]

    smask = _state_mask()

    def body(c, carry):
        rows = pl.ds(pl.multiple_of(c * chunk, chunk), chunk)
        q = q_ref[rows, :].astype(F32) * (GLA_DK ** -0.5)
        k = k_ref[rows, :].astype(F32)
        v = v_ref[rows, :]
        o, qb, kd, b_end = _gla_chunk(q, k, v, glr_ref[rows, :], wgk_ref[...], bgk_ref[...],
                                      tril_ref, pm_ref, levels, chunk)
        st = st_ref[...]
        o = o + lax.dot_general(qb, st.astype(BF16), NT, preferred_element_type=F32)
        u = lax.dot_general(v, kd, TN, preferred_element_type=F32)
        st_ref[...] = st * jnp.exp(b_end[0:1, :]) + jnp.where(smask, u, 0.0)
        o_ref[rows, :] = _gla_finish(o, r_ref[rows, :].astype(F32), gg_ref[...]).astype(o_ref.dtype)
        return carry

    lax.fori_loop(0, q_ref.shape[0] // chunk, body, 0)

    @pl.when(i == pl.num_programs(1) - 1)
    def _():
        sout_ref[...] = st_ref[...]


def _gla_prompt(pa, s0t, wgk, bgk, gg, batch, seq, tb):
    chunk = math.gcd(seq, GLA_CHUNK)
    tril, pm, levels = _gla_tables(chunk, chunk)
    nt = seq // tb
    row = lambda b, i: (b * nt + i, 0)
    const2 = lambda b, i: (0, 0)
    return pl.pallas_call(
        functools.partial(_gla_prompt_kernel, chunk=chunk, levels=levels),
        grid=(batch, nt),
        in_specs=[pl.BlockSpec((tb, GLA_QK), row),
                  pl.BlockSpec((tb, GLA_QK), lambda b, i: (b * nt + i, 1)),
                  pl.BlockSpec((tb, GLA_V), lambda b, i: (b * nt + i, 1)),
                  pl.BlockSpec((tb, GLA_V), lambda b, i: (b * nt + i, 2)),
                  pl.BlockSpec((tb, LANE), lambda b, i: (b * nt + i, (PA_W - LANE) // LANE)),
                  pl.BlockSpec(wgk.shape, const2),
                  pl.BlockSpec(bgk.shape, const2),
                  pl.BlockSpec(gg.shape, const2),
                  pl.BlockSpec(tril.shape, const2),
                  pl.BlockSpec(pm.shape, lambda b, i: (0, 0, 0)),
                  pl.BlockSpec((None, GLA_V, GLA_QK), lambda b, i: (b, 0, 0))],
        out_specs=[pl.BlockSpec((tb, GLA_V), row),
                   pl.BlockSpec((None, GLA_V, GLA_QK), lambda b, i: (b, 0, 0))],
        out_shape=[jax.ShapeDtypeStruct((batch * seq, GLA_V), BF16),
                   jax.ShapeDtypeStruct((batch, GLA_V, GLA_QK), F32)],
        scratch_shapes=[pltpu.VMEM((GLA_V, GLA_QK), F32)],
        compiler_params=_params("parallel", "arbitrary"),
        name="gla_prompt",
    )(pa, pa, pa, pa, pa, wgk, bgk, gg, tril, pm, s0t)


def _gla_sample_kernel(q_ref, k_ref, v_ref, r_ref, glr_ref, wgk_ref, bgk_ref, gg_ref, tril_ref, pm_ref, s0_ref,
                       o_ref, sout_ref, *, seg, levels):
    rows = q_ref.shape[0]
    q = q_ref[...] * (GLA_DK ** -0.5)
    v = v_ref[...].astype(BF16)
    o, qb, kd, b_end = _gla_chunk(q, k_ref[...], v, glr_ref[...].astype(BF16), wgk_ref[...], bgk_ref[...],
                                  tril_ref, pm_ref, levels, seg)
    smask = _state_mask()
    seq_o = _iota_div((rows, GLA_V), 0, seg)
    for j in range(rows // seg):
        st = s0_ref[j]
        oj = lax.dot_general(qb, st.astype(BF16), NT, preferred_element_type=F32)
        o = o + jnp.where(seq_o == j, oj, 0.0)
        u = lax.dot_general(jnp.where(seq_o == j, v, jnp.zeros_like(v)), kd, TN, preferred_element_type=F32)
        sout_ref[j] = st * jnp.exp(b_end[j * seg:j * seg + 1, :]) + jnp.where(smask, u, 0.0)
    o_ref[...] = _gla_finish(o, r_ref[...], gg_ref[...])


def _gla_sample(pa, s0t, wgk, bgk, gg, batch, seq, rows):
    seg = math.gcd(seq, GLA_CHUNK)
    assert seg == seq and seg % SUBLANE == 0, "sample sequences must be one sublane-aligned chunk"
    tril, pm, levels = _gla_tables(rows, seg)
    nseq = rows // seg
    row = lambda i: (i, 0)
    const2 = lambda i: (0, 0)
    return pl.pallas_call(
        functools.partial(_gla_sample_kernel, seg=seg, levels=levels),
        grid=(batch * seq // rows,),
        in_specs=[pl.BlockSpec((rows, GLA_QK), row),
                  pl.BlockSpec((rows, GLA_QK), lambda i: (i, 1)),
                  pl.BlockSpec((rows, GLA_V), lambda i: (i, 1)),
                  pl.BlockSpec((rows, GLA_V), lambda i: (i, 2)),
                  pl.BlockSpec((rows, LANE), lambda i: (i, (PA_W - LANE) // LANE)),
                  pl.BlockSpec(wgk.shape, const2),
                  pl.BlockSpec(bgk.shape, const2),
                  pl.BlockSpec(gg.shape, const2),
                  pl.BlockSpec(tril.shape, const2),
                  pl.BlockSpec(pm.shape, lambda i: (0, 0, 0)),
                  pl.BlockSpec((nseq, GLA_V, GLA_QK), lambda i: (i, 0, 0))],
        out_specs=[pl.BlockSpec((rows, GLA_V), row),
                   pl.BlockSpec((nseq, GLA_V, GLA_QK), lambda i: (i, 0, 0))],
        out_shape=[jax.ShapeDtypeStruct((batch * seq, GLA_V), F32),
                   jax.ShapeDtypeStruct((batch, GLA_V, GLA_QK), F32)],
        compiler_params=_params("parallel"),
        name="gla_sample",
    )(pa, pa, pa, pa, pa, wgk, bgk, gg, tril, pm, s0t)


def _state_to_blockdiag_t(s):
    b = s.shape[0]
    st = jnp.swapaxes(s, 2, 3)
    eye = jnp.eye(GLA_HEADS, dtype=s.dtype)
    return (st[:, :, :, None, :] * eye[None, :, None, :, None]).reshape(b, GLA_V, GLA_QK)


def _blockdiag_t_to_state(st):
    b = st.shape[0]
    s5 = st.reshape(b, GLA_HEADS, GLA_DV, GLA_HEADS, GLA_DK)
    return jnp.stack([jnp.swapaxes(s5[:, h, :, h, :], 1, 2) for h in range(GLA_HEADS)], axis=1)


def _alibi_slopes():
    return np.asarray([2.0 ** (-8.0 * (h + 1) / DIL_HEADS) for h in range(DIL_HEADS)], np.float64)


DIL_GROUP = 4
DIL_GW = DIL_GROUP * DIL_HD


def _dil_bias():
    i = np.arange(DIL_KEYS)[:, None]
    c = np.arange(2 * DIL_KEYS)[None, :]
    dist = DIL_KEYS + i - c
    ok = (dist >= 0) & (dist <= DIL_KEYS)
    out = np.empty((len(DIL_PATTERNS), DIL_HEADS, DIL_KEYS, 2 * DIL_KEYS), np.float32)
    for p, (_, d) in enumerate(DIL_PATTERNS):
        for h, sl in enumerate(_alibi_slopes()):
            out[p, h] = np.where(ok, -sl * (dist * d), -np.inf)
    return jnp.asarray(out.reshape(len(DIL_PATTERNS), DIL_HEADS // DIL_GROUP, DIL_GROUP * DIL_KEYS, 2 * DIL_KEYS))


def _dil_prompt_kernel(qkv_ref, bias_ref, o_ref, acc_ref, st_ref):
    seq = qkv_ref.shape[1]
    blk = DIL_KEYS
    ngroups = DIL_HEADS // DIL_GROUP
    wparts = HALF_W // LANE
    aparts = DIL_GW // LANE
    head = _iota_div((blk, DIL_GW), 1, DIL_HD)
    lane = lax.broadcasted_iota(I32, (blk, LANE), 1)

    def rows(start, n, d):
        return pl.ds(pl.multiple_of(start, blk), n) if d == 1 else pl.ds(start, n, stride=d)

    def stack(col):
        return jnp.concatenate(col, axis=0)

    def unpack(which, sel):
        return _unpack_halves(jnp.concatenate([qkv_ref[which * wparts + j, sel, :] for j in range(wparts)], axis=1))

    def block(p, d, start, has_prev):
        first, last = p == 0, p == len(DIL_PATTERNS) - 1
        qsel = rows(start, blk, d)
        ksel = rows(start - d * blk, 2 * blk, d) if has_prev else qsel
        q2, k2, v2 = unpack(0, qsel), unpack(1, ksel), unpack(2, ksel)
        st_old = None if first else st_ref[qsel, :]
        st_new = jnp.zeros((blk, LANE), F32)
        for g in range(ngroups):
            gs = slice(g * DIL_GW, (g + 1) * DIL_GW)
            qg = q2[g] * (DIL_HD ** -0.5)
            qst = jnp.concatenate([jnp.where(head == hh, qg, jnp.zeros_like(qg)) for hh in range(DIL_GROUP)], axis=0)
            bias = bias_ref[p, g] if has_prev else bias_ref[p, g, :, blk:2 * blk]
            s = lax.dot_general(qst, k2[g], NT, preferred_element_type=F32) + bias
            m_new = jnp.max(s, axis=-1, keepdims=True)
            if not first:
                m_old = stack([st_old[:, g * DIL_GROUP + hh:g * DIL_GROUP + hh + 1] for hh in range(DIL_GROUP)])
                l_old = stack([st_old[:, DIL_HEADS + g * DIL_GROUP + hh:DIL_HEADS + g * DIL_GROUP + hh + 1]
                               for hh in range(DIL_GROUP)])
                m_new = jnp.maximum(m_old, m_new)
                alpha = jnp.exp(m_old - m_new)
            pr = jnp.exp(s - m_new)
            l_new = jnp.sum(pr, axis=-1, keepdims=True)
            if not first:
                l_new = alpha * l_old + l_new
            pv = jnp.dot(pr.astype(BF16), v2[g], preferred_element_type=F32)
            acc_old = None if first else jnp.concatenate(
                [acc_ref[g * aparts + j, qsel, :] for j in range(aparts)], axis=1)
            new = jnp.zeros((blk, DIL_GW), F32)
            for hh in range(DIL_GROUP):
                rs = slice(hh * blk, (hh + 1) * blk)
                c = pv[rs]
                if not first:
                    c = alpha[rs] * acc_old + c
                if last:
                    c = c * (1.0 / l_new[rs])
                else:
                    st_new = jnp.where(lane == g * DIL_GROUP + hh, m_new[rs], st_new)
                    st_new = jnp.where(lane == DIL_HEADS + g * DIL_GROUP + hh, l_new[rs], st_new)
                new = jnp.where(head == hh, c, new)
            for j in range(aparts):
                acc_ref[g * aparts + j, qsel, :] = new[:, j * LANE:(j + 1) * LANE]
        if not last:
            st_ref[qsel, :] = st_new

    for p, (_, d) in enumerate(DIL_PATTERNS):
        nblk = seq // (d * blk)

        def first_block(r, carry, p=p, d=d):
            block(p, d, r, False)
            return carry

        def later_blocks(r, carry, p=p, d=d, nblk=nblk):
            def one(ib, carry):
                block(p, d, ib * (d * blk) + r, True)
                return carry
            return lax.fori_loop(1, nblk, one, carry)

        lax.fori_loop(0, d, first_block, 0)
        if nblk > 1:
            lax.fori_loop(0, d, later_blocks, 0)

    for j in range(DIL_W // LANE):
        o_ref[:, j * LANE:(j + 1) * LANE] = acc_ref[j].astype(o_ref.dtype)


def _dil_prompt(qkv, batch, seq):
    for w, d in DIL_PATTERNS:
        assert w // d == DIL_KEYS and seq % (d * DIL_KEYS) == 0
    bias = _dil_bias()
    return pl.pallas_call(
        _dil_prompt_kernel,
        grid=(batch,),
        in_specs=[pl.BlockSpec((qkv.shape[0], seq, LANE), lambda b: (0, b, 0)),
                  pl.BlockSpec(bias.shape, lambda b: (0, 0, 0, 0))],
        out_specs=pl.BlockSpec((seq, DIL_W), lambda b: (b, 0)),
        out_shape=jax.ShapeDtypeStruct((batch * seq, DIL_W), BF16),
        scratch_shapes=[pltpu.VMEM((DIL_W // LANE, seq, LANE), F32), pltpu.VMEM((seq, LANE), F32)],
        compiler_params=_params("parallel"),
        name="dil_prompt",
    )(qkv, bias)


def _dil_sample_bias(past, seq):
    t = np.arange(seq)[:, None]

    def table(dist, valid):
        mult = np.zeros(dist.shape, np.float64)
        for w, d in DIL_PATTERNS:
            mult += valid & (dist >= 0) & (dist <= w) & (dist % d == 0)
        with np.errstate(divide="ignore"):
            logm = np.log(mult)
        return np.concatenate([-sl * dist + logm for sl in _alibi_slopes()], axis=0).astype(np.float32)

    dist_c = past + t - np.arange(past)[None, :]
    c = np.arange(LANE)[None, :]
    return (jnp.asarray(table(dist_c, np.ones_like(dist_c, bool))),
            jnp.asarray(table(t - c, np.broadcast_to(c < seq, (seq, LANE)))))


def _dil_sample_kernel(q_ref, kn_ref, vn_ref, kc_ref, vc_ref, bc_ref, bn_ref, o_ref, *, seq):
    rows = DIL_HEADS * seq
    q = q_ref[...] * (DIL_HD ** -0.5)
    qrep = jnp.concatenate([q] * DIL_HEADS, axis=0)
    own = _iota_div((rows, DIL_W), 0, seq) == _iota_div((rows, DIL_W), 1, DIL_HD)
    qbd = jnp.where(own, qrep, 0.0).astype(BF16)
    pad = jnp.zeros((LANE - seq, DIL_W), F32)
    kn = jnp.concatenate([kn_ref[...], pad], axis=0).astype(BF16)
    vn = jnp.concatenate([vn_ref[...], pad], axis=0).astype(BF16)
    s_c = lax.dot_general(qbd, kc_ref[...].astype(BF16), NT, preferred_element_type=F32) + bc_ref[...]
    s_n = lax.dot_general(qbd, kn, NT, preferred_element_type=F32) + bn_ref[...]
    m = jnp.maximum(jnp.max(s_c, axis=-1, keepdims=True), jnp.max(s_n, axis=-1, keepdims=True))
    p_c = jnp.exp(s_c - m)
    p_n = jnp.exp(s_n - m)
    l = jnp.sum(p_c, axis=-1, keepdims=True) + jnp.sum(p_n, axis=-1, keepdims=True)
    o = (jnp.dot(p_c.astype(BF16), vc_ref[...].astype(BF16), preferred_element_type=F32)
         + jnp.dot(p_n.astype(BF16), vn, preferred_element_type=F32)) * (1.0 / l)
    o = jnp.where(own, o, 0.0)
    res = o[0:seq]
    for h in range(1, DIL_HEADS):
        res = res + o[h * seq:(h + 1) * seq]
    o_ref[...] = res


def _dil_sample(pb, k_new, v_new, cache_k, cache_v, batch, seq):
    past = cache_k.shape[1]
    bias_c, bias_n = _dil_sample_bias(past, seq)
    rows = DIL_HEADS * seq
    return pl.pallas_call(
        functools.partial(_dil_sample_kernel, seq=seq),
        grid=(batch,),
        in_specs=[pl.BlockSpec((seq, DIL_W), lambda b: (b, 0)),
                  pl.BlockSpec((seq, DIL_W), lambda b: (b, 0)),
                  pl.BlockSpec((seq, DIL_W), lambda b: (b, 0)),
                  pl.BlockSpec((None, past, DIL_W), lambda b: (b, 0, 0)),
                  pl.BlockSpec((None, past, DIL_W), lambda b: (b, 0, 0)),
                  pl.BlockSpec((rows, past), lambda b: (0, 0)),
                  pl.BlockSpec((rows, LANE), lambda b: (0, 0))],
        out_specs=pl.BlockSpec((seq, DIL_W), lambda b: (b, 0)),
        out_shape=jax.ShapeDtypeStruct((batch * seq, DIL_W), F32),
        compiler_params=_params("parallel"),
        name="dil_sample",
    )(pb, k_new, v_new, cache_k, cache_v, bias_c, bias_n)


def _out_proj_kernel(oa_ref, ob_ref, x_ref, wo_ref, g_ref, wq_ref, h_ref, q_ref, *, qscale):
    o = jnp.concatenate([oa_ref[...].astype(BF16), ob_ref[...].astype(BF16)], axis=1)
    h = x_ref[...] + jnp.dot(o, wo_ref[...], preferred_element_type=F32)
    h_ref[...] = h
    n = _rms(h, g_ref[...]).astype(BF16)
    q_ref[...] = (jnp.dot(n, wq_ref[...], preferred_element_type=F32) * qscale).astype(q_ref.dtype)


def _out_proj(oa, ob, x, wo, g, wq, q_dtype, tm):
    n, d = x.shape
    row = lambda i: (i, 0)
    const = lambda i: (0, 0)
    return pl.pallas_call(
        functools.partial(_out_proj_kernel, qscale=(d // MEM_HEADS) ** -0.5),
        grid=(n // tm,),
        in_specs=[pl.BlockSpec((tm, GLA_V), row), pl.BlockSpec((tm, DIL_W), row), pl.BlockSpec((tm, d), row),
                  pl.BlockSpec(wo.shape, const), pl.BlockSpec((1, d), const), pl.BlockSpec(wq.shape, const)],
        out_specs=[pl.BlockSpec((tm, d), row), pl.BlockSpec((tm, d), row)],
        out_shape=[jax.ShapeDtypeStruct((n, d), F32), jax.ShapeDtypeStruct((n, d), q_dtype)],
        compiler_params=_params("parallel"),
        name="out_proj",
    )(oa, ob, x, wo, g, wq)


def _cross_kernel(q_ref, mk_ref, mv_ref, o_ref):
    hd = q_ref.shape[1] // MEM_HEADS
    for h in range(MEM_HEADS):
        sl = slice(h * hd, (h + 1) * hd)
        s = lax.dot_general(q_ref[:, sl].astype(BF16), mk_ref[:, sl].astype(BF16), NT, preferred_element_type=F32)
        p = jnp.exp(s - jnp.max(s, axis=-1, keepdims=True))
        l = jnp.sum(p, axis=-1, keepdims=True)
        o = jnp.dot(p.astype(BF16), mv_ref[:, sl].astype(BF16), preferred_element_type=F32) * (1.0 / l)
        o_ref[:, sl] = o.astype(o_ref.dtype)


def _cross(q, mk, mv, batch, seq, tq, out_dtype):
    n, d = q.shape
    nt = seq // tq
    nm = mk.shape[1]
    return pl.pallas_call(
        _cross_kernel,
        grid=(batch, nt),
        in_specs=[pl.BlockSpec((tq, d), lambda b, i: (b * nt + i, 0)),
                  pl.BlockSpec((None, nm, d), lambda b, i: (b, 0, 0)),
                  pl.BlockSpec((None, nm, d), lambda b, i: (b, 0, 0))],
        out_specs=pl.BlockSpec((tq, d), lambda b, i: (b * nt + i, 0)),
        out_shape=jax.ShapeDtypeStruct((n, d), out_dtype),
        compiler_params=_params("parallel", "parallel"),
        name="cross_attn",
    )(q, mk, mv)


def _co_proj_kernel(o_ref, h_ref, wo_ref, g_ref, wrh_ref, wrl_ref, br_ref, h2_ref, n3_ref, lg_ref):
    h2 = h_ref[...] + jnp.dot(o_ref[...].astype(BF16), wo_ref[...], preferred_element_type=F32)
    h2_ref[...] = h2
    n3 = _rms(h2, g_ref[...])
    hi = n3.astype(BF16)
    lo = (n3 - hi.astype(F32)).astype(BF16)
    n3_ref[...] = hi
    lg_ref[...] = (jnp.dot(hi, wrh_ref[...], preferred_element_type=F32)
                   + jnp.dot(lo, wrh_ref[...], preferred_element_type=F32)
                   + jnp.dot(hi, wrl_ref[...], preferred_element_type=F32) + br_ref[...])


def _co_proj(o, h, wo, g, wrh, wrl, br, tm):
    n, d = h.shape
    row = lambda i: (i, 0)
    const = lambda i: (0, 0)
    return pl.pallas_call(
        _co_proj_kernel,
        grid=(n // tm,),
        in_specs=[pl.BlockSpec((tm, d), row), pl.BlockSpec((tm, d), row), pl.BlockSpec(wo.shape, const),
                  pl.BlockSpec((1, d), const), pl.BlockSpec(wrh.shape, const), pl.BlockSpec(wrl.shape, const),
                  pl.BlockSpec((1, LANE), const)],
        out_specs=[pl.BlockSpec((tm, d), row), pl.BlockSpec((tm, d), row), pl.BlockSpec((tm, LANE), row)],
        out_shape=[jax.ShapeDtypeStruct((n, d), F32), jax.ShapeDtypeStruct((n, d), BF16),
                   jax.ShapeDtypeStruct((n, LANE), F32)],
        compiler_params=_params("parallel"),
        name="co_proj",
    )(o, h, wo, g, wrh, wrl, br)


META_E, META_POS, META_W = 0, TOP_K, 2 * TOP_K


def _router_kernel(lg_ref, tri_ref, meta_ref, tile_ref, cnt_ref, run_ref, *, chunk):
    @pl.when(pl.program_id(0) == 0)
    def _():
        run_ref[...] = jnp.zeros_like(run_ref)

    lg = lg_ref[...]
    tm = lg.shape[0]
    lane = lax.broadcasted_iota(I32, (tm, LANE), 1)
    lane_f = lane.astype(F32)
    ninf = -jnp.inf
    first = lambda hit: jnp.min(jnp.where(hit, lane_f, float(LANE)), axis=-1, keepdims=True).astype(I32)

    gl = jnp.where(lane < N_GROUPS, lg, ninf)
    gmax = jnp.max(gl, axis=-1, keepdims=True)
    gidx = first(gl == gmax)
    pg = 1.0 / jnp.sum(jnp.exp(gl - gmax), axis=-1, keepdims=True)

    ex = lane - N_GROUPS
    el = jnp.where((ex >= 0) & (ex < N_EXPERTS) & ((ex >> int(math.log2(EXP_PER_GROUP))) == gidx), lg, ninf)
    t1 = jnp.max(el, axis=-1, keepdims=True)
    i1 = first(el == t1)
    el2 = jnp.where(lane == i1, ninf, el)
    t2 = jnp.max(el2, axis=-1, keepdims=True)
    i2 = first(el2 == t2)
    e = jnp.exp(t2 - t1)
    gates = (pg / (1.0 + e), pg * e / (1.0 + e))
    experts = (i1 - N_GROUPS, i2 - N_GROUPS)

    hits = [lane == ex_j for ex_j in experts]
    onehot = jnp.where(hits[0] | hits[1], 1.0, 0.0)
    before = jnp.dot(tri_ref[...], onehot.astype(BF16), preferred_element_type=F32)
    count = jnp.sum(onehot, axis=0, keepdims=True)
    slots = jnp.floor((count + (chunk - 1)) * (1.0 / chunk)) * chunk
    pos = [jnp.sum(jnp.where(lane < ex_j, slots, 0.0) + jnp.where(hit, before, 0.0), axis=-1, keepdims=True)
           for ex_j, hit in zip(experts, hits)]

    meta = jnp.zeros((tm, LANE), F32)
    for base, vals in ((META_E, [x.astype(F32) for x in experts]), (META_POS, pos), (META_W, gates)):
        for j, val in enumerate(vals):
            meta = jnp.where(lane == base + j, val, meta)
    meta_ref[...] = meta

    sub = lax.broadcasted_iota(I32, (SUBLANE, LANE), 0)
    tile_ref[...] = jnp.where(sub == 0, run_ref[...], jnp.where(sub == 1, slots, 0.0))
    run_ref[...] = run_ref[...] + slots
    cnt_ref[...] = run_ref[...]


def _router(logits, tm, chunk):
    n = logits.shape[0]
    tri = jnp.asarray(np.tril(np.ones((tm, tm), np.float32), -1), BF16)
    return pl.pallas_call(
        functools.partial(_router_kernel, chunk=chunk),
        grid=(n // tm,),
        in_specs=[pl.BlockSpec((tm, LANE), lambda i: (i, 0)), pl.BlockSpec((tm, tm), lambda i: (0, 0))],
        out_specs=[pl.BlockSpec((tm, LANE), lambda i: (i, 0)),
                   pl.BlockSpec((SUBLANE, LANE), lambda i: (i, 0)),
                   pl.BlockSpec((SUBLANE, LANE), lambda i: (0, 0))],
        out_shape=[jax.ShapeDtypeStruct((n, LANE), F32),
                   jax.ShapeDtypeStruct((n // tm * SUBLANE, LANE), F32),
                   jax.ShapeDtypeStruct((SUBLANE, LANE), F32)],
        scratch_shapes=[pltpu.VMEM((SUBLANE, LANE), F32)],
        compiler_params=_params("arbitrary"),
        name="router",
    )(logits, tri)


TAB_BASE, TAB_OFF, TAB_N, TAB_TOTAL = 0, N_EXPERTS, 2 * N_EXPERTS, 3 * N_EXPERTS
FILL_BASE, FILL_N, FILL_TAIL = 0, N_EXPERTS, 2 * N_EXPERTS
FILL_ROWS = 64


def _one_hot(meta, lane_pos, slot):
    return jnp.where(lane_pos == meta[:, META_POS + slot:META_POS + slot + 1], 1.0, 0.0).astype(BF16)


def _chunk_copies(tab_ref, sorted_ref, hbm_ref, sem, chunk, to_hbm):
    def copy(off, base):
        src, dst = sorted_ref.at[pl.ds(off, chunk), :], hbm_ref.at[pl.ds(base, chunk), :]
        return pltpu.make_async_copy(src, dst, sem) if to_hbm else pltpu.make_async_copy(dst, src, sem)

    def per_expert(e, carry):
        base, off = tab_ref[0, TAB_BASE + e], tab_ref[0, TAB_OFF + e]

        def one(c, carry):
            copy(pl.multiple_of(off + c * chunk, chunk), pl.multiple_of(base + c * chunk, chunk)).start()
            return carry

        return lax.fori_loop(0, tab_ref[0, TAB_N + e], one, carry)

    lax.fori_loop(0, N_EXPERTS, per_expert, 0)

    def drain(c, carry):
        copy(0, 0).wait()
        return carry

    lax.fori_loop(0, tab_ref[0, TAB_TOTAL], drain, 0)


def _dispatch_kernel(tab_ref, fill_ref, x_ref, meta_ref, xs_ref, sorted_ref, zero_ref, sem, *, chunk):
    tm, npos = x_ref.shape[0], sorted_ref.shape[0]
    meta = meta_ref[...]
    lane_pos = lax.broadcasted_iota(I32, (tm, npos), 1).astype(F32)
    place = _one_hot(meta, lane_pos, 0) + _one_hot(meta, lane_pos, 1)
    sorted_ref[...] = lax.dot_general(place, x_ref[...], TN, preferred_element_type=F32)
    _chunk_copies(tab_ref, sorted_ref, xs_ref, sem, chunk, to_hbm=True)

    @pl.when(pl.program_id(0) == pl.num_programs(0) - 1)
    def _():
        zero_ref[...] = jnp.zeros_like(zero_ref)
        big = zero_ref.shape[0]

        def fill(row, size):
            return pltpu.make_async_copy(zero_ref.at[pl.ds(0, size), :],
                                         xs_ref.at[pl.ds(pl.multiple_of(row, size), size), :], sem)

        def per_expert(e, total):
            def one(c, carry):
                fill(fill_ref[0, FILL_BASE + e] + c * chunk, chunk).start()
                return carry
            lax.fori_loop(0, fill_ref[0, FILL_N + e], one, 0)
            return total + fill_ref[0, FILL_N + e]

        def drain(c, carry):
            fill(0, chunk).wait()
            return carry

        lax.fori_loop(0, lax.fori_loop(0, N_EXPERTS, per_expert, 0), drain, 0)

        def tail(c, carry):
            fill(fill_ref[0, FILL_TAIL] + c * big, big).start()
            return carry

        def drain_tail(c, carry):
            fill(0, big).wait()
            return carry

        lax.fori_loop(0, fill_ref[0, FILL_TAIL + 1], tail, 0)
        lax.fori_loop(0, fill_ref[0, FILL_TAIL + 1], drain_tail, 0)


def _dispatch(x, meta, tab, fill, rows, tm, chunk):
    n, d = x.shape
    npos = TOP_K * tm + N_EXPERTS * chunk
    return pl.pallas_call(
        functools.partial(_dispatch_kernel, chunk=chunk),
        grid=(n // tm,),
        in_specs=[pl.BlockSpec((None, 1, LANE), lambda i: (i, 0, 0), memory_space=pltpu.SMEM),
                  pl.BlockSpec((1, LANE), lambda i: (0, 0), memory_space=pltpu.SMEM),
                  pl.BlockSpec((tm, d), lambda i: (i, 0)),
                  pl.BlockSpec((tm, LANE), lambda i: (i, 0))],
        out_specs=pl.BlockSpec(memory_space=pl.ANY),
        out_shape=jax.ShapeDtypeStruct((rows, d), F32),
        scratch_shapes=[pltpu.VMEM((npos, d), F32), pltpu.VMEM((FILL_ROWS, d), F32), pltpu.SemaphoreType.DMA(())],
        compiler_params=_params("arbitrary"),
        name="moe_dispatch",
    )(tab, fill, x, meta)


def _expert_kernel(be_ref, nu_ref, x_ref, w1_ref, w3_ref, w2_ref, y_ref):
    del be_ref
    live = pl.program_id(0) < nu_ref[0]

    @pl.when(live)
    def _():
        x = x_ref[...].astype(BF16)
        a = jnp.dot(x, w1_ref[...], preferred_element_type=F32)
        b = jnp.dot(x, w3_ref[...], preferred_element_type=F32)
        y_ref[...] = jnp.dot((_silu(a) * b).astype(BF16), w2_ref[...], preferred_element_type=F32)

    @pl.when(jnp.logical_not(live))
    def _():
        y_ref[...] = jnp.zeros_like(y_ref)


def _experts(xs, block_e, n_used, w1, w3, w2, bm):
    rows, d = xs.shape
    de = w1.shape[2]
    return pl.pallas_call(
        _expert_kernel,
        grid_spec=pltpu.PrefetchScalarGridSpec(
            num_scalar_prefetch=2,
            grid=(rows // bm,),
            in_specs=[pl.BlockSpec((bm, d), lambda i, be, nu: (i, 0)),
                      pl.BlockSpec((None, d, de), lambda i, be, nu: (be[i], 0, 0)),
                      pl.BlockSpec((None, d, de), lambda i, be, nu: (be[i], 0, 0)),
                      pl.BlockSpec((None, de, d), lambda i, be, nu: (be[i], 0, 0))],
            out_specs=pl.BlockSpec((bm, d), lambda i, be, nu: (i, 0))),
        out_shape=jax.ShapeDtypeStruct((rows, d), F32),
        compiler_params=_params("arbitrary"),
        name="moe_experts",
    )(block_e, n_used, xs, w1, w3, w2)


def _combine_kernel(tab_ref, h_ref, meta_ref, g_ref, yb_ref, y_ref, sorted_ref, sem, *, chunk):
    @pl.when(pl.program_id(0) == 0)
    def _():
        sorted_ref[...] = jnp.zeros_like(sorted_ref)

    _chunk_copies(tab_ref, sorted_ref, yb_ref, sem, chunk, to_hbm=False)
    tm, npos = h_ref.shape[0], sorted_ref.shape[0]
    meta = meta_ref[...]
    lane_pos = lax.broadcasted_iota(I32, (tm, npos), 1).astype(F32)
    yb = sorted_ref[...].astype(BF16)
    moe = (meta[:, META_W:META_W + 1] * jnp.dot(_one_hot(meta, lane_pos, 0), yb, preferred_element_type=F32)
           + meta[:, META_W + 1:META_W + 2] * jnp.dot(_one_hot(meta, lane_pos, 1), yb, preferred_element_type=F32))
    y_ref[...] = _rms(h_ref[...] + moe, g_ref[...])


def _combine(h, meta, tab, yb, g, tm, chunk):
    n, d = h.shape
    npos = TOP_K * tm + N_EXPERTS * chunk
    return pl.pallas_call(
        functools.partial(_combine_kernel, chunk=chunk),
        grid=(n // tm,),
        in_specs=[pl.BlockSpec((None, 1, LANE), lambda i: (i, 0, 0), memory_space=pltpu.SMEM),
                  pl.BlockSpec((tm, d), lambda i: (i, 0)),
                  pl.BlockSpec((tm, LANE), lambda i: (i, 0)),
                  pl.BlockSpec((1, d), lambda i: (0, 0)),
                  pl.BlockSpec(memory_space=pl.ANY)],
        out_specs=pl.BlockSpec((tm, d), lambda i: (i, 0)),
        out_shape=jax.ShapeDtypeStruct((n, d), F32),
        scratch_shapes=[pltpu.VMEM((npos, d), F32), pltpu.SemaphoreType.DMA(())],
        compiler_params=_params("arbitrary"),
        name="moe_combine",
    )(tab, h, meta, g, yb)


def _moe_final(h2, n3, logits, w1, w3, w2, g_final, tm, bm, chunk):
    n = h2.shape[0]
    nt = n // tm
    assert 3 * N_EXPERTS < LANE and bm % chunk == 0
    meta, tiles, cnt = _router(logits, tm, chunk)
    rows_e = cnt[0, :N_EXPERTS].astype(I32)
    padded = (rows_e + bm - 1) // bm * bm
    pad_end = jnp.cumsum(padded)
    pad_start = pad_end - padded
    tiles = tiles.reshape(nt, SUBLANE, LANE)
    run_before = tiles[:, 0, :N_EXPERTS].astype(I32)
    nchunks = tiles[:, 1, :N_EXPERTS].astype(I32) // chunk
    offs = chunk * (jnp.cumsum(nchunks, axis=1) - nchunks)
    pad_lanes = lambda a: jnp.pad(a, ((0, 0), (0, LANE - a.shape[1])))
    tab = pad_lanes(jnp.concatenate([pad_start[None, :] + run_before, offs, nchunks,
                                     jnp.sum(nchunks, axis=1, keepdims=True)], axis=1)).reshape(nt, 1, LANE)
    max_rows = n * TOP_K + (chunk - 1) * min(n * TOP_K, nt * N_EXPERTS) + N_EXPERTS * (bm - 1)
    nb = -(-max_rows // bm)
    assert bm % FILL_ROWS == 0
    fill = pad_lanes(jnp.concatenate([pad_start + rows_e, (padded - rows_e) // chunk,
                                      pad_end[-1:], (nb * bm - pad_end[-1:]) // FILL_ROWS])[None, :])
    block_e = jnp.minimum(jnp.searchsorted(pad_end, jnp.arange(nb, dtype=I32) * bm, side="right"),
                          N_EXPERTS - 1).astype(I32)
    n_used = (pad_end[-1:] // bm).astype(I32)
    xs = _dispatch(n3, meta, tab, fill, nb * bm, tm, chunk)
    yb = _experts(xs, block_e, n_used, w1, w3, w2, bm)
    return _combine(h2, meta, tab, yb, g_final, tm, chunk)


def _group(x3, mk, mv, w, keep, *, cache=None, state=None):
    batch, seq, d = x3.shape
    n = batch * seq
    x = x3.reshape(n, d)
    tm = _tile(n, TOKEN_TILE)
    sample = cache is not None
    pa, pb, k_new, v_new = _norm_proj(x, w["g1"], w["w_in"], batch, seq, keep, not sample, tm)
    if sample:
        assert keep == seq
        oa, st = _gla_sample(pa, _state_to_blockdiag_t(state), w["wgk"], w["bgk"], w["gg"], batch, seq,
                             rows=GLA_CHUNK)
        ob = _dil_sample(pb, k_new, v_new, cache[0].reshape(batch, -1, DIL_W), cache[1].reshape(batch, -1, DIL_W),
                         batch, seq)
    else:
        zeros = jnp.zeros((batch, GLA_V, GLA_QK), F32)
        oa, st = _gla_prompt(pa, zeros, w["wgk"], w["bgk"], w["gg"], batch, seq, tb=_tile(seq, TOKEN_TILE))
        ob = _dil_prompt(pb, batch, seq)
    h1, qc = _out_proj(oa, ob, x, w["w_out"], w["g2"], w["w_cq"], F32 if sample else BF16, tm)
    oc = _cross(qc, mk, mv, batch, seq, _tile(seq, TOKEN_TILE), F32 if sample else BF16)
    h2, n3, logits = _co_proj(oc, h1, w["w_co"], w["g3"], w["wr_hi"], w["wr_lo"], w["br"], tm)
    y = _moe_final(h2, n3, logits, w["w_e1"], w["w_e3"], w["w_e2"], w["g_final"], _tile(n, MOE_TILE), MOE_BLOCK,
                   MOE_CHUNK)
    return (y.reshape(batch, seq, d), _blockdiag_t_to_state(st),
            k_new.reshape(batch, keep, DIL_HEADS, DIL_HD), v_new.reshape(batch, keep, DIL_HEADS, DIL_HD))


def _pack_weights(l, g_norm1, w_in, w_gk2, b_gk, g_gla_out, w_out, g_norm2, w_cq, w_co, g_norm3, w_gr, b_gr,
                  w_er, b_er, w_e1, w_e3, w_e2, g_final):
    d = w_in.shape[1]
    sp = np.cumsum([GLA_QK, GLA_QK, GLA_V, GATE_RANK, GLA_V, DIL_W, DIL_W, DIL_W])
    wi = w_in[l]
    q_a, k_a, v_a, glr, r_a, q_b, k_b, v_b = (wi[:, a:b] for a, b in zip([0, *sp[:-1]], sp))
    glr = jnp.pad(glr, ((0, 0), (0, LANE - GATE_RANK)))
    wr = jnp.pad(jnp.concatenate([w_gr[l], w_er[l]], axis=1), ((0, 0), (0, LANE - N_GROUPS - N_EXPERTS)))
    wr_hi = wr.astype(BF16)
    return dict(
        g1=g_norm1[l].reshape(1, d), g2=g_norm2[l].reshape(1, d), g3=g_norm3[l].reshape(1, d),
        g_final=g_final.reshape(1, d),
        w_in=jnp.concatenate([q_a, k_a, v_a, r_a, glr, q_b, k_b, v_b], axis=1).astype(BF16),
        wgk=jnp.pad(w_gk2[l], ((0, LANE - GATE_RANK), (0, 0))).astype(BF16),
        bgk=b_gk[l].reshape(1, GLA_QK), gg=g_gla_out[l].reshape(1, GLA_DV),
        w_out=w_out[l].astype(BF16), w_cq=w_cq[l].astype(BF16), w_co=w_co[l].astype(BF16),
        wr_hi=wr_hi, wr_lo=(wr - wr_hi.astype(F32)).astype(BF16),
        br=jnp.pad(jnp.concatenate([b_gr[l], b_er[l]]), (0, LANE - N_GROUPS - N_EXPERTS)).reshape(1, LANE),
        w_e1=w_e1[l].astype(BF16), w_e3=w_e3[l].astype(BF16), w_e2=w_e2[l].astype(BF16))


def kernel(x_prompt, x_sample, cache_swa_k, cache_swa_v, state_gla, cache_mem_k, cache_mem_v, mem_prompt,
           g_norm1, w_in, w_gk2, b_gk, g_gla_out, w_out, g_norm2, g_mem, w_cq, w_mk, w_mv, w_co,
           g_norm3, w_gr, b_gr, w_er, b_er, w_e1, w_e3, w_e2, g_final):
    depth = w_in.shape[0]
    assert depth == 1, "the final norm is fused into the last MoE stage; stacked layers are not supported"
    batch, seq, d = x_prompt.shape
    sb, sseq, _ = x_sample.shape
    nm = mem_prompt.shape[1]
    keep = min(DIL_PATTERNS[-1][0], seq)
    l = 0
    w = _pack_weights(l, g_norm1, w_in, w_gk2, b_gk, g_gla_out, w_out, g_norm2, w_cq, w_co, g_norm3,
                      w_gr, b_gr, w_er, b_er, w_e1, w_e3, w_e2, g_final)
    mkv = _norm_matmul(mem_prompt.reshape(batch * nm, d), g_mem[l].reshape(1, d),
                       jnp.concatenate([w_mk[l], w_mv[l]], axis=1).astype(BF16), _tile(batch * nm, TOKEN_TILE))
    mk = mkv[:, :d].reshape(batch, nm, d)
    mv = mkv[:, d:].reshape(batch, nm, d)
    yp, sp, kp, vp = _group(x_prompt, mk, mv, w, keep)
    ys, ss, kn, vn = _group(x_sample, cache_mem_k[l].reshape(sb, nm, d), cache_mem_v[l].reshape(sb, nm, d), w, sseq,
                            cache=(cache_swa_k[l], cache_swa_v[l]), state=state_gla[l])
    heads = lambda m: m.reshape(batch, nm, MEM_HEADS, d // MEM_HEADS)
    return (yp, ys, *(o[None] for o in (kp, vp, sp, heads(mk), heads(mv), kn, vn, ss)))
```

```python
import functools
import math

import numpy as np
import jax
import jax.numpy as jnp
from jax import lax
from jax.experimental import pallas as pl
from jax.experimental.pallas import tpu as pltpu

F32, BF16, I32 = jnp.float32, jnp.bfloat16, jnp.int32

GLA_HEADS, GLA_DK, GLA_DV = 4, 64, 128
GLA_QK, GLA_V = GLA_HEADS * GLA_DK, GLA_HEADS * GLA_DV
GATE_RANK, GATE_NORM, GLA_CHUNK = 16, 16.0, 64
DIL_HEADS, DIL_HD = 8, 64
DIL_W = DIL_HEADS * DIL_HD
DIL_PATTERNS = ((128, 1), (512, 4), (2048, 16))
DIL_KEYS = 128
MEM_HEADS = 4
N_GROUPS, EXP_PER_GROUP, TOP_K = 4, 8, 2
N_EXPERTS = N_GROUPS * EXP_PER_GROUP
EPS = 1e-6

LANE = 128
SUBLANE = 8
VMEM_LIMIT_BYTES = 56 * 1024 * 1024

PA_W = 2 * GLA_QK + 2 * GLA_V + LANE
PB_W = 3 * DIL_W
HALF_W = DIL_W // 2
HI_MASK = -65536

TOKEN_TILE = 512
MOE_TILE = 256
MOE_BLOCK = 512
MOE_CHUNK = SUBLANE

NT = (((1,), (1,)), ((), ()))
TN = (((0,), (0,)), ((), ()))


def _params(*sem):
    return pltpu.CompilerParams(dimension_semantics=sem, vmem_limit_bytes=VMEM_LIMIT_BYTES)


def _tile(n, want):
    t = min(n, want)
    assert n % t == 0
    return t


def _rms(x, g):
    y = x * lax.rsqrt(jnp.mean(x * x, axis=-1, keepdims=True) + EPS)
    return y * g


def _silu(x):
    return x / (1.0 + jnp.exp(-x))


def _iota_div(shape, dim, n):
    assert n & (n - 1) == 0
    return lax.broadcasted_iota(I32, shape, dim) >> int(math.log2(n))


def _pack_halves(r):
    u = lax.bitcast_convert_type(r.astype(BF16).astype(F32), I32)
    lo = u[:, :HALF_W]
    return lax.shift_right_logical(lo, jnp.full_like(lo, 16)) | (u[:, HALF_W:] & HI_MASK)


def _unpack_halves(w):
    lo = lax.bitcast_convert_type(w << 16, F32).astype(BF16)
    hi = lax.bitcast_convert_type(w & HI_MASK, F32).astype(BF16)
    return lo, hi


def _norm_proj_kernel(x_ref, g_ref, w_ref, pa_ref, pb_ref, k_ref, v_ref, *, pack):
    n = _rms(x_ref[...], g_ref[...]).astype(BF16)
    for c0 in range(0, PA_W, DIL_W):
        c1 = min(c0 + DIL_W, PA_W)
        pa_ref[:, c0:c1] = jnp.dot(n, w_ref[:, c0:c1], preferred_element_type=F32).astype(pa_ref.dtype)
    for j, kv_ref in enumerate((None, k_ref, v_ref)):
        r = jnp.dot(n, w_ref[:, PA_W + j * DIL_W:PA_W + (j + 1) * DIL_W], preferred_element_type=F32)
        if pack:
            words = _pack_halves(r)
            for part in range(HALF_W // LANE):
                pb_ref[j * (HALF_W // LANE) + part] = words[:, part * LANE:(part + 1) * LANE]
        else:
            pb_ref[:, j * DIL_W:(j + 1) * DIL_W] = r
        if kv_ref is not None:
            kv_ref[...] = r


def _norm_proj(x, g, w, batch, seq, keep, pack, tm):
    n, d = x.shape
    if keep == seq:
        kv_map = lambda i: (i, 0)
    else:
        nt, nk = seq // tm, keep // tm
        assert keep % tm == 0
        kv_map = lambda i: ((i // nt) * nk + jnp.maximum(i % nt - (nt - nk), 0), 0)
    if pack:
        nparts = 3 * HALF_W // LANE
        pb_spec = pl.BlockSpec((nparts, tm, LANE), lambda i: (0, i, 0))
        pb_shape = jax.ShapeDtypeStruct((nparts, n, LANE), I32)
    else:
        pb_spec = pl.BlockSpec((tm, PB_W), lambda i: (i, 0))
        pb_shape = jax.ShapeDtypeStruct((n, PB_W), F32)
    pa_dtype = BF16 if pack else F32
    return pl.pallas_call(
        functools.partial(_norm_proj_kernel, pack=pack),
        grid=(n // tm,),
        in_specs=[pl.BlockSpec((tm, d), lambda i: (i, 0)),
                  pl.BlockSpec((1, d), lambda i: (0, 0)),
                  pl.BlockSpec((d, PA_W + PB_W), lambda i: (0, 0))],
        out_specs=[pl.BlockSpec((tm, PA_W), lambda i: (i, 0)),
                   pb_spec,
                   pl.BlockSpec((tm, DIL_W), kv_map),
                   pl.BlockSpec((tm, DIL_W), kv_map)],
        out_shape=[jax.ShapeDtypeStruct((n, PA_W), pa_dtype),
                   pb_shape,
                   jax.ShapeDtypeStruct((batch * keep, DIL_W), F32),
                   jax.ShapeDtypeStruct((batch * keep, DIL_W), F32)],
        compiler_params=_params("arbitrary"),
        name="norm_proj",
    )(x, g, w)


def _norm_matmul_kernel(x_ref, g_ref, w_ref, o_ref):
    n = _rms(x_ref[...], g_ref[...]).astype(BF16)
    for c0 in range(0, o_ref.shape[1], DIL_W):
        o_ref[:, c0:c0 + DIL_W] = jnp.dot(n, w_ref[:, c0:c0 + DIL_W], preferred_element_type=F32)


def _norm_matmul(x, g, w, tm):
    n, d = x.shape
    m = w.shape[1]
    return pl.pallas_call(
        _norm_matmul_kernel,
        grid=(n // tm,),
        in_specs=[pl.BlockSpec((tm, d), lambda i: (i, 0)),
                  pl.BlockSpec((1, d), lambda i: (0, 0)),
                  pl.BlockSpec((d, m), lambda i: (0, 0))],
        out_specs=pl.BlockSpec((tm, m), lambda i: (i, 0)),
        out_shape=jax.ShapeDtypeStruct((n, m), F32),
        compiler_params=_params("parallel"),
        name="norm_matmul",
    )(x, g, w)


def _gla_tables(chunk, seg):
    idx = np.arange(chunk)
    tril = ((idx[None, :] <= idx[:, None]) & (idx[None, :] // seg == idx[:, None] // seg)).astype(np.float32)
    masks, levels = [], []
    s = seg // 2
    while s >= 1:
        same = (idx[:, None] // (2 * s)) == (idx[None, :] // (2 * s))
        masks.append(same & ((idx[:, None] // s) % 2 == 1) & ((idx[None, :] // s) % 2 == 0))
        levels.append(s)
        s //= 2
    masks.append(idx[:, None] == idx[None, :])
    pm = np.tile(np.stack(masks).astype(np.float32), (1, 1, GLA_HEADS))
    return jnp.asarray(tril, BF16), jnp.asarray(pm), tuple(levels)


def _block_row(b, blk, idx):
    c, w = b.shape
    b3 = b.reshape(c // blk, blk, w)
    return jnp.broadcast_to(b3[:, idx:idx + 1, :], (c // blk, blk, w)).reshape(c, w)


def _level_ref(b, s, row):
    c = b.shape[0]
    if 2 * s >= SUBLANE:
        return _block_row(b, 2 * s, s - 1)
    down = lambda n: pltpu.roll(b, n, 0)
    if s == 2:
        m = row & 3
        return jnp.where(m == 0, pltpu.roll(b, c - 1, 0), jnp.where(m == 1, b, jnp.where(m == 2, down(1), down(2))))
    assert s == 1
    return jnp.where((row & 1) == 1, down(1), b)


def _gla_chunk(q, k, v, glr, wgk, bgk, tril_ref, pm_ref, levels, seg):
    c = q.shape[0]
    nl = len(levels)
    gk = jnp.dot(glr, wgk, preferred_element_type=F32) + bgk
    la = (jnp.minimum(gk, 0.0) - jnp.log1p(jnp.exp(-jnp.abs(gk)))) * (1.0 / GATE_NORM)
    hi = la.astype(BF16)
    r1 = la - hi.astype(F32)
    mid = r1.astype(BF16)
    lo = (r1 - mid.astype(F32)).astype(BF16)
    b3 = jnp.dot(tril_ref[...], jnp.concatenate([hi, mid, lo], axis=1), preferred_element_type=F32)
    b = b3[:, :GLA_QK] + b3[:, GLA_QK:2 * GLA_QK] + b3[:, 2 * GLA_QK:]
    b_end = _block_row(b, seg, seg - 1)

    row = lax.broadcasted_iota(I32, (c, GLA_QK), 0)
    khead = _iota_div((c, GLA_QK), 1, GLA_DK)
    vhead = _iota_div((c, GLA_V), 1, GLA_DV)

    def by_head(x, head):
        return jnp.concatenate([jnp.where(head == h, x, jnp.zeros_like(x)) for h in range(GLA_HEADS)], axis=0)

    def level(qe, ke, pm):
        a = lax.dot_general(qe, by_head(ke, khead), NT, preferred_element_type=F32)
        return jnp.where(pm > 0.0, a, 0.0)

    acc = level(q.astype(BF16), k.astype(BF16), pm_ref[nl])
    for l, s in enumerate(levels):
        ref = _level_ref(b, s, row)
        right = ((row >> int(math.log2(s))) & 1) == 1
        qe = (q * jnp.exp(jnp.where(right, b - ref, 0.0))).astype(BF16)
        ke = (k * jnp.exp(jnp.where(right, 0.0, ref - b))).astype(BF16)
        acc = acc + level(qe, ke, pm_ref[l])

    o = jnp.dot(acc.astype(BF16), by_head(v, vhead), preferred_element_type=F32)
    qb = (q * jnp.exp(b)).astype(BF16)
    kd = (k * jnp.exp(b_end - b)).astype(BF16)
    return o, qb, kd, b_end


def _gla_finish(o, r, gg):
    outs = []
    for h in range(GLA_HEADS):
        sl = slice(h * GLA_DV, (h + 1) * GLA_DV)
        outs.append(_rms(o[:, sl], gg) * _silu(r[:, sl]))
    return jnp.concatenate(outs, axis=1)


def _state_mask():
    return _iota_div((GLA_V, GLA_QK), 0, GLA_DV) == _iota_div((GLA_V, GLA_QK), 1, GLA_DK)


def _gla_prompt_kernel(q_ref, k_ref, v_ref, r_ref, glr_ref, wgk_ref, bgk_ref, gg_ref, tril_ref, pm_ref, s0_ref,
                       o_ref, sout_ref, st_ref, *, chunk, levels):
    i = pl.program_id(1)

    @pl.when(i == 0)
    def _():
        st_ref[...] = s0_ref[...]

    smask = _state_mask()

    def body(c, carry):
        rows = pl.ds(pl.multiple_of(c * chunk, chunk), chunk)
        q = q_ref[rows, :].astype(F32) * (GLA_DK ** -0.5)
        k = k_ref[rows, :].astype(F32)
        v = v_ref[rows, :]
        o, qb, kd, b_end = _gla_chunk(q, k, v, glr_ref[rows, :], wgk_ref[...], bgk_ref[...],
                                      tril_ref, pm_ref, levels, chunk)
        st = st_ref[...]
        o = o + lax.dot_general(qb, st.astype(BF16), NT, preferred_element_type=F32)
        u = lax.dot_general(v, kd, TN, preferred_element_type=F32)
        st_ref[...] = st * jnp.exp(b_end[0:1, :]) + jnp.where(smask, u, 0.0)
        o_ref[rows, :] = _gla_finish(o, r_ref[rows, :].astype(F32), gg_ref[...]).astype(o_ref.dtype)
        return carry

    lax.fori_loop(0, q_ref.shape[0] // chunk, body, 0, unroll=2)

    @pl.when(i == pl.num_programs(1) - 1)
    def _():
        sout_ref[...] = st_ref[...]


def _gla_prompt(pa, s0t, wgk, bgk, gg, batch, seq, tb):
    chunk = math.gcd(seq, GLA_CHUNK)
    tril, pm, levels = _gla_tables(chunk, chunk)
    nt = seq // tb
    row = lambda b, i: (b * nt + i, 0)
    const2 = lambda b, i: (0, 0)
    return pl.pallas_call(
        functools.partial(_gla_prompt_kernel, chunk=chunk, levels=levels),
        grid=(batch, nt),
        in_specs=[pl.BlockSpec((tb, GLA_QK), row),
                  pl.BlockSpec((tb, GLA_QK), lambda b, i: (b * nt + i, 1)),
                  pl.BlockSpec((tb, GLA_V), lambda b, i: (b * nt + i, 1)),
                  pl.BlockSpec((tb, GLA_V), lambda b, i: (b * nt + i, 2)),
                  pl.BlockSpec((tb, LANE), lambda b, i: (b * nt + i, (PA_W - LANE) // LANE)),
                  pl.BlockSpec(wgk.shape, const2),
                  pl.BlockSpec(bgk.shape, const2),
                  pl.BlockSpec(gg.shape, const2),
                  pl.BlockSpec(tril.shape, const2),
                  pl.BlockSpec(pm.shape, lambda b, i: (0, 0, 0)),
                  pl.BlockSpec((None, GLA_V, GLA_QK), lambda b, i: (b, 0, 0))],
        out_specs=[pl.BlockSpec((tb, GLA_V), row),
                   pl.BlockSpec((None, GLA_V, GLA_QK), lambda b, i: (b, 0, 0))],
        out_shape=[jax.ShapeDtypeStruct((batch * seq, GLA_V), BF16),
                   jax.ShapeDtypeStruct((batch, GLA_V, GLA_QK), F32)],
        scratch_shapes=[pltpu.VMEM((GLA_V, GLA_QK), F32)],
        compiler_params=_params("parallel", "arbitrary"),
        name="gla_prompt",
    )(pa, pa, pa, pa, pa, wgk, bgk, gg, tril, pm, s0t)


def _gla_sample_kernel(q_ref, k_ref, v_ref, r_ref, glr_ref, wgk_ref, bgk_ref, gg_ref, tril_ref, pm_ref, s0_ref,
                       o_ref, sout_ref, *, seg, levels):
    rows = q_ref.shape[0]
    q = q_ref[...] * (GLA_DK ** -0.5)
    v = v_ref[...].astype(BF16)
    o, qb, kd, b_end = _gla_chunk(q, k_ref[...], v, glr_ref[...].astype(BF16), wgk_ref[...], bgk_ref[...],
                                  tril_ref, pm_ref, levels, seg)
    smask = _state_mask()
    seq_o = _iota_div((rows, GLA_V), 0, seg)
    for j in range(rows // seg):
        st = s0_ref[j]
        oj = lax.dot_general(qb, st.astype(BF16), NT, preferred_element_type=F32)
        o = o + jnp.where(seq_o == j, oj, 0.0)
        u = lax.dot_general(jnp.where(seq_o == j, v, jnp.zeros_like(v)), kd, TN, preferred_element_type=F32)
        sout_ref[j] = st * jnp.exp(b_end[j * seg:j * seg + 1, :]) + jnp.where(smask, u, 0.0)
    o_ref[...] = _gla_finish(o, r_ref[...], gg_ref[...])


def _gla_sample(pa, s0t, wgk, bgk, gg, batch, seq, rows):
    seg = math.gcd(seq, GLA_CHUNK)
    assert seg == seq and seg % SUBLANE == 0, "sample sequences must be one sublane-aligned chunk"
    tril, pm, levels = _gla_tables(rows, seg)
    nseq = rows // seg
    row = lambda i: (i, 0)
    const2 = lambda i: (0, 0)
    return pl.pallas_call(
        functools.partial(_gla_sample_kernel, seg=seg, levels=levels),
        grid=(batch * seq // rows,),
        in_specs=[pl.BlockSpec((rows, GLA_QK), row),
                  pl.BlockSpec((rows, GLA_QK), lambda i: (i, 1)),
                  pl.BlockSpec((rows, GLA_V), lambda i: (i, 1)),
                  pl.BlockSpec((rows, GLA_V), lambda i: (i, 2)),
                  pl.BlockSpec((rows, LANE), lambda i: (i, (PA_W - LANE) // LANE)),
                  pl.BlockSpec(wgk.shape, const2),
                  pl.BlockSpec(bgk.shape, const2),
                  pl.BlockSpec(gg.shape, const2),
                  pl.BlockSpec(tril.shape, const2),
                  pl.BlockSpec(pm.shape, lambda i: (0, 0, 0)),
                  pl.BlockSpec((nseq, GLA_V, GLA_QK), lambda i: (i, 0, 0))],
        out_specs=[pl.BlockSpec((rows, GLA_V), row),
                   pl.BlockSpec((nseq, GLA_V, GLA_QK), lambda i: (i, 0, 0))],
        out_shape=[jax.ShapeDtypeStruct((batch * seq, GLA_V), F32),
                   jax.ShapeDtypeStruct((batch, GLA_V, GLA_QK), F32)],
        compiler_params=_params("parallel"),
        name="gla_sample",
    )(pa, pa, pa, pa, pa, wgk, bgk, gg, tril, pm, s0t)


def _state_to_blockdiag_t(s):
    b = s.shape[0]
    st = jnp.swapaxes(s, 2, 3)
    eye = jnp.eye(GLA_HEADS, dtype=s.dtype)
    return (st[:, :, :, None, :] * eye[None, :, None, :, None]).reshape(b, GLA_V, GLA_QK)


def _blockdiag_t_to_state(st):
    b = st.shape[0]
    s5 = st.reshape(b, GLA_HEADS, GLA_DV, GLA_HEADS, GLA_DK)
    return jnp.stack([jnp.swapaxes(s5[:, h, :, h, :], 1, 2) for h in range(GLA_HEADS)], axis=1)


def _alibi_slopes():
    return np.asarray([2.0 ** (-8.0 * (h + 1) / DIL_HEADS) for h in range(DIL_HEADS)], np.float64)


DIL_GROUP = 4
DIL_GW = DIL_GROUP * DIL_HD


def _dil_bias():
    i = np.arange(DIL_KEYS)[:, None]
    c = np.arange(2 * DIL_KEYS)[None, :]
    dist = DIL_KEYS + i - c
    ok = (dist >= 0) & (dist <= DIL_KEYS)
    out = np.empty((len(DIL_PATTERNS), DIL_HEADS, DIL_KEYS, 2 * DIL_KEYS), np.float32)
    for p, (_, d) in enumerate(DIL_PATTERNS):
        for h, sl in enumerate(_alibi_slopes()):
            out[p, h] = np.where(ok, -sl * (dist * d), -np.inf)
    out = out.reshape(len(DIL_PATTERNS), DIL_HEADS // DIL_GROUP, DIL_GROUP * DIL_KEYS, 2 * DIL_KEYS)
    return jnp.asarray(np.ascontiguousarray(out.transpose(0, 1, 3, 2)))


def _dil_prompt_kernel(qkv_hbm, biast_ref, o_ref, qkv_ref, acc_ref, m_ref, l_ref, sem):
    seq = qkv_ref.shape[1]
    blk = DIL_KEYS
    ngroups = DIL_HEADS // DIL_GROUP
    wparts = HALF_W // LANE
    aparts = DIL_GW // LANE
    head = _iota_div((blk, DIL_GW), 1, DIL_HD)

    load = pltpu.make_async_copy(qkv_hbm.at[:, pl.ds(pl.multiple_of(pl.program_id(0) * seq, seq), seq), :],
                                 qkv_ref, sem)
    load.start()
    load.wait()

    def rows(start, n, d):
        return pl.ds(pl.multiple_of(start, blk), n) if d == 1 else pl.ds(start, n, stride=d)

    def unpack(which, sel):
        return _unpack_halves(jnp.concatenate([qkv_ref[which * wparts + j, sel, :] for j in range(wparts)], axis=1))

    def by_head(x, keep):
        return jnp.concatenate([jnp.where(keep(hh), x, jnp.zeros_like(x)) for hh in range(DIL_GROUP)], axis=0)

    def stacked_q(q2, g):
        return by_head(q2[g] * (DIL_HD ** -0.5), lambda hh: head == hh)

    def find_max(p, d, start, has_prev):
        qsel = rows(start, blk, d)
        ksel = rows(start - d * blk, 2 * blk, d) if has_prev else qsel
        q2, k2 = unpack(0, qsel), unpack(1, ksel)
        per_head = []
        for g in range(ngroups):
            bias = biast_ref[p, g] if has_prev else biast_ref[p, g, blk:2 * blk, :]
            st = lax.dot_general(k2[g], stacked_q(q2, g), NT, preferred_element_type=F32) + bias
            mg = jnp.max(st, axis=0, keepdims=True)
            per_head += [mg[:, hh * blk:(hh + 1) * blk] for hh in range(DIL_GROUP)]
        rest = jnp.full((blk - DIL_HEADS, blk), -jnp.inf, F32)
        mt = jnp.concatenate(per_head + [rest], axis=0).T
        m_ref[qsel, :] = mt if p == 0 else jnp.maximum(m_ref[qsel, :], mt)

    def accumulate(p, d, start, has_prev):
        qsel = rows(start, blk, d)
        ksel = rows(start - d * blk, 2 * blk, d) if has_prev else qsel
        q2, k2, v2 = unpack(0, qsel), unpack(1, ksel), unpack(2, ksel)
        m_rows = m_ref[qsel, :].T
        per_head = []
        for g in range(ngroups):
            bias = biast_ref[p, g] if has_prev else biast_ref[p, g, blk:2 * blk, :]
            st = lax.dot_general(k2[g], stacked_q(q2, g), NT, preferred_element_type=F32) + bias
            m_g = jnp.concatenate([m_rows[g * DIL_GROUP + hh:g * DIL_GROUP + hh + 1, :] for hh in range(DIL_GROUP)],
                                  axis=1)
            pt = jnp.exp(st - m_g)
            l_g = jnp.sum(pt, axis=0, keepdims=True)
            per_head += [l_g[:, hh * blk:(hh + 1) * blk] for hh in range(DIL_GROUP)]
            out = lax.dot_general(pt.astype(BF16), v2[g], TN, preferred_element_type=F32)
            new = jnp.zeros((blk, DIL_GW), F32)
            for hh in range(DIL_GROUP):
                new = jnp.where(head == hh, out[hh * blk:(hh + 1) * blk], new)
            for j in range(aparts):
                part = new[:, j * LANE:(j + 1) * LANE]
                acc_ref[g * aparts + j, qsel, :] = part if p == 0 else acc_ref[g * aparts + j, qsel, :] + part
        rest = jnp.zeros((blk - DIL_HEADS, blk), F32)
        l_new = jnp.concatenate(per_head + [rest], axis=0).T
        l_ref[qsel, :] = l_new if p == 0 else l_ref[qsel, :] + l_new

    def sweep(block):
        for p, (_, d) in enumerate(DIL_PATTERNS):
            nblk = seq // (d * blk)

            def first_block(r, carry, p=p, d=d):
                block(p, d, r, False)
                return carry

            def later_blocks(r, carry, p=p, d=d, nblk=nblk):
                def one(ib, carry):
                    block(p, d, ib * (d * blk) + r, True)
                    return carry
                return lax.fori_loop(1, nblk, one, carry)

            lax.fori_loop(0, d, first_block, 0)
            if nblk > 1:
                lax.fori_loop(0, d, later_blocks, 0)

    sweep(find_max)
    sweep(accumulate)

    half = _iota_div((blk, LANE), 1, DIL_HD)

    def normalise(i, carry):
        sel = pl.ds(pl.multiple_of(i * blk, blk), blk)
        inv = 1.0 / l_ref[sel, :]
        for j in range(DIL_W // LANE):
            scale = jnp.where(half == 0, inv[:, 2 * j:2 * j + 1], inv[:, 2 * j + 1:2 * j + 2])
            o_ref[sel, j * LANE:(j + 1) * LANE] = (acc_ref[j, sel, :] * scale).astype(o_ref.dtype)
        return carry

    lax.fori_loop(0, seq // blk, normalise, 0)


def _dil_prompt(qkv, batch, seq):
    for w, d in DIL_PATTERNS:
        assert w // d == DIL_KEYS and seq % (d * DIL_KEYS) == 0
    biast = _dil_bias()
    return pl.pallas_call(
        _dil_prompt_kernel,
        grid=(batch,),
        in_specs=[pl.BlockSpec(memory_space=pl.ANY),
                  pl.BlockSpec(biast.shape, lambda b: (0, 0, 0, 0))],
        out_specs=pl.BlockSpec((seq, DIL_W), lambda b: (b, 0)),
        out_shape=jax.ShapeDtypeStruct((batch * seq, DIL_W), BF16),
        scratch_shapes=[pltpu.VMEM((qkv.shape[0], seq, LANE), I32),
                        pltpu.VMEM((DIL_W // LANE, seq, LANE), F32),
                        pltpu.VMEM((seq, LANE), F32), pltpu.VMEM((seq, LANE), F32),
                        pltpu.SemaphoreType.DMA(())],
        compiler_params=_params("arbitrary"),
        name="dil_prompt",
    )(qkv, biast)


def _dil_sample_bias(past, seq):
    t = np.arange(seq)[:, None]

    def table(dist, valid):
        mult = np.zeros(dist.shape, np.float64)
        for w, d in DIL_PATTERNS:
            mult += valid & (dist >= 0) & (dist <= w) & (dist % d == 0)
        with np.errstate(divide="ignore"):
            logm = np.log(mult)
        return np.concatenate([-sl * dist + logm for sl in _alibi_slopes()], axis=0).astype(np.float32)

    dist_c = past + t - np.arange(past)[None, :]
    c = np.arange(LANE)[None, :]
    return (jnp.asarray(table(dist_c, np.ones_like(dist_c, bool))),
            jnp.asarray(table(t - c, np.broadcast_to(c < seq, (seq, LANE)))))


def _dil_sample_kernel(q_ref, kn_ref, vn_ref, kc_ref, vc_ref, bc_ref, bn_ref, o_ref, *, seq):
    rows = DIL_HEADS * seq
    q = q_ref[...] * (DIL_HD ** -0.5)
    qrep = jnp.concatenate([q] * DIL_HEADS, axis=0)
    own = _iota_div((rows, DIL_W), 0, seq) == _iota_div((rows, DIL_W), 1, DIL_HD)
    qbd = jnp.where(own, qrep, 0.0).astype(BF16)
    pad = jnp.zeros((LANE - seq, DIL_W), F32)
    kn = jnp.concatenate([kn_ref[...], pad], axis=0).astype(BF16)
    vn = jnp.concatenate([vn_ref[...], pad], axis=0).astype(BF16)
    s_c = lax.dot_general(qbd, kc_ref[...].astype(BF16), NT, preferred_element_type=F32) + bc_ref[...]
    s_n = lax.dot_general(qbd, kn, NT, preferred_element_type=F32) + bn_ref[...]
    m = jnp.maximum(jnp.max(s_c, axis=-1, keepdims=True), jnp.max(s_n, axis=-1, keepdims=True))
    p_c = jnp.exp(s_c - m)
    p_n = jnp.exp(s_n - m)
    l = jnp.sum(p_c, axis=-1, keepdims=True) + jnp.sum(p_n, axis=-1, keepdims=True)
    o = (jnp.dot(p_c.astype(BF16), vc_ref[...].astype(BF16), preferred_element_type=F32)
         + jnp.dot(p_n.astype(BF16), vn, preferred_element_type=F32)) * (1.0 / l)
    o = jnp.where(own, o, 0.0)
    res = o[0:seq]
    for h in range(1, DIL_HEADS):
        res = res + o[h * seq:(h + 1) * seq]
    o_ref[...] = res


def _dil_sample(pb, k_new, v_new, cache_k, cache_v, batch, seq):
    past = cache_k.shape[1]
    bias_c, bias_n = _dil_sample_bias(past, seq)
    rows = DIL_HEADS * seq
    return pl.pallas_call(
        functools.partial(_dil_sample_kernel, seq=seq),
        grid=(batch,),
        in_specs=[pl.BlockSpec((seq, DIL_W), lambda b: (b, 0)),
                  pl.BlockSpec((seq, DIL_W), lambda b: (b, 0)),
                  pl.BlockSpec((seq, DIL_W), lambda b: (b, 0)),
                  pl.BlockSpec((None, past, DIL_W), lambda b: (b, 0, 0)),
                  pl.BlockSpec((None, past, DIL_W), lambda b: (b, 0, 0)),
                  pl.BlockSpec((rows, past), lambda b: (0, 0)),
                  pl.BlockSpec((rows, LANE), lambda b: (0, 0))],
        out_specs=pl.BlockSpec((seq, DIL_W), lambda b: (b, 0)),
        out_shape=jax.ShapeDtypeStruct((batch * seq, DIL_W), F32),
        compiler_params=_params("parallel"),
        name="dil_sample",
    )(pb, k_new, v_new, cache_k, cache_v, bias_c, bias_n)


def _out_proj_kernel(oa_ref, ob_ref, x_ref, wo_ref, g_ref, wq_ref, h_ref, q_ref, *, qscale):
    o = jnp.concatenate([oa_ref[...].astype(BF16), ob_ref[...].astype(BF16)], axis=1)
    h = x_ref[...] + jnp.dot(o, wo_ref[...], preferred_element_type=F32)
    h_ref[...] = h
    n = _rms(h, g_ref[...]).astype(BF16)
    q_ref[...] = (jnp.dot(n, wq_ref[...], preferred_element_type=F32) * qscale).astype(q_ref.dtype)


def _out_proj(oa, ob, x, wo, g, wq, q_dtype, tm):
    n, d = x.shape
    row = lambda i: (i, 0)
    const = lambda i: (0, 0)
    return pl.pallas_call(
        functools.partial(_out_proj_kernel, qscale=(d // MEM_HEADS) ** -0.5),
        grid=(n // tm,),
        in_specs=[pl.BlockSpec((tm, GLA_V), row), pl.BlockSpec((tm, DIL_W), row), pl.BlockSpec((tm, d), row),
                  pl.BlockSpec(wo.shape, const), pl.BlockSpec((1, d), const), pl.BlockSpec(wq.shape, const)],
        out_specs=[pl.BlockSpec((tm, d), row), pl.BlockSpec((tm, d), row)],
        out_shape=[jax.ShapeDtypeStruct((n, d), F32), jax.ShapeDtypeStruct((n, d), q_dtype)],
        compiler_params=_params("parallel"),
        name="out_proj",
    )(oa, ob, x, wo, g, wq)


def _cross_kernel(q_ref, mk_ref, mv_ref, o_ref):
    hd = q_ref.shape[1] // MEM_HEADS
    for h in range(MEM_HEADS):
        sl = slice(h * hd, (h + 1) * hd)
        s = lax.dot_general(q_ref[:, sl].astype(BF16), mk_ref[:, sl].astype(BF16), NT, preferred_element_type=F32)
        p = jnp.exp(s - jnp.max(s, axis=-1, keepdims=True))
        l = jnp.sum(p, axis=-1, keepdims=True)
        o = jnp.dot(p.astype(BF16), mv_ref[:, sl].astype(BF16), preferred_element_type=F32) * (1.0 / l)
        o_ref[:, sl] = o.astype(o_ref.dtype)


def _cross(q, mk, mv, batch, seq, tq, out_dtype):
    n, d = q.shape
    nt = seq // tq
    nm = mk.shape[1]
    return pl.pallas_call(
        _cross_kernel,
        grid=(batch, nt),
        in_specs=[pl.BlockSpec((tq, d), lambda b, i: (b * nt + i, 0)),
                  pl.BlockSpec((None, nm, d), lambda b, i: (b, 0, 0)),
                  pl.BlockSpec((None, nm, d), lambda b, i: (b, 0, 0))],
        out_specs=pl.BlockSpec((tq, d), lambda b, i: (b * nt + i, 0)),
        out_shape=jax.ShapeDtypeStruct((n, d), out_dtype),
        compiler_params=_params("parallel", "parallel"),
        name="cross_attn",
    )(q, mk, mv)


def _co_proj_kernel(o_ref, h_ref, wo_ref, g_ref, wrh_ref, wrl_ref, br_ref, h2_ref, n3_ref, lg_ref):
    h2 = h_ref[...] + jnp.dot(o_ref[...].astype(BF16), wo_ref[...], preferred_element_type=F32)
    h2_ref[...] = h2
    n3 = _rms(h2, g_ref[...])
    hi = n3.astype(BF16)
    lo = (n3 - hi.astype(F32)).astype(BF16)
    n3_ref[...] = hi
    lg_ref[...] = (jnp.dot(hi, wrh_ref[...], preferred_element_type=F32)
                   + jnp.dot(lo, wrh_ref[...], preferred_element_type=F32)
                   + jnp.dot(hi, wrl_ref[...], preferred_element_type=F32) + br_ref[...])


def _co_proj(o, h, wo, g, wrh, wrl, br, tm):
    n, d = h.shape
    row = lambda i: (i, 0)
    const = lambda i: (0, 0)
    return pl.pallas_call(
        _co_proj_kernel,
        grid=(n // tm,),
        in_specs=[pl.BlockSpec((tm, d), row), pl.BlockSpec((tm, d), row), pl.BlockSpec(wo.shape, const),
                  pl.BlockSpec((1, d), const), pl.BlockSpec(wrh.shape, const), pl.BlockSpec(wrl.shape, const),
                  pl.BlockSpec((1, LANE), const)],
        out_specs=[pl.BlockSpec((tm, d), row), pl.BlockSpec((tm, d), row), pl.BlockSpec((tm, LANE), row)],
        out_shape=[jax.ShapeDtypeStruct((n, d), F32), jax.ShapeDtypeStruct((n, d), BF16),
                   jax.ShapeDtypeStruct((n, LANE), F32)],
        compiler_params=_params("parallel"),
        name="co_proj",
    )(o, h, wo, g, wrh, wrl, br)


META_E, META_POS, META_W = 0, TOP_K, 2 * TOP_K


def _router_kernel(lg_ref, tri_ref, meta_ref, tile_ref, cnt_ref, run_ref, *, chunk):
    @pl.when(pl.program_id(0) == 0)
    def _():
        run_ref[...] = jnp.zeros_like(run_ref)

    lg = lg_ref[...]
    tm = lg.shape[0]
    lane = lax.broadcasted_iota(I32, (tm, LANE), 1)
    lane_f = lane.astype(F32)
    ninf = -jnp.inf
    first = lambda hit: jnp.min(jnp.where(hit, lane_f, float(LANE)), axis=-1, keepdims=True).astype(I32)

    gl = jnp.where(lane < N_GROUPS, lg, ninf)
    gmax = jnp.max(gl, axis=-1, keepdims=True)
    gidx = first(gl == gmax)
    pg = 1.0 / jnp.sum(jnp.exp(gl - gmax), axis=-1, keepdims=True)

    ex = lane - N_GROUPS
    el = jnp.where((ex >= 0) & (ex < N_EXPERTS) & ((ex >> int(math.log2(EXP_PER_GROUP))) == gidx), lg, ninf)
    t1 = jnp.max(el, axis=-1, keepdims=True)
    i1 = first(el == t1)
    el2 = jnp.where(lane == i1, ninf, el)
    t2 = jnp.max(el2, axis=-1, keepdims=True)
    i2 = first(el2 == t2)
    e = jnp.exp(t2 - t1)
    gates = (pg / (1.0 + e), pg * e / (1.0 + e))
    experts = (i1 - N_GROUPS, i2 - N_GROUPS)

    hits = [lane == ex_j for ex_j in experts]
    onehot = jnp.where(hits[0] | hits[1], 1.0, 0.0)
    before = jnp.dot(tri_ref[...], onehot.astype(BF16), preferred_element_type=F32)
    count = jnp.sum(onehot, axis=0, keepdims=True)
    slots = jnp.floor((count + (chunk - 1)) * (1.0 / chunk)) * chunk
    pos = [jnp.sum(jnp.where(lane < ex_j, slots, 0.0) + jnp.where(hit, before, 0.0), axis=-1, keepdims=True)
           for ex_j, hit in zip(experts, hits)]

    meta = jnp.zeros((tm, LANE), F32)
    for base, vals in ((META_E, [x.astype(F32) for x in experts]), (META_POS, pos), (META_W, gates)):
        for j, val in enumerate(vals):
            meta = jnp.where(lane == base + j, val, meta)
    meta_ref[...] = meta

    sub = lax.broadcasted_iota(I32, (SUBLANE, LANE), 0)
    tile_ref[...] = jnp.where(sub == 0, run_ref[...], jnp.where(sub == 1, slots, 0.0))
    run_ref[...] = run_ref[...] + slots
    cnt_ref[...] = run_ref[...]


def _router(logits, tm, chunk):
    n = logits.shape[0]
    tri = jnp.asarray(np.tril(np.ones((tm, tm), np.float32), -1), BF16)
    return pl.pallas_call(
        functools.partial(_router_kernel, chunk=chunk),
        grid=(n // tm,),
        in_specs=[pl.BlockSpec((tm, LANE), lambda i: (i, 0)), pl.BlockSpec((tm, tm), lambda i: (0, 0))],
        out_specs=[pl.BlockSpec((tm, LANE), lambda i: (i, 0)),
                   pl.BlockSpec((SUBLANE, LANE), lambda i: (i, 0)),
                   pl.BlockSpec((SUBLANE, LANE), lambda i: (0, 0))],
        out_shape=[jax.ShapeDtypeStruct((n, LANE), F32),
                   jax.ShapeDtypeStruct((n // tm * SUBLANE, LANE), F32),
                   jax.ShapeDtypeStruct((SUBLANE, LANE), F32)],
        scratch_shapes=[pltpu.VMEM((SUBLANE, LANE), F32)],
        compiler_params=_params("arbitrary"),
        name="router",
    )(logits, tri)


TAB_BASE, TAB_OFF, TAB_N, TAB_TOTAL = 0, N_EXPERTS, 2 * N_EXPERTS, 3 * N_EXPERTS
FILL_BASE, FILL_N, FILL_TAIL = 0, N_EXPERTS, 2 * N_EXPERTS
FILL_ROWS = 64


def _one_hot(meta, lane_pos, slot):
    return jnp.where(lane_pos == meta[:, META_POS + slot:META_POS + slot + 1], 1.0, 0.0).astype(BF16)


def _chunk_copies(tab_ref, sorted_ref, hbm_ref, sem, chunk, to_hbm):
    def copy(off, base):
        src, dst = sorted_ref.at[pl.ds(off, chunk), :], hbm_ref.at[pl.ds(base, chunk), :]
        return pltpu.make_async_copy(src, dst, sem) if to_hbm else pltpu.make_async_copy(dst, src, sem)

    def per_expert(e, carry):
        base, off = tab_ref[0, TAB_BASE + e], tab_ref[0, TAB_OFF + e]

        def one(c, carry):
            copy(pl.multiple_of(off + c * chunk, chunk), pl.multiple_of(base + c * chunk, chunk)).start()
            return carry

        return lax.fori_loop(0, tab_ref[0, TAB_N + e], one, carry)

    lax.fori_loop(0, N_EXPERTS, per_expert, 0)

    def drain(c, carry):
        copy(0, 0).wait()
        return carry

    lax.fori_loop(0, tab_ref[0, TAB_TOTAL], drain, 0)


def _dispatch_kernel(tab_ref, fill_ref, x_ref, meta_ref, xs_ref, sorted_ref, zero_ref, sem, *, chunk):
    tm, npos = x_ref.shape[0], sorted_ref.shape[0]
    meta = meta_ref[...]
    lane_pos = lax.broadcasted_iota(I32, (tm, npos), 1).astype(F32)
    place = _one_hot(meta, lane_pos, 0) + _one_hot(meta, lane_pos, 1)
    sorted_ref[...] = lax.dot_general(place, x_ref[...], TN, preferred_element_type=F32)
    _chunk_copies(tab_ref, sorted_ref, xs_ref, sem, chunk, to_hbm=True)

    @pl.when(pl.program_id(0) == pl.num_programs(0) - 1)
    def _():
        zero_ref[...] = jnp.zeros_like(zero_ref)
        big = zero_ref.shape[0]

        def fill(row, size):
            return pltpu.make_async_copy(zero_ref.at[pl.ds(0, size), :],
                                         xs_ref.at[pl.ds(pl.multiple_of(row, size), size), :], sem)

        def per_expert(e, total):
            def one(c, carry):
                fill(fill_ref[0, FILL_BASE + e] + c * chunk, chunk).start()
                return carry
            lax.fori_loop(0, fill_ref[0, FILL_N + e], one, 0)
            return total + fill_ref[0, FILL_N + e]

        def drain(c, carry):
            fill(0, chunk).wait()
            return carry

        lax.fori_loop(0, lax.fori_loop(0, N_EXPERTS, per_expert, 0), drain, 0)

        def tail(c, carry):
            fill(fill_ref[0, FILL_TAIL] + c * big, big).start()
            return carry

        def drain_tail(c, carry):
            fill(0, big).wait()
            return carry

        lax.fori_loop(0, fill_ref[0, FILL_TAIL + 1], tail, 0)
        lax.fori_loop(0, fill_ref[0, FILL_TAIL + 1], drain_tail, 0)


def _dispatch(x, meta, tab, fill, rows, tm, chunk):
    n, d = x.shape
    npos = TOP_K * tm + N_EXPERTS * chunk
    return pl.pallas_call(
        functools.partial(_dispatch_kernel, chunk=chunk),
        grid=(n // tm,),
        in_specs=[pl.BlockSpec((None, 1, LANE), lambda i: (i, 0, 0), memory_space=pltpu.SMEM),
                  pl.BlockSpec((1, LANE), lambda i: (0, 0), memory_space=pltpu.SMEM),
                  pl.BlockSpec((tm, d), lambda i: (i, 0)),
                  pl.BlockSpec((tm, LANE), lambda i: (i, 0))],
        out_specs=pl.BlockSpec(memory_space=pl.ANY),
        out_shape=jax.ShapeDtypeStruct((rows, d), F32),
        scratch_shapes=[pltpu.VMEM((npos, d), F32), pltpu.VMEM((FILL_ROWS, d), F32), pltpu.SemaphoreType.DMA(())],
        compiler_params=_params("arbitrary"),
        name="moe_dispatch",
    )(tab, fill, x, meta)


def _expert_kernel(be_ref, nu_ref, x_ref, w1_ref, w3_ref, w2_ref, y_ref):
    del be_ref
    live = pl.program_id(0) < nu_ref[0]

    @pl.when(live)
    def _():
        x = x_ref[...].astype(BF16)
        a = jnp.dot(x, w1_ref[...], preferred_element_type=F32)
        b = jnp.dot(x, w3_ref[...], preferred_element_type=F32)
        y_ref[...] = jnp.dot((_silu(a) * b).astype(BF16), w2_ref[...], preferred_element_type=F32)

    @pl.when(jnp.logical_not(live))
    def _():
        y_ref[...] = jnp.zeros_like(y_ref)


def _experts(xs, block_e, n_used, w1, w3, w2, bm):
    rows, d = xs.shape
    de = w1.shape[2]
    return pl.pallas_call(
        _expert_kernel,
        grid_spec=pltpu.PrefetchScalarGridSpec(
            num_scalar_prefetch=2,
            grid=(rows // bm,),
            in_specs=[pl.BlockSpec((bm, d), lambda i, be, nu: (i, 0)),
                      pl.BlockSpec((None, d, de), lambda i, be, nu: (be[i], 0, 0)),
                      pl.BlockSpec((None, d, de), lambda i, be, nu: (be[i], 0, 0)),
                      pl.BlockSpec((None, de, d), lambda i, be, nu: (be[i], 0, 0))],
            out_specs=pl.BlockSpec((bm, d), lambda i, be, nu: (i, 0))),
        out_shape=jax.ShapeDtypeStruct((rows, d), F32),
        compiler_params=_params("arbitrary"),
        name="moe_experts",
    )(block_e, n_used, xs, w1, w3, w2)


def _combine_kernel(tab_ref, h_ref, meta_ref, g_ref, yb_ref, y_ref, sorted_ref, sem, *, chunk):
    @pl.when(pl.program_id(0) == 0)
    def _():
        sorted_ref[...] = jnp.zeros_like(sorted_ref)

    _chunk_copies(tab_ref, sorted_ref, yb_ref, sem, chunk, to_hbm=False)
    tm, npos = h_ref.shape[0], sorted_ref.shape[0]
    meta = meta_ref[...]
    lane_pos = lax.broadcasted_iota(I32, (tm, npos), 1).astype(F32)
    yb = sorted_ref[...].astype(BF16)
    moe = (meta[:, META_W:META_W + 1] * jnp.dot(_one_hot(meta, lane_pos, 0), yb, preferred_element_type=F32)
           + meta[:, META_W + 1:META_W + 2] * jnp.dot(_one_hot(meta, lane_pos, 1), yb, preferred_element_type=F32))
    y_ref[...] = _rms(h_ref[...] + moe, g_ref[...])


def _combine(h, meta, tab, yb, g, tm, chunk):
    n, d = h.shape
    npos = TOP_K * tm + N_EXPERTS * chunk
    return pl.pallas_call(
        functools.partial(_combine_kernel, chunk=chunk),
        grid=(n // tm,),
        in_specs=[pl.BlockSpec((None, 1, LANE), lambda i: (i, 0, 0), memory_space=pltpu.SMEM),
                  pl.BlockSpec((tm, d), lambda i: (i, 0)),
                  pl.BlockSpec((tm, LANE), lambda i: (i, 0)),
                  pl.BlockSpec((1, d), lambda i: (0, 0)),
                  pl.BlockSpec(memory_space=pl.ANY)],
        out_specs=pl.BlockSpec((tm, d), lambda i: (i, 0)),
        out_shape=jax.ShapeDtypeStruct((n, d), F32),
        scratch_shapes=[pltpu.VMEM((npos, d), F32), pltpu.SemaphoreType.DMA(())],
        compiler_params=_params("arbitrary"),
        name="moe_combine",
    )(tab, h, meta, g, yb)


def _moe_final(h2, n3, logits, w1, w3, w2, g_final, tm, bm, chunk):
    n = h2.shape[0]
    nt = n // tm
    assert 3 * N_EXPERTS < LANE and bm % chunk == 0
    meta, tiles, cnt = _router(logits, tm, chunk)
    rows_e = cnt[0, :N_EXPERTS].astype(I32)
    padded = (rows_e + bm - 1) // bm * bm
    pad_end = jnp.cumsum(padded)
    pad_start = pad_end - padded
    tiles = tiles.reshape(nt, SUBLANE, LANE)
    run_before = tiles[:, 0, :N_EXPERTS].astype(I32)
    nchunks = tiles[:, 1, :N_EXPERTS].astype(I32) // chunk
    offs = chunk * (jnp.cumsum(nchunks, axis=1) - nchunks)
    pad_lanes = lambda a: jnp.pad(a, ((0, 0), (0, LANE - a.shape[1])))
    tab = pad_lanes(jnp.concatenate([pad_start[None, :] + run_before, offs, nchunks,
                                     jnp.sum(nchunks, axis=1, keepdims=True)], axis=1)).reshape(nt, 1, LANE)
    max_rows = n * TOP_K + (chunk - 1) * min(n * TOP_K, nt * N_EXPERTS) + N_EXPERTS * (bm - 1)
    nb = -(-max_rows // bm)
    assert bm % FILL_ROWS == 0
    fill = pad_lanes(jnp.concatenate([pad_start + rows_e, (padded - rows_e) // chunk,
                                      pad_end[-1:], (nb * bm - pad_end[-1:]) // FILL_ROWS])[None, :])
    block_start = jnp.arange(nb, dtype=I32) * bm
    block_e = jnp.minimum(jnp.sum((pad_end[None, :] <= block_start[:, None]).astype(I32), axis=1), N_EXPERTS - 1)
    n_used = (pad_end[-1:] // bm).astype(I32)
    xs = _dispatch(n3, meta, tab, fill, nb * bm, tm, chunk)
    yb = _experts(xs, block_e, n_used, w1, w3, w2, bm)
    return _combine(h2, meta, tab, yb, g_final, tm, chunk)


def _group(x3, mk, mv, w, keep, *, cache=None, state=None):
    batch, seq, d = x3.shape
    n = batch * seq
    x = x3.reshape(n, d)
    tm = _tile(n, TOKEN_TILE)
    sample = cache is not None
    pa, pb, k_new, v_new = _norm_proj(x, w["g1"], w["w_in"], batch, seq, keep, not sample, tm)
    if sample:
        assert keep == seq
        oa, st = _gla_sample(pa, _state_to_blockdiag_t(state), w["wgk"], w["bgk"], w["gg"], batch, seq,
                             rows=GLA_CHUNK)
        ob = _dil_sample(pb, k_new, v_new, cache[0].reshape(batch, -1, DIL_W), cache[1].reshape(batch, -1, DIL_W),
                         batch, seq)
    else:
        zeros = jnp.zeros((batch, GLA_V, GLA_QK), F32)
        oa, st = _gla_prompt(pa, zeros, w["wgk"], w["bgk"], w["gg"], batch, seq, tb=_tile(seq, TOKEN_TILE))
        ob = _dil_prompt(pb, batch, seq)
    h1, qc = _out_proj(oa, ob, x, w["w_out"], w["g2"], w["w_cq"], F32 if sample else BF16, tm)
    oc = _cross(qc, mk, mv, batch, seq, _tile(seq, TOKEN_TILE), F32 if sample else BF16)
    h2, n3, logits = _co_proj(oc, h1, w["w_co"], w["g3"], w["wr_hi"], w["wr_lo"], w["br"], tm)
    y = _moe_final(h2, n3, logits, w["w_e1"], w["w_e3"], w["w_e2"], w["g_final"], _tile(n, MOE_TILE), MOE_BLOCK,
                   MOE_CHUNK)
    return (y.reshape(batch, seq, d), _blockdiag_t_to_state(st),
            k_new.reshape(batch, keep, DIL_HEADS, DIL_HD), v_new.reshape(batch, keep, DIL_HEADS, DIL_HD))


def _pack_weights(l, g_norm1, w_in, w_gk2, b_gk, g_gla_out, w_out, g_norm2, w_cq, w_co, g_norm3, w_gr, b_gr,
                  w_er, b_er, w_e1, w_e3, w_e2, g_final):
    d = w_in.shape[1]
    sp = np.cumsum([GLA_QK, GLA_QK, GLA_V, GATE_RANK, GLA_V, DIL_W, DIL_W, DIL_W])
    wi = w_in[l]
    q_a, k_a, v_a, glr, r_a, q_b, k_b, v_b = (wi[:, a:b] for a, b in zip([0, *sp[:-1]], sp))
    glr = jnp.pad(glr, ((0, 0), (0, LANE - GATE_RANK)))
    wr = jnp.pad(jnp.concatenate([w_gr[l], w_er[l]], axis=1), ((0, 0), (0, LANE - N_GROUPS - N_EXPERTS)))
    wr_hi = wr.astype(BF16)
    return dict(
        g1=g_norm1[l].reshape(1, d), g2=g_norm2[l].reshape(1, d), g3=g_norm3[l].reshape(1, d),
        g_final=g_final.reshape(1, d),
        w_in=jnp.concatenate([q_a, k_a, v_a, r_a, glr, q_b, k_b, v_b], axis=1).astype(BF16),
        wgk=jnp.pad(w_gk2[l], ((0, LANE - GATE_RANK), (0, 0))).astype(BF16),
        bgk=b_gk[l].reshape(1, GLA_QK), gg=g_gla_out[l].reshape(1, GLA_DV),
        w_out=w_out[l].astype(BF16), w_cq=w_cq[l].astype(BF16), w_co=w_co[l].astype(BF16),
        wr_hi=wr_hi, wr_lo=(wr - wr_hi.astype(F32)).astype(BF16),
        br=jnp.pad(jnp.concatenate([b_gr[l], b_er[l]]), (0, LANE - N_GROUPS - N_EXPERTS)).reshape(1, LANE),
        w_e1=w_e1[l].astype(BF16), w_e3=w_e3[l].astype(BF16), w_e2=w_e2[l].astype(BF16))


def kernel(x_prompt, x_sample, cache_swa_k, cache_swa_v, state_gla, cache_mem_k, cache_mem_v, mem_prompt,
           g_norm1, w_in, w_gk2, b_gk, g_gla_out, w_out, g_norm2, g_mem, w_cq, w_mk, w_mv, w_co,
           g_norm3, w_gr, b_gr, w_er, b_er, w_e1, w_e3, w_e2, g_final):
    depth = w_in.shape[0]
    assert depth == 1, "the final norm is fused into the last MoE stage; stacked layers are not supported"
    batch, seq, d = x_prompt.shape
    sb, sseq, _ = x_sample.shape
    nm = mem_prompt.shape[1]
    keep = min(DIL_PATTERNS[-1][0], seq)
    l = 0
    w = _pack_weights(l, g_norm1, w_in, w_gk2, b_gk, g_gla_out, w_out, g_norm2, w_cq, w_co, g_norm3,
                      w_gr, b_gr, w_er, b_er, w_e1, w_e3, w_e2, g_final)
    mkv = _norm_matmul(mem_prompt.reshape(batch * nm, d), g_mem[l].reshape(1, d),
                       jnp.concatenate([w_mk[l], w_mv[l]], axis=1).astype(BF16), _tile(batch * nm, TOKEN_TILE))
    mk = mkv[:, :d].reshape(batch, nm, d)
    mv = mkv[:, d:].reshape(batch, nm, d)
    yp, sp, kp, vp = _group(x_prompt, mk, mv, w, keep)
    past = cache_swa_k.shape[2]
    ys, ss, kn, vn = _group(x_sample, cache_mem_k.reshape(sb, nm, d), cache_mem_v.reshape(sb, nm, d), w, sseq,
                            cache=(cache_swa_k.reshape(sb, past, DIL_W), cache_swa_v.reshape(sb, past, DIL_W)),
                            state=state_gla[l])
    heads = lambda m: m.reshape(batch, nm, MEM_HEADS, d // MEM_HEADS)
    return (yp, ys, *(o[None] for o in (kp, vp, sp, heads(mk), heads(mv), kn, vn, ss)))
```

```python
import functools
import math

import numpy as np
import jax
import jax.numpy as jnp
from jax import lax
from jax.experimental import pallas as pl
from jax.experimental.pallas import tpu as pltpu

F32, BF16, I32 = jnp.float32, jnp.bfloat16, jnp.int32

GLA_HEADS, GLA_DK, GLA_DV = 4, 64, 128
GLA_QK, GLA_V = GLA_HEADS * GLA_DK, GLA_HEADS * GLA_DV
GATE_RANK, GATE_NORM, GLA_CHUNK = 16, 16.0, 64
DIL_HEADS, DIL_HD = 8, 64
DIL_W = DIL_HEADS * DIL_HD
DIL_PATTERNS = ((128, 1), (512, 4), (2048, 16))
DIL_KEYS = 128
MEM_HEADS = 4
N_GROUPS, EXP_PER_GROUP, TOP_K = 4, 8, 2
N_EXPERTS = N_GROUPS * EXP_PER_GROUP
EPS = 1e-6

LANE = 128
SUBLANE = 8
VMEM_LIMIT_BYTES = 56 * 1024 * 1024

PA_W = 2 * GLA_QK + 2 * GLA_V + LANE
PB_W = 3 * DIL_W
HALF_W = DIL_W // 2
HI_MASK = -65536

TOKEN_TILE = 512
MOE_TILE = 256
MOE_BLOCK = 512
MOE_CHUNK = SUBLANE

NT = (((1,), (1,)), ((), ()))
TN = (((0,), (0,)), ((), ()))


def _params(*sem):
    return pltpu.CompilerParams(dimension_semantics=sem, vmem_limit_bytes=VMEM_LIMIT_BYTES)


def _tile(n, want):
    t = min(n, want)
    assert n % t == 0
    return t


def _rms(x, g):
    y = x * lax.rsqrt(jnp.mean(x * x, axis=-1, keepdims=True) + EPS)
    return y * g


def _silu(x):
    return x / (1.0 + jnp.exp(-x))


def _iota_div(shape, dim, n):
    assert n & (n - 1) == 0
    return lax.broadcasted_iota(I32, shape, dim) >> int(math.log2(n))


def _pack_halves(r):
    u = lax.bitcast_convert_type(r.astype(BF16).astype(F32), I32)
    lo = u[:, :HALF_W]
    return lax.shift_right_logical(lo, jnp.full_like(lo, 16)) | (u[:, HALF_W:] & HI_MASK)


def _unpack_halves(w):
    lo = lax.bitcast_convert_type(w << 16, F32).astype(BF16)
    hi = lax.bitcast_convert_type(w & HI_MASK, F32).astype(BF16)
    return lo, hi


def _norm_proj_kernel(x_ref, g_ref, w_ref, pa_ref, pb_ref, k_ref, v_ref, *, pack):
    n = _rms(x_ref[...], g_ref[...]).astype(BF16)
    for c0 in range(0, PA_W, DIL_W):
        c1 = min(c0 + DIL_W, PA_W)
        pa_ref[:, c0:c1] = jnp.dot(n, w_ref[:, c0:c1], preferred_element_type=F32).astype(pa_ref.dtype)
    for j, kv_ref in enumerate((None, k_ref, v_ref)):
        r = jnp.dot(n, w_ref[:, PA_W + j * DIL_W:PA_W + (j + 1) * DIL_W], preferred_element_type=F32)
        if pack:
            words = _pack_halves(r)
            for part in range(HALF_W // LANE):
                pb_ref[j * (HALF_W // LANE) + part] = words[:, part * LANE:(part + 1) * LANE]
        else:
            pb_ref[:, j * DIL_W:(j + 1) * DIL_W] = r
        if kv_ref is not None:
            kv_ref[...] = r


def _norm_proj(x, g, w, batch, seq, keep, pack, tm):
    n, d = x.shape
    if keep == seq:
        kv_map = lambda i: (i, 0)
    else:
        nt, nk = seq // tm, keep // tm
        assert keep % tm == 0
        kv_map = lambda i: ((i // nt) * nk + jnp.maximum(i % nt - (nt - nk), 0), 0)
    if pack:
        nparts = 3 * HALF_W // LANE
        pb_spec = pl.BlockSpec((nparts, tm, LANE), lambda i: (0, i, 0))
        pb_shape = jax.ShapeDtypeStruct((nparts, n, LANE), I32)
    else:
        pb_spec = pl.BlockSpec((tm, PB_W), lambda i: (i, 0))
        pb_shape = jax.ShapeDtypeStruct((n, PB_W), F32)
    pa_dtype = BF16 if pack else F32
    return pl.pallas_call(
        functools.partial(_norm_proj_kernel, pack=pack),
        grid=(n // tm,),
        in_specs=[pl.BlockSpec((tm, d), lambda i: (i, 0)),
                  pl.BlockSpec((1, d), lambda i: (0, 0)),
                  pl.BlockSpec((d, PA_W + PB_W), lambda i: (0, 0))],
        out_specs=[pl.BlockSpec((tm, PA_W), lambda i: (i, 0)),
                   pb_spec,
                   pl.BlockSpec((tm, DIL_W), kv_map),
                   pl.BlockSpec((tm, DIL_W), kv_map)],
        out_shape=[jax.ShapeDtypeStruct((n, PA_W), pa_dtype),
                   pb_shape,
                   jax.ShapeDtypeStruct((batch * keep, DIL_W), F32),
                   jax.ShapeDtypeStruct((batch * keep, DIL_W), F32)],
        compiler_params=_params("arbitrary"),
        name="norm_proj",
    )(x, g, w)


def _norm_matmul_kernel(x_ref, g_ref, w_ref, o_ref):
    n = _rms(x_ref[...], g_ref[...]).astype(BF16)
    for c0 in range(0, o_ref.shape[1], DIL_W):
        o_ref[:, c0:c0 + DIL_W] = jnp.dot(n, w_ref[:, c0:c0 + DIL_W], preferred_element_type=F32)


def _norm_matmul(x, g, w, tm):
    n, d = x.shape
    m = w.shape[1]
    return pl.pallas_call(
        _norm_matmul_kernel,
        grid=(n // tm,),
        in_specs=[pl.BlockSpec((tm, d), lambda i: (i, 0)),
                  pl.BlockSpec((1, d), lambda i: (0, 0)),
                  pl.BlockSpec((d, m), lambda i: (0, 0))],
        out_specs=pl.BlockSpec((tm, m), lambda i: (i, 0)),
        out_shape=jax.ShapeDtypeStruct((n, m), F32),
        compiler_params=_params("parallel"),
        name="norm_matmul",
    )(x, g, w)


def _gla_tables(chunk, seg):
    idx = np.arange(chunk)
    tril = ((idx[None, :] <= idx[:, None]) & (idx[None, :] // seg == idx[:, None] // seg)).astype(np.float32)
    masks, levels = [], []
    s = seg // 2
    while s >= 1:
        same = (idx[:, None] // (2 * s)) == (idx[None, :] // (2 * s))
        masks.append(same & ((idx[:, None] // s) % 2 == 1) & ((idx[None, :] // s) % 2 == 0))
        levels.append(s)
        s //= 2
    masks.append(idx[:, None] == idx[None, :])
    pm = np.tile(np.stack(masks).astype(np.float32), (1, 1, GLA_HEADS))
    return jnp.asarray(tril, BF16), jnp.asarray(pm), tuple(levels)


def _block_row(b, blk, idx):
    c, w = b.shape
    b3 = b.reshape(c // blk, blk, w)
    return jnp.broadcast_to(b3[:, idx:idx + 1, :], (c // blk, blk, w)).reshape(c, w)


def _level_ref(b, s, row):
    c = b.shape[0]
    if 2 * s >= SUBLANE:
        return _block_row(b, 2 * s, s - 1)
    down = lambda n: pltpu.roll(b, n, 0)
    if s == 2:
        m = row & 3
        return jnp.where(m == 0, pltpu.roll(b, c - 1, 0), jnp.where(m == 1, b, jnp.where(m == 2, down(1), down(2))))
    assert s == 1
    return jnp.where((row & 1) == 1, down(1), b)


def _gla_chunk(q, k, v, glr, wgk, bgk, tril_ref, pm_ref, levels, seg):
    c = q.shape[0]
    nl = len(levels)
    gk = jnp.dot(glr, wgk, preferred_element_type=F32) + bgk
    la = (jnp.minimum(gk, 0.0) - jnp.log1p(jnp.exp(-jnp.abs(gk)))) * (1.0 / GATE_NORM)
    hi = la.astype(BF16)
    r1 = la - hi.astype(F32)
    mid = r1.astype(BF16)
    lo = (r1 - mid.astype(F32)).astype(BF16)
    b3 = jnp.dot(tril_ref[...], jnp.concatenate([hi, mid, lo], axis=1), preferred_element_type=F32)
    b = b3[:, :GLA_QK] + b3[:, GLA_QK:2 * GLA_QK] + b3[:, 2 * GLA_QK:]
    b_end = _block_row(b, seg, seg - 1)

    row = lax.broadcasted_iota(I32, (c, GLA_QK), 0)
    khead = _iota_div((c, GLA_QK), 1, GLA_DK)
    vhead = _iota_div((c, GLA_V), 1, GLA_DV)

    def by_head(x, head):
        return jnp.concatenate([jnp.where(head == h, x, jnp.zeros_like(x)) for h in range(GLA_HEADS)], axis=0)

    def level(qe, ke, pm):
        a = lax.dot_general(qe, by_head(ke, khead), NT, preferred_element_type=F32)
        return jnp.where(pm > 0.0, a, 0.0)

    acc = level(q.astype(BF16), k.astype(BF16), pm_ref[nl])
    for l, s in enumerate(levels):
        ref = _level_ref(b, s, row)
        right = ((row >> int(math.log2(s))) & 1) == 1
        qe = (q * jnp.exp(jnp.where(right, b - ref, 0.0))).astype(BF16)
        ke = (k * jnp.exp(jnp.where(right, 0.0, ref - b))).astype(BF16)
        acc = acc + level(qe, ke, pm_ref[l])

    o = jnp.dot(acc.astype(BF16), by_head(v, vhead), preferred_element_type=F32)
    qb = (q * jnp.exp(b)).astype(BF16)
    kd = (k * jnp.exp(b_end - b)).astype(BF16)
    return o, qb, kd, b_end


def _gla_finish(o, r, gg):
    outs = []
    for h in range(GLA_HEADS):
        sl = slice(h * GLA_DV, (h + 1) * GLA_DV)
        outs.append(_rms(o[:, sl], gg) * _silu(r[:, sl]))
    return jnp.concatenate(outs, axis=1)


def _state_mask():
    return _iota_div((GLA_V, GLA_QK), 0, GLA_DV) == _iota_div((GLA_V, GLA_QK), 1, GLA_DK)


def _gla_prompt_kernel(q_ref, k_ref, v_ref, r_ref, glr_ref, wgk_ref, bgk_ref, gg_ref, tril_ref, pm_ref, s0_ref,
                       o_ref, sout_ref, st_ref, *, chunk, levels):
    i = pl.program_id(1)

    @pl.when(i == 0)
    def _():
        st_ref[...] = s0_ref[...]

    smask = _state_mask()

    def body(c, carry):
        rows = pl.ds(pl.multiple_of(c * chunk, chunk), chunk)
        q = q_ref[rows, :].astype(F32) * (GLA_DK ** -0.5)
        k = k_ref[rows, :].astype(F32)
        v = v_ref[rows, :]
        o, qb, kd, b_end = _gla_chunk(q, k, v, glr_ref[rows, :], wgk_ref[...], bgk_ref[...],
                                      tril_ref, pm_ref, levels, chunk)
        st = st_ref[...]
        o = o + lax.dot_general(qb, st.astype(BF16), NT, preferred_element_type=F32)
        u = lax.dot_general(v, kd, TN, preferred_element_type=F32)
        st_ref[...] = st * jnp.exp(b_end[0:1, :]) + jnp.where(smask, u, 0.0)
        o_ref[rows, :] = _gla_finish(o, r_ref[rows, :].astype(F32), gg_ref[...]).astype(o_ref.dtype)
        return carry

    lax.fori_loop(0, q_ref.shape[0] // chunk, body, 0, unroll=2)

    @pl.when(i == pl.num_programs(1) - 1)
    def _():
        sout_ref[...] = st_ref[...]


def _gla_prompt(pa, s0t, wgk, bgk, gg, batch, seq, tb):
    chunk = math.gcd(seq, GLA_CHUNK)
    tril, pm, levels = _gla_tables(chunk, chunk)
    nt = seq // tb
    row = lambda b, i: (b * nt + i, 0)
    const2 = lambda b, i: (0, 0)
    return pl.pallas_call(
        functools.partial(_gla_prompt_kernel, chunk=chunk, levels=levels),
        grid=(batch, nt),
        in_specs=[pl.BlockSpec((tb, GLA_QK), row),
                  pl.BlockSpec((tb, GLA_QK), lambda b, i: (b * nt + i, 1)),
                  pl.BlockSpec((tb, GLA_V), lambda b, i: (b * nt + i, 1)),
                  pl.BlockSpec((tb, GLA_V), lambda b, i: (b * nt + i, 2)),
                  pl.BlockSpec((tb, LANE), lambda b, i: (b * nt + i, (PA_W - LANE) // LANE)),
                  pl.BlockSpec(wgk.shape, const2),
                  pl.BlockSpec(bgk.shape, const2),
                  pl.BlockSpec(gg.shape, const2),
                  pl.BlockSpec(tril.shape, const2),
                  pl.BlockSpec(pm.shape, lambda b, i: (0, 0, 0)),
                  pl.BlockSpec((None, GLA_V, GLA_QK), lambda b, i: (b, 0, 0))],
        out_specs=[pl.BlockSpec((tb, GLA_V), row),
                   pl.BlockSpec((None, GLA_V, GLA_QK), lambda b, i: (b, 0, 0))],
        out_shape=[jax.ShapeDtypeStruct((batch * seq, GLA_V), BF16),
                   jax.ShapeDtypeStruct((batch, GLA_V, GLA_QK), F32)],
        scratch_shapes=[pltpu.VMEM((GLA_V, GLA_QK), F32)],
        compiler_params=_params("parallel", "arbitrary"),
        name="gla_prompt",
    )(pa, pa, pa, pa, pa, wgk, bgk, gg, tril, pm, s0t)


def _gla_sample_kernel(q_ref, k_ref, v_ref, r_ref, glr_ref, wgk_ref, bgk_ref, gg_ref, tril_ref, pm_ref, s0_ref,
                       o_ref, sout_ref, *, seg, levels):
    rows = q_ref.shape[0]
    q = q_ref[...] * (GLA_DK ** -0.5)
    v = v_ref[...].astype(BF16)
    o, qb, kd, b_end = _gla_chunk(q, k_ref[...], v, glr_ref[...].astype(BF16), wgk_ref[...], bgk_ref[...],
                                  tril_ref, pm_ref, levels, seg)
    smask = _state_mask()
    seq_o = _iota_div((rows, GLA_V), 0, seg)
    for j in range(rows // seg):
        st = s0_ref[j]
        oj = lax.dot_general(qb, st.astype(BF16), NT, preferred_element_type=F32)
        o = o + jnp.where(seq_o == j, oj, 0.0)
        u = lax.dot_general(jnp.where(seq_o == j, v, jnp.zeros_like(v)), kd, TN, preferred_element_type=F32)
        sout_ref[j] = st * jnp.exp(b_end[j * seg:j * seg + 1, :]) + jnp.where(smask, u, 0.0)
    o_ref[...] = _gla_finish(o, r_ref[...], gg_ref[...])


def _gla_sample(pa, s0t, wgk, bgk, gg, batch, seq, rows):
    seg = math.gcd(seq, GLA_CHUNK)
    assert seg == seq and seg % SUBLANE == 0, "sample sequences must be one sublane-aligned chunk"
    tril, pm, levels = _gla_tables(rows, seg)
    nseq = rows // seg
    row = lambda i: (i, 0)
    const2 = lambda i: (0, 0)
    return pl.pallas_call(
        functools.partial(_gla_sample_kernel, seg=seg, levels=levels),
        grid=(batch * seq // rows,),
        in_specs=[pl.BlockSpec((rows, GLA_QK), row),
                  pl.BlockSpec((rows, GLA_QK), lambda i: (i, 1)),
                  pl.BlockSpec((rows, GLA_V), lambda i: (i, 1)),
                  pl.BlockSpec((rows, GLA_V), lambda i: (i, 2)),
                  pl.BlockSpec((rows, LANE), lambda i: (i, (PA_W - LANE) // LANE)),
                  pl.BlockSpec(wgk.shape, const2),
                  pl.BlockSpec(bgk.shape, const2),
                  pl.BlockSpec(gg.shape, const2),
                  pl.BlockSpec(tril.shape, const2),
                  pl.BlockSpec(pm.shape, lambda i: (0, 0, 0)),
                  pl.BlockSpec((nseq, GLA_V, GLA_QK), lambda i: (i, 0, 0))],
        out_specs=[pl.BlockSpec((rows, GLA_V), row),
                   pl.BlockSpec((nseq, GLA_V, GLA_QK), lambda i: (i, 0, 0))],
        out_shape=[jax.ShapeDtypeStruct((batch * seq, GLA_V), F32),
                   jax.ShapeDtypeStruct((batch, GLA_V, GLA_QK), F32)],
        compiler_params=_params("parallel"),
        name="gla_sample",
    )(pa, pa, pa, pa, pa, wgk, bgk, gg, tril, pm, s0t)


def _state_to_blockdiag_t(s):
    b = s.shape[0]
    st = jnp.swapaxes(s, 2, 3)
    eye = jnp.eye(GLA_HEADS, dtype=s.dtype)
    return (st[:, :, :, None, :] * eye[None, :, None, :, None]).reshape(b, GLA_V, GLA_QK)


def _blockdiag_t_to_state(st):
    b = st.shape[0]
    s5 = st.reshape(b, GLA_HEADS, GLA_DV, GLA_HEADS, GLA_DK)
    return jnp.stack([jnp.swapaxes(s5[:, h, :, h, :], 1, 2) for h in range(GLA_HEADS)], axis=1)


def _alibi_slopes():
    return np.asarray([2.0 ** (-8.0 * (h + 1) / DIL_HEADS) for h in range(DIL_HEADS)], np.float64)


DIL_GROUP = 4
DIL_GW = DIL_GROUP * DIL_HD


def _dil_bias():
    i = np.arange(DIL_KEYS)[:, None]
    c = np.arange(2 * DIL_KEYS)[None, :]
    dist = DIL_KEYS + i - c
    ok = (dist >= 0) & (dist <= DIL_KEYS)
    out = np.empty((len(DIL_PATTERNS), DIL_HEADS, DIL_KEYS, 2 * DIL_KEYS), np.float32)
    for p, (_, d) in enumerate(DIL_PATTERNS):
        for h, sl in enumerate(_alibi_slopes()):
            out[p, h] = np.where(ok, -sl * (dist * d), -np.inf)
    out = out.reshape(len(DIL_PATTERNS), DIL_HEADS // DIL_GROUP, DIL_GROUP * DIL_KEYS, 2 * DIL_KEYS)
    return jnp.asarray(np.ascontiguousarray(out.transpose(0, 1, 3, 2)))


def _dil_prompt_kernel(qkv_hbm, biast_ref, o_ref, qkv_ref, acc_ref, m_ref, l_ref, sem):
    seq = qkv_ref.shape[1]
    blk = DIL_KEYS
    ngroups = DIL_HEADS // DIL_GROUP
    wparts = HALF_W // LANE
    aparts = DIL_GW // LANE
    head = _iota_div((blk, DIL_GW), 1, DIL_HD)

    load = pltpu.make_async_copy(qkv_hbm.at[:, pl.ds(pl.multiple_of(pl.program_id(0) * seq, seq), seq), :],
                                 qkv_ref, sem)
    load.start()
    load.wait()

    def rows(start, n, d):
        return pl.ds(pl.multiple_of(start, blk), n) if d == 1 else pl.ds(start, n, stride=d)

    def unpack(which, sel):
        return _unpack_halves(jnp.concatenate([qkv_ref[which * wparts + j, sel, :] for j in range(wparts)], axis=1))

    def by_head(x, keep):
        return jnp.concatenate([jnp.where(keep(hh), x, jnp.zeros_like(x)) for hh in range(DIL_GROUP)], axis=0)

    def stacked_q(q2, g):
        return by_head(q2[g] * (DIL_HD ** -0.5), lambda hh: head == hh)

    def find_max(p, d, start, has_prev):
        qsel = rows(start, blk, d)
        ksel = rows(start - d * blk, 2 * blk, d) if has_prev else qsel
        q2, k2 = unpack(0, qsel), unpack(1, ksel)
        per_head = []
        for g in range(ngroups):
            bias = biast_ref[p, g] if has_prev else biast_ref[p, g, blk:2 * blk, :]
            st = lax.dot_general(k2[g], stacked_q(q2, g), NT, preferred_element_type=F32) + bias
            mg = jnp.max(st, axis=0, keepdims=True)
            per_head += [mg[:, hh * blk:(hh + 1) * blk] for hh in range(DIL_GROUP)]
        rest = jnp.full((blk - DIL_HEADS, blk), -jnp.inf, F32)
        mt = jnp.concatenate(per_head + [rest], axis=0).T
        m_ref[qsel, :] = mt if p == 0 else jnp.maximum(m_ref[qsel, :], mt)

    def accumulate(p, d, start, has_prev):
        qsel = rows(start, blk, d)
        ksel = rows(start - d * blk, 2 * blk, d) if has_prev else qsel
        q2, k2, v2 = unpack(0, qsel), unpack(1, ksel), unpack(2, ksel)
        m_rows = m_ref[qsel, :].T
        per_head = []
        for g in range(ngroups):
            bias = biast_ref[p, g] if has_prev else biast_ref[p, g, blk:2 * blk, :]
            st = lax.dot_general(k2[g], stacked_q(q2, g), NT, preferred_element_type=F32) + bias
            m_g = jnp.concatenate([m_rows[g * DIL_GROUP + hh:g * DIL_GROUP + hh + 1, :] for hh in range(DIL_GROUP)],
                                  axis=1)
            pt = jnp.exp(st - m_g)
            l_g = jnp.sum(pt, axis=0, keepdims=True)
            per_head += [l_g[:, hh * blk:(hh + 1) * blk] for hh in range(DIL_GROUP)]
            out = lax.dot_general(pt.astype(BF16), v2[g], TN, preferred_element_type=F32)
            new = jnp.zeros((blk, DIL_GW), F32)
            for hh in range(DIL_GROUP):
                new = jnp.where(head == hh, out[hh * blk:(hh + 1) * blk], new)
            for j in range(aparts):
                part = new[:, j * LANE:(j + 1) * LANE]
                acc_ref[g * aparts + j, qsel, :] = part if p == 0 else acc_ref[g * aparts + j, qsel, :] + part
        rest = jnp.zeros((blk - DIL_HEADS, blk), F32)
        l_new = jnp.concatenate(per_head + [rest], axis=0).T
        l_ref[qsel, :] = l_new if p == 0 else l_ref[qsel, :] + l_new

    def sweep(block):
        for p, (_, d) in enumerate(DIL_PATTERNS):
            nblk = seq // (d * blk)

            def first_block(r, carry, p=p, d=d):
                block(p, d, r, False)
                return carry

            def later_blocks(r, carry, p=p, d=d, nblk=nblk):
                def one(ib, carry):
                    block(p, d, ib * (d * blk) + r, True)
                    return carry
                return lax.fori_loop(1, nblk, one, carry)

            lax.fori_loop(0, d, first_block, 0)
            if nblk > 1:
                lax.fori_loop(0, d, later_blocks, 0)

    sweep(find_max)
    sweep(accumulate)

    half = _iota_div((blk, LANE), 1, DIL_HD)

    def normalise(i, carry):
        sel = pl.ds(pl.multiple_of(i * blk, blk), blk)
        inv = 1.0 / l_ref[sel, :]
        for j in range(DIL_W // LANE):
            scale = jnp.where(half == 0, inv[:, 2 * j:2 * j + 1], inv[:, 2 * j + 1:2 * j + 2])
            o_ref[sel, j * LANE:(j + 1) * LANE] = (acc_ref[j, sel, :] * scale).astype(o_ref.dtype)
        return carry

    lax.fori_loop(0, seq // blk, normalise, 0)


def _dil_prompt(qkv, batch, seq):
    for w, d in DIL_PATTERNS:
        assert w // d == DIL_KEYS and seq % (d * DIL_KEYS) == 0
    biast = _dil_bias()
    return pl.pallas_call(
        _dil_prompt_kernel,
        grid=(batch,),
        in_specs=[pl.BlockSpec(memory_space=pl.ANY),
                  pl.BlockSpec(biast.shape, lambda b: (0, 0, 0, 0))],
        out_specs=pl.BlockSpec((seq, DIL_W), lambda b: (b, 0)),
        out_shape=jax.ShapeDtypeStruct((batch * seq, DIL_W), BF16),
        scratch_shapes=[pltpu.VMEM((qkv.shape[0], seq, LANE), I32),
                        pltpu.VMEM((DIL_W // LANE, seq, LANE), F32),
                        pltpu.VMEM((seq, LANE), F32), pltpu.VMEM((seq, LANE), F32),
                        pltpu.SemaphoreType.DMA(())],
        compiler_params=_params("arbitrary"),
        name="dil_prompt",
    )(qkv, biast)


def _dil_sample_bias(positions, past, seq):
    positions = np.asarray(positions)[None, :]
    dist = past + np.arange(seq)[:, None] - positions
    mult = np.zeros(dist.shape, np.float64)
    for w, d in DIL_PATTERNS:
        mult += (positions >= 0) & (dist >= 0) & (dist <= w) & (dist % d == 0)
    with np.errstate(divide="ignore"):
        logm = np.log(mult)
    return jnp.asarray(np.concatenate([-sl * dist + logm for sl in _alibi_slopes()], axis=0).astype(np.float32))


def _dil_sample_kernel(q_ref, kn_ref, vn_ref, *refs, seq):
    nseg = (len(refs) - 2) // 3
    k_refs, v_refs, b_refs, o_ref = refs[:nseg], refs[nseg:2 * nseg], refs[2 * nseg:3 * nseg + 1], refs[-1]
    rows = DIL_HEADS * seq
    q = q_ref[...] * (DIL_HD ** -0.5)
    qrep = jnp.concatenate([q] * DIL_HEADS, axis=0)
    own = _iota_div((rows, DIL_W), 0, seq) == _iota_div((rows, DIL_W), 1, DIL_HD)
    qbd = jnp.where(own, qrep, 0.0).astype(BF16)
    pad = jnp.zeros((LANE - seq, DIL_W), F32)
    ks = [r[...].astype(BF16) for r in k_refs] + [jnp.concatenate([kn_ref[...], pad], axis=0).astype(BF16)]
    vs = [r[...].astype(BF16) for r in v_refs] + [jnp.concatenate([vn_ref[...], pad], axis=0).astype(BF16)]
    ss = [lax.dot_general(qbd, k, NT, preferred_element_type=F32) + b[...] for k, b in zip(ks, b_refs)]
    m = functools.reduce(jnp.maximum, [jnp.max(s, axis=-1, keepdims=True) for s in ss])
    ps = [jnp.exp(s - m) for s in ss]
    l = sum(jnp.sum(p, axis=-1, keepdims=True) for p in ps)
    o = sum(jnp.dot(p.astype(BF16), v, preferred_element_type=F32) for p, v in zip(ps, vs)) * (1.0 / l)
    o = jnp.where(own, o, 0.0)
    res = o[0:seq]
    for h in range(1, DIL_HEADS):
        res = res + o[h * seq:(h + 1) * seq]
    o_ref[...] = res


def _dil_sample(pb, k_new, v_new, cache_k, cache_v, batch, seq):
    past = cache_k.shape[1]
    w_far, d_far = max(DIL_PATTERNS, key=lambda wd: wd[1])
    cut = past - max(w for w, d in DIL_PATTERNS if d != d_far)
    assert cut >= 0 and past % d_far == 0 and cut % d_far == 0 and seq <= d_far and w_far <= past
    assert all(seq >= d for _, d in DIL_PATTERNS if d != d_far)
    nfar = cut // d_far
    far = lambda c: c[:, :cut].reshape(batch, nfar, d_far, DIL_W)[:, :, :seq].reshape(batch, nfar * seq, DIL_W)
    near = lambda c: c[:, cut:]
    far_pos = (np.arange(nfar)[:, None] * d_far + np.arange(seq)[None, :]).reshape(-1)
    new_pos = np.where(np.arange(LANE) < seq, past + np.arange(LANE), -1)
    segs = [(far(cache_k), far(cache_v), far_pos), (near(cache_k), near(cache_v), np.arange(cut, past))]
    segs = [s for s in segs if s[2].size]
    biases = [_dil_sample_bias(pos, past, seq) for _, _, pos in segs] + [_dil_sample_bias(new_pos, past, seq)]
    row = pl.BlockSpec((seq, DIL_W), lambda b: (b, 0))
    seg_spec = lambda a: pl.BlockSpec((None,) + a.shape[1:], lambda b: (b, 0, 0))
    const = lambda a: pl.BlockSpec(a.shape, lambda b: (0, 0))
    return pl.pallas_call(
        functools.partial(_dil_sample_kernel, seq=seq),
        grid=(batch,),
        in_specs=[row, row, row] + [seg_spec(s[0]) for s in segs] + [seg_spec(s[1]) for s in segs]
                 + [const(b) for b in biases],
        out_specs=row,
        out_shape=jax.ShapeDtypeStruct((batch * seq, DIL_W), F32),
        compiler_params=_params("parallel"),
        name="dil_sample",
    )(pb, k_new, v_new, *[s[0] for s in segs], *[s[1] for s in segs], *biases)


def _out_proj_kernel(oa_ref, ob_ref, x_ref, wo_ref, g_ref, wq_ref, h_ref, q_ref, *, qscale):
    o = jnp.concatenate([oa_ref[...].astype(BF16), ob_ref[...].astype(BF16)], axis=1)
    h = x_ref[...] + jnp.dot(o, wo_ref[...], preferred_element_type=F32)
    h_ref[...] = h
    n = _rms(h, g_ref[...]).astype(BF16)
    q_ref[...] = (jnp.dot(n, wq_ref[...], preferred_element_type=F32) * qscale).astype(q_ref.dtype)


def _out_proj(oa, ob, x, wo, g, wq, q_dtype, tm):
    n, d = x.shape
    row = lambda i: (i, 0)
    const = lambda i: (0, 0)
    return pl.pallas_call(
        functools.partial(_out_proj_kernel, qscale=(d // MEM_HEADS) ** -0.5),
        grid=(n // tm,),
        in_specs=[pl.BlockSpec((tm, GLA_V), row), pl.BlockSpec((tm, DIL_W), row), pl.BlockSpec((tm, d), row),
                  pl.BlockSpec(wo.shape, const), pl.BlockSpec((1, d), const), pl.BlockSpec(wq.shape, const)],
        out_specs=[pl.BlockSpec((tm, d), row), pl.BlockSpec((tm, d), row)],
        out_shape=[jax.ShapeDtypeStruct((n, d), F32), jax.ShapeDtypeStruct((n, d), q_dtype)],
        compiler_params=_params("parallel"),
        name="out_proj",
    )(oa, ob, x, wo, g, wq)


def _cross_kernel(q_ref, mk_ref, mv_ref, o_ref):
    hd = q_ref.shape[1] // MEM_HEADS
    for h in range(MEM_HEADS):
        sl = slice(h * hd, (h + 1) * hd)
        s = lax.dot_general(q_ref[:, sl].astype(BF16), mk_ref[:, sl].astype(BF16), NT, preferred_element_type=F32)
        p = jnp.exp(s - jnp.max(s, axis=-1, keepdims=True))
        l = jnp.sum(p, axis=-1, keepdims=True)
        o = jnp.dot(p.astype(BF16), mv_ref[:, sl].astype(BF16), preferred_element_type=F32) * (1.0 / l)
        o_ref[:, sl] = o.astype(o_ref.dtype)


def _cross(q, mk, mv, batch, seq, tq, out_dtype):
    n, d = q.shape
    nt = seq // tq
    nm = mk.shape[1]
    return pl.pallas_call(
        _cross_kernel,
        grid=(batch, nt),
        in_specs=[pl.BlockSpec((tq, d), lambda b, i: (b * nt + i, 0)),
                  pl.BlockSpec((None, nm, d), lambda b, i: (b, 0, 0)),
                  pl.BlockSpec((None, nm, d), lambda b, i: (b, 0, 0))],
        out_specs=pl.BlockSpec((tq, d), lambda b, i: (b * nt + i, 0)),
        out_shape=jax.ShapeDtypeStruct((n, d), out_dtype),
        compiler_params=_params("parallel", "parallel"),
        name="cross_attn",
    )(q, mk, mv)


def _co_proj_kernel(o_ref, h_ref, wo_ref, g_ref, wrh_ref, wrl_ref, br_ref, h2_ref, n3_ref, lg_ref):
    h2 = h_ref[...] + jnp.dot(o_ref[...].astype(BF16), wo_ref[...], preferred_element_type=F32)
    h2_ref[...] = h2
    n3 = _rms(h2, g_ref[...])
    hi = n3.astype(BF16)
    lo = (n3 - hi.astype(F32)).astype(BF16)
    n3_ref[...] = hi
    lg_ref[...] = (jnp.dot(hi, wrh_ref[...], preferred_element_type=F32)
                   + jnp.dot(lo, wrh_ref[...], preferred_element_type=F32)
                   + jnp.dot(hi, wrl_ref[...], preferred_element_type=F32) + br_ref[...])


def _co_proj(o, h, wo, g, wrh, wrl, br, tm):
    n, d = h.shape
    row = lambda i: (i, 0)
    const = lambda i: (0, 0)
    return pl.pallas_call(
        _co_proj_kernel,
        grid=(n // tm,),
        in_specs=[pl.BlockSpec((tm, d), row), pl.BlockSpec((tm, d), row), pl.BlockSpec(wo.shape, const),
                  pl.BlockSpec((1, d), const), pl.BlockSpec(wrh.shape, const), pl.BlockSpec(wrl.shape, const),
                  pl.BlockSpec((1, LANE), const)],
        out_specs=[pl.BlockSpec((tm, d), row), pl.BlockSpec((tm, d), row), pl.BlockSpec((tm, LANE), row)],
        out_shape=[jax.ShapeDtypeStruct((n, d), F32), jax.ShapeDtypeStruct((n, d), BF16),
                   jax.ShapeDtypeStruct((n, LANE), F32)],
        compiler_params=_params("parallel"),
        name="co_proj",
    )(o, h, wo, g, wrh, wrl, br)


META_E, META_POS, META_W = 0, TOP_K, 2 * TOP_K


def _router_kernel(lg_ref, tri_ref, meta_ref, tile_ref, cnt_ref, run_ref, *, chunk):
    @pl.when(pl.program_id(0) == 0)
    def _():
        run_ref[...] = jnp.zeros_like(run_ref)

    lg = lg_ref[...]
    tm = lg.shape[0]
    lane = lax.broadcasted_iota(I32, (tm, LANE), 1)
    lane_f = lane.astype(F32)
    ninf = -jnp.inf
    first = lambda hit: jnp.min(jnp.where(hit, lane_f, float(LANE)), axis=-1, keepdims=True).astype(I32)

    gl = jnp.where(lane < N_GROUPS, lg, ninf)
    gmax = jnp.max(gl, axis=-1, keepdims=True)
    gidx = first(gl == gmax)
    pg = 1.0 / jnp.sum(jnp.exp(gl - gmax), axis=-1, keepdims=True)

    ex = lane - N_GROUPS
    el = jnp.where((ex >= 0) & (ex < N_EXPERTS) & ((ex >> int(math.log2(EXP_PER_GROUP))) == gidx), lg, ninf)
    t1 = jnp.max(el, axis=-1, keepdims=True)
    i1 = first(el == t1)
    el2 = jnp.where(lane == i1, ninf, el)
    t2 = jnp.max(el2, axis=-1, keepdims=True)
    i2 = first(el2 == t2)
    e = jnp.exp(t2 - t1)
    gates = (pg / (1.0 + e), pg * e / (1.0 + e))
    experts = (i1 - N_GROUPS, i2 - N_GROUPS)

    hits = [lane == ex_j for ex_j in experts]
    onehot = jnp.where(hits[0] | hits[1], 1.0, 0.0)
    before = jnp.dot(tri_ref[...], onehot.astype(BF16), preferred_element_type=F32)
    count = jnp.sum(onehot, axis=0, keepdims=True)
    slots = jnp.floor((count + (chunk - 1)) * (1.0 / chunk)) * chunk
    pos = [jnp.sum(jnp.where(lane < ex_j, slots, 0.0) + jnp.where(hit, before, 0.0), axis=-1, keepdims=True)
           for ex_j, hit in zip(experts, hits)]

    meta = jnp.zeros((tm, LANE), F32)
    for base, vals in ((META_E, [x.astype(F32) for x in experts]), (META_POS, pos), (META_W, gates)):
        for j, val in enumerate(vals):
            meta = jnp.where(lane == base + j, val, meta)
    meta_ref[...] = meta

    sub = lax.broadcasted_iota(I32, (SUBLANE, LANE), 0)
    tile_ref[...] = jnp.where(sub == 0, run_ref[...], jnp.where(sub == 1, slots, 0.0))
    run_ref[...] = run_ref[...] + slots
    cnt_ref[...] = run_ref[...]


def _router(logits, tm, chunk):
    n = logits.shape[0]
    tri = jnp.asarray(np.tril(np.ones((tm, tm), np.float32), -1), BF16)
    return pl.pallas_call(
        functools.partial(_router_kernel, chunk=chunk),
        grid=(n // tm,),
        in_specs=[pl.BlockSpec((tm, LANE), lambda i: (i, 0)), pl.BlockSpec((tm, tm), lambda i: (0, 0))],
        out_specs=[pl.BlockSpec((tm, LANE), lambda i: (i, 0)),
                   pl.BlockSpec((SUBLANE, LANE), lambda i: (i, 0)),
                   pl.BlockSpec((SUBLANE, LANE), lambda i: (0, 0))],
        out_shape=[jax.ShapeDtypeStruct((n, LANE), F32),
                   jax.ShapeDtypeStruct((n // tm * SUBLANE, LANE), F32),
                   jax.ShapeDtypeStruct((SUBLANE, LANE), F32)],
        scratch_shapes=[pltpu.VMEM((SUBLANE, LANE), F32)],
        compiler_params=_params("arbitrary"),
        name="router",
    )(logits, tri)


TAB_BASE, TAB_OFF, TAB_N, TAB_TOTAL = 0, N_EXPERTS, 2 * N_EXPERTS, 3 * N_EXPERTS
FILL_BASE, FILL_N, FILL_TAIL = 0, N_EXPERTS, 2 * N_EXPERTS
FILL_ROWS = 64


def _one_hot(meta, lane_pos, slot):
    return jnp.where(lane_pos == meta[:, META_POS + slot:META_POS + slot + 1], 1.0, 0.0).astype(BF16)


def _chunk_copy(sorted_ref, hbm_ref, sem, chunk, to_hbm, off, base):
    src, dst = sorted_ref.at[pl.ds(off, chunk), :], hbm_ref.at[pl.ds(base, chunk), :]
    return pltpu.make_async_copy(src, dst, sem) if to_hbm else pltpu.make_async_copy(dst, src, sem)


def _start_chunks(tab_ref, sorted_ref, hbm_ref, sem, chunk, to_hbm):
    def per_expert(e, carry):
        base, off = tab_ref[0, TAB_BASE + e], tab_ref[0, TAB_OFF + e]

        def one(c, carry):
            _chunk_copy(sorted_ref, hbm_ref, sem, chunk, to_hbm, pl.multiple_of(off + c * chunk, chunk),
                        pl.multiple_of(base + c * chunk, chunk)).start()
            return carry

        return lax.fori_loop(0, tab_ref[0, TAB_N + e], one, carry)

    lax.fori_loop(0, N_EXPERTS, per_expert, 0)


def _wait_chunks(tab_ref, sorted_ref, hbm_ref, sem, chunk, to_hbm):
    def drain(c, carry):
        _chunk_copy(sorted_ref, hbm_ref, sem, chunk, to_hbm, 0, 0).wait()
        return carry

    lax.fori_loop(0, tab_ref[0, TAB_TOTAL], drain, 0)


def _dispatch_kernel(tab_ref, prev_ref, fill_ref, x_ref, meta_ref, xs_ref, sorted_ref, zero_ref, sems, *, chunk):
    i = pl.program_id(0)
    slot = lax.rem(i, 2)
    tm, npos = x_ref.shape[0], sorted_ref.shape[1]
    meta = meta_ref[...]
    lane_pos = lax.broadcasted_iota(I32, (tm, npos), 1).astype(F32)
    place = _one_hot(meta, lane_pos, 0) + _one_hot(meta, lane_pos, 1)
    sorted_ref[slot] = lax.dot_general(place, x_ref[...], TN, preferred_element_type=F32)

    @pl.when(i > 0)
    def _():
        _wait_chunks(prev_ref, sorted_ref.at[1 - slot], xs_ref, sems.at[1 - slot], chunk, True)

    _start_chunks(tab_ref, sorted_ref.at[slot], xs_ref, sems.at[slot], chunk, True)

    @pl.when(i == pl.num_programs(0) - 1)
    def _():
        _wait_chunks(tab_ref, sorted_ref.at[slot], xs_ref, sems.at[slot], chunk, True)
        zero_ref[...] = jnp.zeros_like(zero_ref)
        big = zero_ref.shape[0]
        sem = sems.at[0]

        def fill(row, size):
            return pltpu.make_async_copy(zero_ref.at[pl.ds(0, size), :],
                                         xs_ref.at[pl.ds(pl.multiple_of(row, size), size), :], sem)

        def per_expert(e, total):
            def one(c, carry):
                fill(fill_ref[0, FILL_BASE + e] + c * chunk, chunk).start()
                return carry
            lax.fori_loop(0, fill_ref[0, FILL_N + e], one, 0)
            return total + fill_ref[0, FILL_N + e]

        def drain(c, carry):
            fill(0, chunk).wait()
            return carry

        lax.fori_loop(0, lax.fori_loop(0, N_EXPERTS, per_expert, 0), drain, 0)

        def tail(c, carry):
            fill(fill_ref[0, FILL_TAIL] + c * big, big).start()
            return carry

        def drain_tail(c, carry):
            fill(0, big).wait()
            return carry

        lax.fori_loop(0, fill_ref[0, FILL_TAIL + 1], tail, 0)
        lax.fori_loop(0, fill_ref[0, FILL_TAIL + 1], drain_tail, 0)


def _dispatch(x, meta, tab, fill, rows, tm, chunk):
    n, d = x.shape
    npos = TOP_K * tm + N_EXPERTS * chunk
    return pl.pallas_call(
        functools.partial(_dispatch_kernel, chunk=chunk),
        grid=(n // tm,),
        in_specs=[pl.BlockSpec((None, 1, LANE), lambda i: (i, 0, 0), memory_space=pltpu.SMEM),
                  pl.BlockSpec((None, 1, LANE), lambda i: (jnp.maximum(i - 1, 0), 0, 0), memory_space=pltpu.SMEM),
                  pl.BlockSpec((1, LANE), lambda i: (0, 0), memory_space=pltpu.SMEM),
                  pl.BlockSpec((tm, d), lambda i: (i, 0)),
                  pl.BlockSpec((tm, LANE), lambda i: (i, 0))],
        out_specs=pl.BlockSpec(memory_space=pl.ANY),
        out_shape=jax.ShapeDtypeStruct((rows, d), F32),
        scratch_shapes=[pltpu.VMEM((2, npos, d), F32), pltpu.VMEM((FILL_ROWS, d), F32),
                        pltpu.SemaphoreType.DMA((2,))],
        compiler_params=_params("arbitrary"),
        name="moe_dispatch",
    )(tab, tab, fill, x, meta)


def _expert_kernel(be_ref, nu_ref, x_ref, w1_ref, w3_ref, w2_ref, y_ref):
    del be_ref
    live = pl.program_id(0) < nu_ref[0]

    @pl.when(live)
    def _():
        x = x_ref[...].astype(BF16)
        a = jnp.dot(x, w1_ref[...], preferred_element_type=F32)
        b = jnp.dot(x, w3_ref[...], preferred_element_type=F32)
        y_ref[...] = jnp.dot((_silu(a) * b).astype(BF16), w2_ref[...], preferred_element_type=F32)

    @pl.when(jnp.logical_not(live))
    def _():
        y_ref[...] = jnp.zeros_like(y_ref)


def _experts(xs, block_e, n_used, w1, w3, w2, bm):
    rows, d = xs.shape
    de = w1.shape[2]
    return pl.pallas_call(
        _expert_kernel,
        grid_spec=pltpu.PrefetchScalarGridSpec(
            num_scalar_prefetch=2,
            grid=(rows // bm,),
            in_specs=[pl.BlockSpec((bm, d), lambda i, be, nu: (i, 0)),
                      pl.BlockSpec((None, d, de), lambda i, be, nu: (be[i], 0, 0)),
                      pl.BlockSpec((None, d, de), lambda i, be, nu: (be[i], 0, 0)),
                      pl.BlockSpec((None, de, d), lambda i, be, nu: (be[i], 0, 0))],
            out_specs=pl.BlockSpec((bm, d), lambda i, be, nu: (i, 0))),
        out_shape=jax.ShapeDtypeStruct((rows, d), F32),
        compiler_params=_params("arbitrary"),
        name="moe_experts",
    )(block_e, n_used, xs, w1, w3, w2)


def _combine_kernel(tab_ref, next_ref, h_ref, meta_ref, g_ref, yb_ref, y_ref, sorted_ref, sems, *, chunk):
    i = pl.program_id(0)
    slot = lax.rem(i, 2)

    @pl.when(i == 0)
    def _():
        sorted_ref[...] = jnp.zeros_like(sorted_ref)
        _start_chunks(tab_ref, sorted_ref.at[0], yb_ref, sems.at[0], chunk, False)

    @pl.when(i + 1 < pl.num_programs(0))
    def _():
        _start_chunks(next_ref, sorted_ref.at[1 - slot], yb_ref, sems.at[1 - slot], chunk, False)

    _wait_chunks(tab_ref, sorted_ref.at[slot], yb_ref, sems.at[slot], chunk, False)
    tm, npos = h_ref.shape[0], sorted_ref.shape[1]
    meta = meta_ref[...]
    lane_pos = lax.broadcasted_iota(I32, (tm, npos), 1).astype(F32)
    yb = sorted_ref[slot].astype(BF16)
    moe = (meta[:, META_W:META_W + 1] * jnp.dot(_one_hot(meta, lane_pos, 0), yb, preferred_element_type=F32)
           + meta[:, META_W + 1:META_W + 2] * jnp.dot(_one_hot(meta, lane_pos, 1), yb, preferred_element_type=F32))
    y_ref[...] = _rms(h_ref[...] + moe, g_ref[...])


def _combine(h, meta, tab, yb, g, tm, chunk):
    n, d = h.shape
    npos = TOP_K * tm + N_EXPERTS * chunk
    return pl.pallas_call(
        functools.partial(_combine_kernel, chunk=chunk),
        grid=(n // tm,),
        in_specs=[pl.BlockSpec((None, 1, LANE), lambda i: (i, 0, 0), memory_space=pltpu.SMEM),
                  pl.BlockSpec((None, 1, LANE), lambda i: (jnp.minimum(i + 1, n // tm - 1), 0, 0),
                               memory_space=pltpu.SMEM),
                  pl.BlockSpec((tm, d), lambda i: (i, 0)),
                  pl.BlockSpec((tm, LANE), lambda i: (i, 0)),
                  pl.BlockSpec((1, d), lambda i: (0, 0)),
                  pl.BlockSpec(memory_space=pl.ANY)],
        out_specs=pl.BlockSpec((tm, d), lambda i: (i, 0)),
        out_shape=jax.ShapeDtypeStruct((n, d), F32),
        scratch_shapes=[pltpu.VMEM((2, npos, d), F32), pltpu.SemaphoreType.DMA((2,))],
        compiler_params=_params("arbitrary"),
        name="moe_combine",
    )(tab, tab, h, meta, g, yb)


def _moe_final(h2, n3, logits, w1, w3, w2, g_final, tm, bm, chunk):
    n = h2.shape[0]
    nt = n // tm
    assert 3 * N_EXPERTS < LANE and bm % chunk == 0
    meta, tiles, cnt = _router(logits, tm, chunk)
    rows_e = cnt[0, :N_EXPERTS].astype(I32)
    padded = (rows_e + bm - 1) // bm * bm
    pad_end = jnp.cumsum(padded)
    pad_start = pad_end - padded
    tiles = tiles.reshape(nt, SUBLANE, LANE)
    run_before = tiles[:, 0, :N_EXPERTS].astype(I32)
    nchunks = tiles[:, 1, :N_EXPERTS].astype(I32) // chunk
    offs = chunk * (jnp.cumsum(nchunks, axis=1) - nchunks)
    pad_lanes = lambda a: jnp.pad(a, ((0, 0), (0, LANE - a.shape[1])))
    tab = pad_lanes(jnp.concatenate([pad_start[None, :] + run_before, offs, nchunks,
                                     jnp.sum(nchunks, axis=1, keepdims=True)], axis=1)).reshape(nt, 1, LANE)
    max_rows = n * TOP_K + (chunk - 1) * min(n * TOP_K, nt * N_EXPERTS) + N_EXPERTS * (bm - 1)
    nb = -(-max_rows // bm)
    assert bm % FILL_ROWS == 0
    fill = pad_lanes(jnp.concatenate([pad_start + rows_e, (padded - rows_e) // chunk,
                                      pad_end[-1:], (nb * bm - pad_end[-1:]) // FILL_ROWS])[None, :])
    block_start = jnp.arange(nb, dtype=I32) * bm
    block_e = jnp.minimum(jnp.sum((pad_end[None, :] <= block_start[:, None]).astype(I32), axis=1), N_EXPERTS - 1)
    n_used = (pad_end[-1:] // bm).astype(I32)
    xs = _dispatch(n3, meta, tab, fill, nb * bm, tm, chunk)
    yb = _experts(xs, block_e, n_used, w1, w3, w2, bm)
    return _combine(h2, meta, tab, yb, g_final, tm, chunk)


def _group(x3, mk, mv, w, keep, *, cache=None, state=None):
    batch, seq, d = x3.shape
    n = batch * seq
    x = x3.reshape(n, d)
    tm = _tile(n, TOKEN_TILE)
    sample = cache is not None
    pa, pb, k_new, v_new = _norm_proj(x, w["g1"], w["w_in"], batch, seq, keep, not sample, tm)
    if sample:
        assert keep == seq
        oa, st = _gla_sample(pa, _state_to_blockdiag_t(state), w["wgk"], w["bgk"], w["gg"], batch, seq,
                             rows=GLA_CHUNK)
        ob = _dil_sample(pb, k_new, v_new, cache[0].reshape(batch, -1, DIL_W), cache[1].reshape(batch, -1, DIL_W),
                         batch, seq)
    else:
        zeros = jnp.zeros((batch, GLA_V, GLA_QK), F32)
        oa, st = _gla_prompt(pa, zeros, w["wgk"], w["bgk"], w["gg"], batch, seq, tb=_tile(seq, TOKEN_TILE))
        ob = _dil_prompt(pb, batch, seq)
    h1, qc = _out_proj(oa, ob, x, w["w_out"], w["g2"], w["w_cq"], F32 if sample else BF16, tm)
    oc = _cross(qc, mk, mv, batch, seq, _tile(seq, TOKEN_TILE), F32 if sample else BF16)
    h2, n3, logits = _co_proj(oc, h1, w["w_co"], w["g3"], w["wr_hi"], w["wr_lo"], w["br"], tm)
    y = _moe_final(h2, n3, logits, w["w_e1"], w["w_e3"], w["w_e2"], w["g_final"], _tile(n, MOE_TILE), MOE_BLOCK,
                   MOE_CHUNK)
    return (y.reshape(batch, seq, d), _blockdiag_t_to_state(st),
            k_new.reshape(batch, keep, DIL_HEADS, DIL_HD), v_new.reshape(batch, keep, DIL_HEADS, DIL_HD))


def _pack_weights(l, g_norm1, w_in, w_gk2, b_gk, g_gla_out, w_out, g_norm2, w_cq, w_co, g_norm3, w_gr, b_gr,
                  w_er, b_er, w_e1, w_e3, w_e2, g_final):
    d = w_in.shape[1]
    sp = np.cumsum([GLA_QK, GLA_QK, GLA_V, GATE_RANK, GLA_V, DIL_W, DIL_W, DIL_W])
    wi = w_in[l]
    q_a, k_a, v_a, glr, r_a, q_b, k_b, v_b = (wi[:, a:b] for a, b in zip([0, *sp[:-1]], sp))
    glr = jnp.pad(glr, ((0, 0), (0, LANE - GATE_RANK)))
    wr = jnp.pad(jnp.concatenate([w_gr[l], w_er[l]], axis=1), ((0, 0), (0, LANE - N_GROUPS - N_EXPERTS)))
    wr_hi = wr.astype(BF16)
    return dict(
        g1=g_norm1[l].reshape(1, d), g2=g_norm2[l].reshape(1, d), g3=g_norm3[l].reshape(1, d),
        g_final=g_final.reshape(1, d),
        w_in=jnp.concatenate([q_a, k_a, v_a, r_a, glr, q_b, k_b, v_b], axis=1).astype(BF16),
        wgk=jnp.pad(w_gk2[l], ((0, LANE - GATE_RANK), (0, 0))).astype(BF16),
        bgk=b_gk[l].reshape(1, GLA_QK), gg=g_gla_out[l].reshape(1, GLA_DV),
        w_out=w_out[l].astype(BF16), w_cq=w_cq[l].astype(BF16), w_co=w_co[l].astype(BF16),
        wr_hi=wr_hi, wr_lo=(wr - wr_hi.astype(F32)).astype(BF16),
        br=jnp.pad(jnp.concatenate([b_gr[l], b_er[l]]), (0, LANE - N_GROUPS - N_EXPERTS)).reshape(1, LANE),
        w_e1=w_e1[l].astype(BF16), w_e3=w_e3[l].astype(BF16), w_e2=w_e2[l].astype(BF16))


def kernel(x_prompt, x_sample, cache_swa_k, cache_swa_v, state_gla, cache_mem_k, cache_mem_v, mem_prompt,
           g_norm1, w_in, w_gk2, b_gk, g_gla_out, w_out, g_norm2, g_mem, w_cq, w_mk, w_mv, w_co,
           g_norm3, w_gr, b_gr, w_er, b_er, w_e1, w_e3, w_e2, g_final):
    depth = w_in.shape[0]
    assert depth == 1, "the final norm is fused into the last MoE stage; stacked layers are not supported"
    batch, seq, d = x_prompt.shape
    sb, sseq, _ = x_sample.shape
    nm = mem_prompt.shape[1]
    keep = min(DIL_PATTERNS[-1][0], seq)
    l = 0
    w = _pack_weights(l, g_norm1, w_in, w_gk2, b_gk, g_gla_out, w_out, g_norm2, w_cq, w_co, g_norm3,
                      w_gr, b_gr, w_er, b_er, w_e1, w_e3, w_e2, g_final)
    mkv = _norm_matmul(mem_prompt.reshape(batch * nm, d), g_mem[l].reshape(1, d),
                       jnp.concatenate([w_mk[l], w_mv[l]], axis=1).astype(BF16), _tile(batch * nm, TOKEN_TILE))
    mk = mkv[:, :d].reshape(batch, nm, d)
    mv = mkv[:, d:].reshape(batch, nm, d)
    yp, sp, kp, vp = _group(x_prompt, mk, mv, w, keep)
    past = cache_swa_k.shape[2]
    ys, ss, kn, vn = _group(x_sample, cache_mem_k.reshape(sb, nm, d), cache_mem_v.reshape(sb, nm, d), w, sseq,
                            cache=(cache_swa_k.reshape(sb, past, DIL_W), cache_swa_v.reshape(sb, past, DIL_W)),
                            state=state_gla[l])
    heads = lambda m: m.reshape(batch, nm, MEM_HEADS, d // MEM_HEADS)
    return (yp, ys, *(o[None] for o in (kp, vp, sp, heads(mk), heads(mv), kn, vn, ss)))
```

```python
import functools
import math

import numpy as np
import jax
import jax.numpy as jnp
from jax import lax
from jax.experimental import pallas as pl
from jax.experimental.pallas import tpu as pltpu

F32, BF16, I32 = jnp.float32, jnp.bfloat16, jnp.int32

GLA_HEADS, GLA_DK, GLA_DV = 4, 64, 128
GLA_QK, GLA_V = GLA_HEADS * GLA_DK, GLA_HEADS * GLA_DV
GATE_RANK, GATE_NORM, GLA_CHUNK = 16, 16.0, 64
DIL_HEADS, DIL_HD = 8, 64
DIL_W = DIL_HEADS * DIL_HD
DIL_PATTERNS = ((128, 1), (512, 4), (2048, 16))
DIL_KEYS = 128
MEM_HEADS = 4
N_GROUPS, EXP_PER_GROUP, TOP_K = 4, 8, 2
N_EXPERTS = N_GROUPS * EXP_PER_GROUP
EPS = 1e-6

LANE = 128
SUBLANE = 8
VMEM_LIMIT_BYTES = 56 * 1024 * 1024

PA_W = 2 * GLA_QK + 2 * GLA_V + LANE
PB_W = 3 * DIL_W
HALF_W = DIL_W // 2
HI_MASK = -65536

TOKEN_TILE = 512
MOE_TILE = 256
MOE_BLOCK = 512
MOE_CHUNK = SUBLANE

NT = (((1,), (1,)), ((), ()))
TN = (((0,), (0,)), ((), ()))


def _params(*sem):
    return pltpu.CompilerParams(dimension_semantics=sem, vmem_limit_bytes=VMEM_LIMIT_BYTES)


def _tile(n, want):
    t = min(n, want)
    assert n % t == 0
    return t


def _rms(x, g):
    y = x * lax.rsqrt(jnp.mean(x * x, axis=-1, keepdims=True) + EPS)
    return y * g


def _silu(x):
    return x / (1.0 + jnp.exp(-x))


def _iota_div(shape, dim, n):
    assert n & (n - 1) == 0
    return lax.broadcasted_iota(I32, shape, dim) >> int(math.log2(n))


def _pack_halves(r):
    u = lax.bitcast_convert_type(r.astype(BF16).astype(F32), I32)
    lo = u[:, :HALF_W]
    return lax.shift_right_logical(lo, jnp.full_like(lo, 16)) | (u[:, HALF_W:] & HI_MASK)


def _unpack_halves(w):
    lo = lax.bitcast_convert_type(w << 16, F32).astype(BF16)
    hi = lax.bitcast_convert_type(w & HI_MASK, F32).astype(BF16)
    return lo, hi


def _norm_proj_kernel(x_ref, g_ref, w_ref, pa_ref, pb_ref, k_ref, v_ref, *, pack):
    n = _rms(x_ref[...], g_ref[...]).astype(BF16)
    for c0 in range(0, PA_W, DIL_W):
        c1 = min(c0 + DIL_W, PA_W)
        pa_ref[:, c0:c1] = jnp.dot(n, w_ref[:, c0:c1], preferred_element_type=F32).astype(pa_ref.dtype)
    for j, kv_ref in enumerate((None, k_ref, v_ref)):
        r = jnp.dot(n, w_ref[:, PA_W + j * DIL_W:PA_W + (j + 1) * DIL_W], preferred_element_type=F32)
        if pack:
            words = _pack_halves(r)
            for part in range(HALF_W // LANE):
                pb_ref[j * (HALF_W // LANE) + part] = words[:, part * LANE:(part + 1) * LANE]
        else:
            pb_ref[:, j * DIL_W:(j + 1) * DIL_W] = r
        if kv_ref is not None:
            kv_ref[...] = r


def _norm_proj(x, g, w, batch, seq, keep, pack, tm):
    n, d = x.shape
    if keep == seq:
        kv_map = lambda i: (i, 0)
    else:
        nt, nk = seq // tm, keep // tm
        assert keep % tm == 0
        kv_map = lambda i: ((i // nt) * nk + jnp.maximum(i % nt - (nt - nk), 0), 0)
    if pack:
        nparts = 3 * HALF_W // LANE
        pb_spec = pl.BlockSpec((nparts, tm, LANE), lambda i: (0, i, 0))
        pb_shape = jax.ShapeDtypeStruct((nparts, n, LANE), I32)
    else:
        pb_spec = pl.BlockSpec((tm, PB_W), lambda i: (i, 0))
        pb_shape = jax.ShapeDtypeStruct((n, PB_W), F32)
    pa_dtype = BF16 if pack else F32
    return pl.pallas_call(
        functools.partial(_norm_proj_kernel, pack=pack),
        grid=(n // tm,),
        in_specs=[pl.BlockSpec((tm, d), lambda i: (i, 0)),
                  pl.BlockSpec((1, d), lambda i: (0, 0)),
                  pl.BlockSpec((d, PA_W + PB_W), lambda i: (0, 0))],
        out_specs=[pl.BlockSpec((tm, PA_W), lambda i: (i, 0)),
                   pb_spec,
                   pl.BlockSpec((tm, DIL_W), kv_map),
                   pl.BlockSpec((tm, DIL_W), kv_map)],
        out_shape=[jax.ShapeDtypeStruct((n, PA_W), pa_dtype),
                   pb_shape,
                   jax.ShapeDtypeStruct((batch * keep, DIL_W), F32),
                   jax.ShapeDtypeStruct((batch * keep, DIL_W), F32)],
        compiler_params=_params("arbitrary"),
        name="norm_proj",
    )(x, g, w)


def _norm_matmul_kernel(x_ref, g_ref, w_ref, o_ref):
    n = _rms(x_ref[...], g_ref[...]).astype(BF16)
    for c0 in range(0, o_ref.shape[1], DIL_W):
        o_ref[:, c0:c0 + DIL_W] = jnp.dot(n, w_ref[:, c0:c0 + DIL_W], preferred_element_type=F32)


def _norm_matmul(x, g, w, tm):
    n, d = x.shape
    m = w.shape[1]
    return pl.pallas_call(
        _norm_matmul_kernel,
        grid=(n // tm,),
        in_specs=[pl.BlockSpec((tm, d), lambda i: (i, 0)),
                  pl.BlockSpec((1, d), lambda i: (0, 0)),
                  pl.BlockSpec((d, m), lambda i: (0, 0))],
        out_specs=pl.BlockSpec((tm, m), lambda i: (i, 0)),
        out_shape=jax.ShapeDtypeStruct((n, m), F32),
        compiler_params=_params("parallel"),
        name="norm_matmul",
    )(x, g, w)


def _gla_tables(chunk, seg):
    idx = np.arange(chunk)
    tril = ((idx[None, :] <= idx[:, None]) & (idx[None, :] // seg == idx[:, None] // seg)).astype(np.float32)
    masks, levels = [], []
    s = seg // 2
    while s >= 1:
        same = (idx[:, None] // (2 * s)) == (idx[None, :] // (2 * s))
        masks.append(same & ((idx[:, None] // s) % 2 == 1) & ((idx[None, :] // s) % 2 == 0))
        levels.append(s)
        s //= 2
    masks.append(idx[:, None] == idx[None, :])
    pm = np.tile(np.stack(masks).astype(np.float32), (1, 1, GLA_HEADS))
    return jnp.asarray(tril, BF16), jnp.asarray(pm), tuple(levels)


def _block_row(b, blk, idx):
    c, w = b.shape
    b3 = b.reshape(c // blk, blk, w)
    return jnp.broadcast_to(b3[:, idx:idx + 1, :], (c // blk, blk, w)).reshape(c, w)


def _level_ref(b, s, row):
    c = b.shape[0]
    if 2 * s >= SUBLANE:
        return _block_row(b, 2 * s, s - 1)
    down = lambda n: pltpu.roll(b, n, 0)
    if s == 2:
        m = row & 3
        return jnp.where(m == 0, pltpu.roll(b, c - 1, 0), jnp.where(m == 1, b, jnp.where(m == 2, down(1), down(2))))
    assert s == 1
    return jnp.where((row & 1) == 1, down(1), b)


def _gla_chunk(q, k, v, glr, wgk, bgk, tril_ref, pm_ref, levels, seg):
    c = q.shape[0]
    nl = len(levels)
    gk = jnp.dot(glr, wgk, preferred_element_type=F32) + bgk
    la = (jnp.minimum(gk, 0.0) - jnp.log(1.0 + jnp.exp(-jnp.abs(gk)))) * (1.0 / GATE_NORM)
    hi = la.astype(BF16)
    r1 = la - hi.astype(F32)
    mid = r1.astype(BF16)
    lo = (r1 - mid.astype(F32)).astype(BF16)
    b3 = jnp.dot(tril_ref[...], jnp.concatenate([hi, mid, lo], axis=1), preferred_element_type=F32)
    b = b3[:, :GLA_QK] + b3[:, GLA_QK:2 * GLA_QK] + b3[:, 2 * GLA_QK:]
    b_end = _block_row(b, seg, seg - 1)

    row = lax.broadcasted_iota(I32, (c, GLA_QK), 0)
    khead = _iota_div((c, GLA_QK), 1, GLA_DK)
    vhead = _iota_div((c, GLA_V), 1, GLA_DV)

    def by_head(x, head):
        return jnp.concatenate([jnp.where(head == h, x, jnp.zeros_like(x)) for h in range(GLA_HEADS)], axis=0)

    def level(qe, ke, pm):
        a = lax.dot_general(qe, by_head(ke, khead), NT, preferred_element_type=F32)
        return jnp.where(pm > 0.0, a, 0.0)

    acc = level(q.astype(BF16), k.astype(BF16), pm_ref[nl])
    for l, s in enumerate(levels):
        ref = _level_ref(b, s, row)
        right = ((row >> int(math.log2(s))) & 1) == 1
        qe = (q * jnp.exp(jnp.where(right, b - ref, 0.0))).astype(BF16)
        ke = (k * jnp.exp(jnp.where(right, 0.0, ref - b))).astype(BF16)
        acc = acc + level(qe, ke, pm_ref[l])

    o = jnp.dot(acc.astype(BF16), by_head(v, vhead), preferred_element_type=F32)
    qb = (q * jnp.exp(b)).astype(BF16)
    kd = (k * jnp.exp(b_end - b)).astype(BF16)
    return o, qb, kd, b_end


def _gla_finish(o, r, gg):
    outs = []
    for h in range(GLA_HEADS):
        sl = slice(h * GLA_DV, (h + 1) * GLA_DV)
        outs.append(_rms(o[:, sl], gg) * _silu(r[:, sl]))
    return jnp.concatenate(outs, axis=1)


def _state_mask():
    return _iota_div((GLA_V, GLA_QK), 0, GLA_DV) == _iota_div((GLA_V, GLA_QK), 1, GLA_DK)


def _gla_prompt_kernel(q_ref, k_ref, v_ref, r_ref, glr_ref, wgk_ref, bgk_ref, gg_ref, tril_ref, pm_ref, s0_ref,
                       o_ref, sout_ref, st_ref, *, chunk, levels):
    i = pl.program_id(1)

    @pl.when(i == 0)
    def _():
        st_ref[...] = s0_ref[...]

    smask = _state_mask()

    def body(c, carry):
        rows = pl.ds(pl.multiple_of(c * chunk, chunk), chunk)
        q = q_ref[rows, :].astype(F32) * (GLA_DK ** -0.5)
        k = k_ref[rows, :].astype(F32)
        v = v_ref[rows, :]
        o, qb, kd, b_end = _gla_chunk(q, k, v, glr_ref[rows, :], wgk_ref[...], bgk_ref[...],
                                      tril_ref, pm_ref, levels, chunk)
        st = st_ref[...]
        o = o + lax.dot_general(qb, st.astype(BF16), NT, preferred_element_type=F32)
        u = lax.dot_general(v, kd, TN, preferred_element_type=F32)
        st_ref[...] = st * jnp.exp(b_end[0:1, :]) + jnp.where(smask, u, 0.0)
        o_ref[rows, :] = _gla_finish(o, r_ref[rows, :].astype(F32), gg_ref[...]).astype(o_ref.dtype)
        return carry

    lax.fori_loop(0, q_ref.shape[0] // chunk, body, 0, unroll=4)

    @pl.when(i == pl.num_programs(1) - 1)
    def _():
        sout_ref[...] = st_ref[...]


def _gla_prompt(pa, s0t, wgk, bgk, gg, batch, seq, tb):
    chunk = math.gcd(seq, GLA_CHUNK)
    tril, pm, levels = _gla_tables(chunk, chunk)
    nt = seq // tb
    row = lambda b, i: (b * nt + i, 0)
    const2 = lambda b, i: (0, 0)
    return pl.pallas_call(
        functools.partial(_gla_prompt_kernel, chunk=chunk, levels=levels),
        grid=(batch, nt),
        in_specs=[pl.BlockSpec((tb, GLA_QK), row),
                  pl.BlockSpec((tb, GLA_QK), lambda b, i: (b * nt + i, 1)),
                  pl.BlockSpec((tb, GLA_V), lambda b, i: (b * nt + i, 1)),
                  pl.BlockSpec((tb, GLA_V), lambda b, i: (b * nt + i, 2)),
                  pl.BlockSpec((tb, LANE), lambda b, i: (b * nt + i, (PA_W - LANE) // LANE)),
                  pl.BlockSpec(wgk.shape, const2),
                  pl.BlockSpec(bgk.shape, const2),
                  pl.BlockSpec(gg.shape, const2),
                  pl.BlockSpec(tril.shape, const2),
                  pl.BlockSpec(pm.shape, lambda b, i: (0, 0, 0)),
                  pl.BlockSpec((None, GLA_V, GLA_QK), lambda b, i: (b, 0, 0))],
        out_specs=[pl.BlockSpec((tb, GLA_V), row),
                   pl.BlockSpec((None, GLA_V, GLA_QK), lambda b, i: (b, 0, 0))],
        out_shape=[jax.ShapeDtypeStruct((batch * seq, GLA_V), BF16),
                   jax.ShapeDtypeStruct((batch, GLA_V, GLA_QK), F32)],
        scratch_shapes=[pltpu.VMEM((GLA_V, GLA_QK), F32)],
        compiler_params=_params("parallel", "arbitrary"),
        name="gla_prompt",
    )(pa, pa, pa, pa, pa, wgk, bgk, gg, tril, pm, s0t)


def _gla_sample_kernel(q_ref, k_ref, v_ref, r_ref, glr_ref, wgk_ref, bgk_ref, gg_ref, tril_ref, pm_ref, s0_ref,
                       o_ref, sout_ref, *, seg, levels):
    rows = q_ref.shape[0]
    q = q_ref[...] * (GLA_DK ** -0.5)
    v = v_ref[...].astype(BF16)
    o, qb, kd, b_end = _gla_chunk(q, k_ref[...], v, glr_ref[...].astype(BF16), wgk_ref[...], bgk_ref[...],
                                  tril_ref, pm_ref, levels, seg)
    smask = _state_mask()
    seq_o = _iota_div((rows, GLA_V), 0, seg)
    for j in range(rows // seg):
        st = s0_ref[j]
        oj = lax.dot_general(qb, st.astype(BF16), NT, preferred_element_type=F32)
        o = o + jnp.where(seq_o == j, oj, 0.0)
        u = lax.dot_general(jnp.where(seq_o == j, v, jnp.zeros_like(v)), kd, TN, preferred_element_type=F32)
        sout_ref[j] = st * jnp.exp(b_end[j * seg:j * seg + 1, :]) + jnp.where(smask, u, 0.0)
    o_ref[...] = _gla_finish(o, r_ref[...], gg_ref[...])


def _gla_sample(pa, s0t, wgk, bgk, gg, batch, seq, rows):
    seg = math.gcd(seq, GLA_CHUNK)
    assert seg == seq and seg % SUBLANE == 0, "sample sequences must be one sublane-aligned chunk"
    tril, pm, levels = _gla_tables(rows, seg)
    nseq = rows // seg
    row = lambda i: (i, 0)
    const2 = lambda i: (0, 0)
    return pl.pallas_call(
        functools.partial(_gla_sample_kernel, seg=seg, levels=levels),
        grid=(batch * seq // rows,),
        in_specs=[pl.BlockSpec((rows, GLA_QK), row),
                  pl.BlockSpec((rows, GLA_QK), lambda i: (i, 1)),
                  pl.BlockSpec((rows, GLA_V), lambda i: (i, 1)),
                  pl.BlockSpec((rows, GLA_V), lambda i: (i, 2)),
                  pl.BlockSpec((rows, LANE), lambda i: (i, (PA_W - LANE) // LANE)),
                  pl.BlockSpec(wgk.shape, const2),
                  pl.BlockSpec(bgk.shape, const2),
                  pl.BlockSpec(gg.shape, const2),
                  pl.BlockSpec(tril.shape, const2),
                  pl.BlockSpec(pm.shape, lambda i: (0, 0, 0)),
                  pl.BlockSpec((nseq, GLA_V, GLA_QK), lambda i: (i, 0, 0))],
        out_specs=[pl.BlockSpec((rows, GLA_V), row),
                   pl.BlockSpec((nseq, GLA_V, GLA_QK), lambda i: (i, 0, 0))],
        out_shape=[jax.ShapeDtypeStruct((batch * seq, GLA_V), F32),
                   jax.ShapeDtypeStruct((batch, GLA_V, GLA_QK), F32)],
        compiler_params=_params("parallel"),
        name="gla_sample",
    )(pa, pa, pa, pa, pa, wgk, bgk, gg, tril, pm, s0t)


def _state_to_blockdiag_t(s):
    b = s.shape[0]
    st = jnp.swapaxes(s, 2, 3)
    eye = jnp.eye(GLA_HEADS, dtype=s.dtype)
    return (st[:, :, :, None, :] * eye[None, :, None, :, None]).reshape(b, GLA_V, GLA_QK)


def _blockdiag_t_to_state(st):
    b = st.shape[0]
    s5 = st.reshape(b, GLA_HEADS, GLA_DV, GLA_HEADS, GLA_DK)
    return jnp.stack([jnp.swapaxes(s5[:, h, :, h, :], 1, 2) for h in range(GLA_HEADS)], axis=1)


def _alibi_slopes():
    return np.asarray([2.0 ** (-8.0 * (h + 1) / DIL_HEADS) for h in range(DIL_HEADS)], np.float64)


DIL_GROUP = 4
DIL_GW = DIL_GROUP * DIL_HD


def _dil_bias():
    i = np.arange(DIL_KEYS)[:, None]
    c = np.arange(2 * DIL_KEYS)[None, :]
    dist = DIL_KEYS + i - c
    ok = (dist >= 0) & (dist <= DIL_KEYS)
    out = np.empty((len(DIL_PATTERNS), DIL_HEADS, DIL_KEYS, 2 * DIL_KEYS), np.float32)
    for p, (_, d) in enumerate(DIL_PATTERNS):
        for h, sl in enumerate(_alibi_slopes()):
            out[p, h] = np.where(ok, -sl * (dist * d), -np.inf)
    out = out.reshape(len(DIL_PATTERNS), DIL_HEADS // DIL_GROUP, DIL_GROUP * DIL_KEYS, 2 * DIL_KEYS)
    return jnp.asarray(np.ascontiguousarray(out.transpose(0, 1, 3, 2)))


def _dil_prompt_kernel(qkv_hbm, biast_ref, o_ref, qkv_ref, acc_ref, m_ref, l_ref, sem):
    seq = qkv_ref.shape[1]
    blk = DIL_KEYS
    ngroups = DIL_HEADS // DIL_GROUP
    wparts = HALF_W // LANE
    aparts = DIL_GW // LANE
    head = _iota_div((blk, DIL_GW), 1, DIL_HD)

    load = pltpu.make_async_copy(qkv_hbm.at[:, pl.ds(pl.multiple_of(pl.program_id(0) * seq, seq), seq), :],
                                 qkv_ref, sem)
    load.start()
    load.wait()

    def rows(start, n, d):
        return pl.ds(pl.multiple_of(start, blk), n) if d == 1 else pl.ds(start, n, stride=d)

    def unpack(which, sel):
        return _unpack_halves(jnp.concatenate([qkv_ref[which * wparts + j, sel, :] for j in range(wparts)], axis=1))

    def by_head(x, keep):
        return jnp.concatenate([jnp.where(keep(hh), x, jnp.zeros_like(x)) for hh in range(DIL_GROUP)], axis=0)

    def stacked_q(q2, g):
        return by_head(q2[g] * (DIL_HD ** -0.5), lambda hh: head == hh)

    def find_max(p, d, start, has_prev):
        qsel = rows(start, blk, d)
        ksel = rows(start - d * blk, 2 * blk, d) if has_prev else qsel
        q2, k2 = unpack(0, qsel), unpack(1, ksel)
        per_head = []
        for g in range(ngroups):
            bias = biast_ref[p, g] if has_prev else biast_ref[p, g, blk:2 * blk, :]
            st = lax.dot_general(k2[g], stacked_q(q2, g), NT, preferred_element_type=F32) + bias
            mg = jnp.max(st, axis=0, keepdims=True)
            per_head += [mg[:, hh * blk:(hh + 1) * blk] for hh in range(DIL_GROUP)]
        rest = jnp.full((blk - DIL_HEADS, blk), -jnp.inf, F32)
        mt = jnp.concatenate(per_head + [rest], axis=0).T
        m_ref[qsel, :] = mt if p == 0 else jnp.maximum(m_ref[qsel, :], mt)

    def accumulate(p, d, start, has_prev):
        qsel = rows(start, blk, d)
        ksel = rows(start - d * blk, 2 * blk, d) if has_prev else qsel
        q2, k2, v2 = unpack(0, qsel), unpack(1, ksel), unpack(2, ksel)
        m_rows = m_ref[qsel, :].T
        per_head = []
        for g in range(ngroups):
            bias = biast_ref[p, g] if has_prev else biast_ref[p, g, blk:2 * blk, :]
            st = lax.dot_general(k2[g], stacked_q(q2, g), NT, preferred_element_type=F32) + bias
            m_g = jnp.concatenate([m_rows[g * DIL_GROUP + hh:g * DIL_GROUP + hh + 1, :] for hh in range(DIL_GROUP)],
                                  axis=1)
            pt = jnp.exp(st - m_g)
            l_g = jnp.sum(pt, axis=0, keepdims=True)
            per_head += [l_g[:, hh * blk:(hh + 1) * blk] for hh in range(DIL_GROUP)]
            out = lax.dot_general(pt.astype(BF16), v2[g], TN, preferred_element_type=F32)
            new = jnp.zeros((blk, DIL_GW), F32)
            for hh in range(DIL_GROUP):
                new = jnp.where(head == hh, out[hh * blk:(hh + 1) * blk], new)
            for j in range(aparts):
                part = new[:, j * LANE:(j + 1) * LANE]
                acc_ref[g * aparts + j, qsel, :] = part if p == 0 else acc_ref[g * aparts + j, qsel, :] + part
        rest = jnp.zeros((blk - DIL_HEADS, blk), F32)
        l_new = jnp.concatenate(per_head + [rest], axis=0).T
        l_ref[qsel, :] = l_new if p == 0 else l_ref[qsel, :] + l_new

    def sweep(block):
        for p, (_, d) in enumerate(DIL_PATTERNS):
            nblk = seq // (d * blk)

            def first_block(r, carry, p=p, d=d):
                block(p, d, r, False)
                return carry

            def later_blocks(r, carry, p=p, d=d, nblk=nblk):
                def one(ib, carry):
                    block(p, d, ib * (d * blk) + r, True)
                    return carry
                return lax.fori_loop(1, nblk, one, carry, unroll=2)

            lax.fori_loop(0, d, first_block, 0, unroll=2 if d > 1 else 1)
            if nblk > 1:
                lax.fori_loop(0, d, later_blocks, 0, unroll=2 if nblk == 2 else 1)

    sweep(find_max)
    sweep(accumulate)

    half = _iota_div((blk, LANE), 1, DIL_HD)

    def normalise(i, carry):
        sel = pl.ds(pl.multiple_of(i * blk, blk), blk)
        inv = 1.0 / l_ref[sel, :]
        for j in range(DIL_W // LANE):
            scale = jnp.where(half == 0, inv[:, 2 * j:2 * j + 1], inv[:, 2 * j + 1:2 * j + 2])
            o_ref[sel, j * LANE:(j + 1) * LANE] = (acc_ref[j, sel, :] * scale).astype(o_ref.dtype)
        return carry

    lax.fori_loop(0, seq // blk, normalise, 0)


def _dil_prompt(qkv, batch, seq):
    for w, d in DIL_PATTERNS:
        assert w // d == DIL_KEYS and seq % (d * DIL_KEYS) == 0
    biast = _dil_bias()
    return pl.pallas_call(
        _dil_prompt_kernel,
        grid=(batch,),
        in_specs=[pl.BlockSpec(memory_space=pl.ANY),
                  pl.BlockSpec(biast.shape, lambda b: (0, 0, 0, 0))],
        out_specs=pl.BlockSpec((seq, DIL_W), lambda b: (b, 0)),
        out_shape=jax.ShapeDtypeStruct((batch * seq, DIL_W), BF16),
        scratch_shapes=[pltpu.VMEM((qkv.shape[0], seq, LANE), I32),
                        pltpu.VMEM((DIL_W // LANE, seq, LANE), F32),
                        pltpu.VMEM((seq, LANE), F32), pltpu.VMEM((seq, LANE), F32),
                        pltpu.SemaphoreType.DMA(())],
        compiler_params=_params("arbitrary"),
        name="dil_prompt",
    )(qkv, biast)


def _dil_sample_bias(positions, past, seq):
    positions = np.asarray(positions)[None, :]
    dist = past + np.arange(seq)[:, None] - positions
    mult = np.zeros(dist.shape, np.float64)
    for w, d in DIL_PATTERNS:
        mult += (positions >= 0) & (dist >= 0) & (dist <= w) & (dist % d == 0)
    with np.errstate(divide="ignore"):
        logm = np.log(mult)
    return jnp.asarray(np.concatenate([-sl * dist + logm for sl in _alibi_slopes()], axis=0).astype(np.float32))


def _dil_sample_kernel(q_ref, kn_ref, vn_ref, *refs, seq):
    nseg = (len(refs) - 2) // 3
    k_refs, v_refs, b_refs, o_ref = refs[:nseg], refs[nseg:2 * nseg], refs[2 * nseg:3 * nseg + 1], refs[-1]
    rows = DIL_HEADS * seq
    q = q_ref[...] * (DIL_HD ** -0.5)
    qrep = jnp.concatenate([q] * DIL_HEADS, axis=0)
    own = _iota_div((rows, DIL_W), 0, seq) == _iota_div((rows, DIL_W), 1, DIL_HD)
    qbd = jnp.where(own, qrep, 0.0).astype(BF16)
    pad = jnp.zeros((LANE - seq, DIL_W), F32)
    ks = [r[...].astype(BF16) for r in k_refs] + [jnp.concatenate([kn_ref[...], pad], axis=0).astype(BF16)]
    vs = [r[...].astype(BF16) for r in v_refs] + [jnp.concatenate([vn_ref[...], pad], axis=0).astype(BF16)]
    ss = [lax.dot_general(qbd, k, NT, preferred_element_type=F32) + b[...] for k, b in zip(ks, b_refs)]
    m = functools.reduce(jnp.maximum, [jnp.max(s, axis=-1, keepdims=True) for s in ss])
    ps = [jnp.exp(s - m) for s in ss]
    l = sum(jnp.sum(p, axis=-1, keepdims=True) for p in ps)
    o = sum(jnp.dot(p.astype(BF16), v, preferred_element_type=F32) for p, v in zip(ps, vs)) * (1.0 / l)
    o = jnp.where(own, o, 0.0)
    res = o[0:seq]
    for h in range(1, DIL_HEADS):
        res = res + o[h * seq:(h + 1) * seq]
    o_ref[...] = res


def _dil_sample(pb, k_new, v_new, cache_k, cache_v, batch, seq):
    past = cache_k.shape[1]
    new_pos = np.where(np.arange(LANE) < seq, past + np.arange(LANE), -1)
    segs = [(cache_k, cache_v, np.arange(past))]
    biases = [_dil_sample_bias(pos, past, seq) for _, _, pos in segs] + [_dil_sample_bias(new_pos, past, seq)]
    row = pl.BlockSpec((seq, DIL_W), lambda b: (b, 0))
    seg_spec = lambda a: pl.BlockSpec((None,) + a.shape[1:], lambda b: (b, 0, 0))
    const = lambda a: pl.BlockSpec(a.shape, lambda b: (0, 0))
    return pl.pallas_call(
        functools.partial(_dil_sample_kernel, seq=seq),
        grid=(batch,),
        in_specs=[row, row, row] + [seg_spec(s[0]) for s in segs] + [seg_spec(s[1]) for s in segs]
                 + [const(b) for b in biases],
        out_specs=row,
        out_shape=jax.ShapeDtypeStruct((batch * seq, DIL_W), F32),
        compiler_params=_params("parallel"),
        name="dil_sample",
    )(pb, k_new, v_new, *[s[0] for s in segs], *[s[1] for s in segs], *biases)


def _out_proj_kernel(oa_ref, ob_ref, x_ref, wo_ref, g_ref, wq_ref, h_ref, q_ref, *, qscale):
    o = jnp.concatenate([oa_ref[...].astype(BF16), ob_ref[...].astype(BF16)], axis=1)
    h = x_ref[...] + jnp.dot(o, wo_ref[...], preferred_element_type=F32)
    h_ref[...] = h
    n = _rms(h, g_ref[...]).astype(BF16)
    q_ref[...] = (jnp.dot(n, wq_ref[...], preferred_element_type=F32) * qscale).astype(q_ref.dtype)


def _out_proj(oa, ob, x, wo, g, wq, q_dtype, tm):
    n, d = x.shape
    row = lambda i: (i, 0)
    const = lambda i: (0, 0)
    return pl.pallas_call(
        functools.partial(_out_proj_kernel, qscale=(d // MEM_HEADS) ** -0.5),
        grid=(n // tm,),
        in_specs=[pl.BlockSpec((tm, GLA_V), row), pl.BlockSpec((tm, DIL_W), row), pl.BlockSpec((tm, d), row),
                  pl.BlockSpec(wo.shape, const), pl.BlockSpec((1, d), const), pl.BlockSpec(wq.shape, const)],
        out_specs=[pl.BlockSpec((tm, d), row), pl.BlockSpec((tm, d), row)],
        out_shape=[jax.ShapeDtypeStruct((n, d), F32), jax.ShapeDtypeStruct((n, d), q_dtype)],
        compiler_params=_params("parallel"),
        name="out_proj",
    )(oa, ob, x, wo, g, wq)


def _cross_kernel(q_ref, mk_ref, mv_ref, o_ref):
    hd = q_ref.shape[1] // MEM_HEADS
    for h in range(MEM_HEADS):
        sl = slice(h * hd, (h + 1) * hd)
        s = lax.dot_general(q_ref[:, sl].astype(BF16), mk_ref[:, sl].astype(BF16), NT, preferred_element_type=F32)
        p = jnp.exp(s - jnp.max(s, axis=-1, keepdims=True))
        l = jnp.sum(p, axis=-1, keepdims=True)
        o = jnp.dot(p.astype(BF16), mv_ref[:, sl].astype(BF16), preferred_element_type=F32) * (1.0 / l)
        o_ref[:, sl] = o.astype(o_ref.dtype)


def _cross(q, mk, mv, batch, seq, tq, out_dtype):
    n, d = q.shape
    nt = seq // tq
    nm = mk.shape[1]
    return pl.pallas_call(
        _cross_kernel,
        grid=(batch, nt),
        in_specs=[pl.BlockSpec((tq, d), lambda b, i: (b * nt + i, 0)),
                  pl.BlockSpec((None, nm, d), lambda b, i: (b, 0, 0)),
                  pl.BlockSpec((None, nm, d), lambda b, i: (b, 0, 0))],
        out_specs=pl.BlockSpec((tq, d), lambda b, i: (b * nt + i, 0)),
        out_shape=jax.ShapeDtypeStruct((n, d), out_dtype),
        compiler_params=_params("parallel", "parallel"),
        name="cross_attn",
    )(q, mk, mv)


def _co_proj_kernel(o_ref, h_ref, wo_ref, g_ref, wrh_ref, wrl_ref, br_ref, h2_ref, n3_ref, lg_ref):
    h2 = h_ref[...] + jnp.dot(o_ref[...].astype(BF16), wo_ref[...], preferred_element_type=F32)
    h2_ref[...] = h2
    n3 = _rms(h2, g_ref[...])
    hi = n3.astype(BF16)
    lo = (n3 - hi.astype(F32)).astype(BF16)
    n3_ref[...] = hi
    lg_ref[...] = (jnp.dot(hi, wrh_ref[...], preferred_element_type=F32)
                   + jnp.dot(lo, wrh_ref[...], preferred_element_type=F32)
                   + jnp.dot(hi, wrl_ref[...], preferred_element_type=F32) + br_ref[...])


def _co_proj(o, h, wo, g, wrh, wrl, br, tm):
    n, d = h.shape
    row = lambda i: (i, 0)
    const = lambda i: (0, 0)
    return pl.pallas_call(
        _co_proj_kernel,
        grid=(n // tm,),
        in_specs=[pl.BlockSpec((tm, d), row), pl.BlockSpec((tm, d), row), pl.BlockSpec(wo.shape, const),
                  pl.BlockSpec((1, d), const), pl.BlockSpec(wrh.shape, const), pl.BlockSpec(wrl.shape, const),
                  pl.BlockSpec((1, LANE), const)],
        out_specs=[pl.BlockSpec((tm, d), row), pl.BlockSpec((tm, d), row), pl.BlockSpec((tm, LANE), row)],
        out_shape=[jax.ShapeDtypeStruct((n, d), F32), jax.ShapeDtypeStruct((n, d), BF16),
                   jax.ShapeDtypeStruct((n, LANE), F32)],
        compiler_params=_params("parallel"),
        name="co_proj",
    )(o, h, wo, g, wrh, wrl, br)


META_E, META_POS, META_W = 0, TOP_K, 2 * TOP_K


def _router_kernel(lg_ref, tri_ref, meta_ref, tile_ref, cnt_ref, run_ref, *, chunk):
    @pl.when(pl.program_id(0) == 0)
    def _():
        run_ref[...] = jnp.zeros_like(run_ref)

    lg = lg_ref[...]
    tm = lg.shape[0]
    lane = lax.broadcasted_iota(I32, (tm, LANE), 1)
    lane_f = lane.astype(F32)
    ninf = -jnp.inf
    first = lambda hit: jnp.min(jnp.where(hit, lane_f, float(LANE)), axis=-1, keepdims=True).astype(I32)

    gl = jnp.where(lane < N_GROUPS, lg, ninf)
    gmax = jnp.max(gl, axis=-1, keepdims=True)
    gidx = first(gl == gmax)
    pg = 1.0 / jnp.sum(jnp.exp(gl - gmax), axis=-1, keepdims=True)

    ex = lane - N_GROUPS
    el = jnp.where((ex >= 0) & (ex < N_EXPERTS) & ((ex >> int(math.log2(EXP_PER_GROUP))) == gidx), lg, ninf)
    t1 = jnp.max(el, axis=-1, keepdims=True)
    i1 = first(el == t1)
    el2 = jnp.where(lane == i1, ninf, el)
    t2 = jnp.max(el2, axis=-1, keepdims=True)
    i2 = first(el2 == t2)
    e = jnp.exp(t2 - t1)
    gates = (pg / (1.0 + e), pg * e / (1.0 + e))
    experts = (i1 - N_GROUPS, i2 - N_GROUPS)

    hits = [lane == ex_j for ex_j in experts]
    onehot = jnp.where(hits[0] | hits[1], 1.0, 0.0)
    before = jnp.dot(tri_ref[...], onehot.astype(BF16), preferred_element_type=F32)
    count = jnp.sum(onehot, axis=0, keepdims=True)
    slots = jnp.floor((count + (chunk - 1)) * (1.0 / chunk)) * chunk
    pos = [jnp.sum(jnp.where(lane < ex_j, slots, 0.0) + jnp.where(hit, before, 0.0), axis=-1, keepdims=True)
           for ex_j, hit in zip(experts, hits)]

    meta = jnp.zeros((tm, LANE), F32)
    for base, vals in ((META_E, [x.astype(F32) for x in experts]), (META_POS, pos), (META_W, gates)):
        for j, val in enumerate(vals):
            meta = jnp.where(lane == base + j, val, meta)
    meta_ref[...] = meta

    sub = lax.broadcasted_iota(I32, (SUBLANE, LANE), 0)
    tile_ref[...] = jnp.where(sub == 0, run_ref[...], jnp.where(sub == 1, slots, 0.0))
    run_ref[...] = run_ref[...] + slots
    cnt_ref[...] = run_ref[...]


def _router(logits, tm, chunk):
    n = logits.shape[0]
    tri = jnp.asarray(np.tril(np.ones((tm, tm), np.float32), -1), BF16)
    return pl.pallas_call(
        functools.partial(_router_kernel, chunk=chunk),
        grid=(n // tm,),
        in_specs=[pl.BlockSpec((tm, LANE), lambda i: (i, 0)), pl.BlockSpec((tm, tm), lambda i: (0, 0))],
        out_specs=[pl.BlockSpec((tm, LANE), lambda i: (i, 0)),
                   pl.BlockSpec((SUBLANE, LANE), lambda i: (i, 0)),
                   pl.BlockSpec((SUBLANE, LANE), lambda i: (0, 0))],
        out_shape=[jax.ShapeDtypeStruct((n, LANE), F32),
                   jax.ShapeDtypeStruct((n // tm * SUBLANE, LANE), F32),
                   jax.ShapeDtypeStruct((SUBLANE, LANE), F32)],
        scratch_shapes=[pltpu.VMEM((SUBLANE, LANE), F32)],
        compiler_params=_params("arbitrary"),
        name="router",
    )(logits, tri)


TAB_TOTAL = LANE - 1
FILL_BASE, FILL_N, FILL_TAIL = 0, N_EXPERTS, 2 * N_EXPERTS
FILL_ROWS = 64


def _one_hot(meta, lane_pos, slot):
    return jnp.where(lane_pos == meta[:, META_POS + slot:META_POS + slot + 1], 1.0, 0.0).astype(BF16)


def _chunk_copy(sorted_ref, hbm_ref, sem, chunk, to_hbm, off, base):
    src, dst = sorted_ref.at[pl.ds(off, chunk), :], hbm_ref.at[pl.ds(base, chunk), :]
    return pltpu.make_async_copy(src, dst, sem) if to_hbm else pltpu.make_async_copy(dst, src, sem)


def _start_chunks(tab_ref, sorted_ref, hbm_ref, sem, chunk, to_hbm):
    def one(c, carry):
        _chunk_copy(sorted_ref, hbm_ref, sem, chunk, to_hbm, pl.multiple_of(c * chunk, chunk),
                    pl.multiple_of(tab_ref[0, c], chunk)).start()
        return carry

    lax.fori_loop(0, tab_ref[0, TAB_TOTAL], one, 0)


def _wait_chunks(tab_ref, sorted_ref, hbm_ref, sem, chunk, to_hbm):
    def drain(c, carry):
        _chunk_copy(sorted_ref, hbm_ref, sem, chunk, to_hbm, 0, 0).wait()
        return carry

    lax.fori_loop(0, tab_ref[0, TAB_TOTAL], drain, 0)


def _dispatch_kernel(tab_ref, prev_ref, fill_ref, x_ref, meta_ref, xs_ref, sorted_ref, zero_ref, sems, *, chunk):
    i = pl.program_id(0)
    slot = lax.rem(i, 2)
    tm, npos = x_ref.shape[0], sorted_ref.shape[1]
    meta = meta_ref[...]
    lane_pos = lax.broadcasted_iota(I32, (tm, npos), 1).astype(F32)
    place = _one_hot(meta, lane_pos, 0) + _one_hot(meta, lane_pos, 1)
    sorted_ref[slot] = lax.dot_general(place, x_ref[...], TN, preferred_element_type=F32)

    @pl.when(i > 0)
    def _():
        _wait_chunks(prev_ref, sorted_ref.at[1 - slot], xs_ref, sems.at[1 - slot], chunk, True)

    _start_chunks(tab_ref, sorted_ref.at[slot], xs_ref, sems.at[slot], chunk, True)

    @pl.when(i == pl.num_programs(0) - 1)
    def _():
        _wait_chunks(tab_ref, sorted_ref.at[slot], xs_ref, sems.at[slot], chunk, True)
        zero_ref[...] = jnp.zeros_like(zero_ref)
        big = zero_ref.shape[0]
        sem = sems.at[0]

        def fill(row, size):
            return pltpu.make_async_copy(zero_ref.at[pl.ds(0, size), :],
                                         xs_ref.at[pl.ds(pl.multiple_of(row, size), size), :], sem)

        def per_expert(e, total):
            def one(c, carry):
                fill(fill_ref[0, FILL_BASE + e] + c * chunk, chunk).start()
                return carry
            lax.fori_loop(0, fill_ref[0, FILL_N + e], one, 0)
            return total + fill_ref[0, FILL_N + e]

        def drain(c, carry):
            fill(0, chunk).wait()
            return carry

        lax.fori_loop(0, lax.fori_loop(0, N_EXPERTS, per_expert, 0), drain, 0)

        def tail(c, carry):
            fill(fill_ref[0, FILL_TAIL] + c * big, big).start()
            return carry

        def drain_tail(c, carry):
            fill(0, big).wait()
            return carry

        lax.fori_loop(0, fill_ref[0, FILL_TAIL + 1], tail, 0)
        lax.fori_loop(0, fill_ref[0, FILL_TAIL + 1], drain_tail, 0)


def _dispatch(x, meta, tab, fill, rows, tm, chunk):
    n, d = x.shape
    npos = TOP_K * tm + N_EXPERTS * chunk
    return pl.pallas_call(
        functools.partial(_dispatch_kernel, chunk=chunk),
        grid=(n // tm,),
        in_specs=[pl.BlockSpec((None, 1, LANE), lambda i: (i, 0, 0), memory_space=pltpu.SMEM),
                  pl.BlockSpec((None, 1, LANE), lambda i: (jnp.maximum(i - 1, 0), 0, 0), memory_space=pltpu.SMEM),
                  pl.BlockSpec((1, LANE), lambda i: (0, 0), memory_space=pltpu.SMEM),
                  pl.BlockSpec((tm, d), lambda i: (i, 0)),
                  pl.BlockSpec((tm, LANE), lambda i: (i, 0))],
        out_specs=pl.BlockSpec(memory_space=pl.ANY),
        out_shape=jax.ShapeDtypeStruct((rows, d), F32),
        scratch_shapes=[pltpu.VMEM((2, npos, d), F32), pltpu.VMEM((FILL_ROWS, d), F32),
                        pltpu.SemaphoreType.DMA((2,))],
        compiler_params=_params("arbitrary"),
        name="moe_dispatch",
    )(tab, tab, fill, x, meta)


def _expert_kernel(be_ref, nu_ref, x_ref, w1_ref, w3_ref, w2_ref, y_ref):
    del be_ref
    live = pl.program_id(0) < nu_ref[0]

    @pl.when(live)
    def _():
        x = x_ref[...].astype(BF16)
        a = jnp.dot(x, w1_ref[...], preferred_element_type=F32)
        b = jnp.dot(x, w3_ref[...], preferred_element_type=F32)
        y_ref[...] = jnp.dot((_silu(a) * b).astype(BF16), w2_ref[...], preferred_element_type=F32)

    @pl.when(jnp.logical_not(live))
    def _():
        y_ref[...] = jnp.zeros_like(y_ref)


def _experts(xs, block_e, n_used, w1, w3, w2, bm):
    rows, d = xs.shape
    de = w1.shape[2]
    return pl.pallas_call(
        _expert_kernel,
        grid_spec=pltpu.PrefetchScalarGridSpec(
            num_scalar_prefetch=2,
            grid=(rows // bm,),
            in_specs=[pl.BlockSpec((bm, d), lambda i, be, nu: (i, 0)),
                      pl.BlockSpec((None, d, de), lambda i, be, nu: (be[i], 0, 0)),
                      pl.BlockSpec((None, d, de), lambda i, be, nu: (be[i], 0, 0)),
                      pl.BlockSpec((None, de, d), lambda i, be, nu: (be[i], 0, 0))],
            out_specs=pl.BlockSpec((bm, d), lambda i, be, nu: (i, 0))),
        out_shape=jax.ShapeDtypeStruct((rows, d), F32),
        compiler_params=_params("arbitrary"),
        name="moe_experts",
    )(block_e, n_used, xs, w1, w3, w2)


def _combine_kernel(tab_ref, next_ref, h_ref, meta_ref, g_ref, yb_ref, y_ref, sorted_ref, sems, *, chunk):
    i = pl.program_id(0)
    slot = lax.rem(i, 2)

    @pl.when(i == 0)
    def _():
        sorted_ref[...] = jnp.zeros_like(sorted_ref)
        _start_chunks(tab_ref, sorted_ref.at[0], yb_ref, sems.at[0], chunk, False)

    @pl.when(i + 1 < pl.num_programs(0))
    def _():
        _start_chunks(next_ref, sorted_ref.at[1 - slot], yb_ref, sems.at[1 - slot], chunk, False)

    _wait_chunks(tab_ref, sorted_ref.at[slot], yb_ref, sems.at[slot], chunk, False)
    tm, npos = h_ref.shape[0], sorted_ref.shape[1]
    meta = meta_ref[...]
    lane_pos = lax.broadcasted_iota(I32, (tm, npos), 1).astype(F32)
    yb = sorted_ref[slot].astype(BF16)
    moe = (meta[:, META_W:META_W + 1] * jnp.dot(_one_hot(meta, lane_pos, 0), yb, preferred_element_type=F32)
           + meta[:, META_W + 1:META_W + 2] * jnp.dot(_one_hot(meta, lane_pos, 1), yb, preferred_element_type=F32))
    y_ref[...] = _rms(h_ref[...] + moe, g_ref[...])


def _combine(h, meta, tab, yb, g, tm, chunk):
    n, d = h.shape
    npos = TOP_K * tm + N_EXPERTS * chunk
    return pl.pallas_call(
        functools.partial(_combine_kernel, chunk=chunk),
        grid=(n // tm,),
        in_specs=[pl.BlockSpec((None, 1, LANE), lambda i: (i, 0, 0), memory_space=pltpu.SMEM),
                  pl.BlockSpec((None, 1, LANE), lambda i: (jnp.minimum(i + 1, n // tm - 1), 0, 0),
                               memory_space=pltpu.SMEM),
                  pl.BlockSpec((tm, d), lambda i: (i, 0)),
                  pl.BlockSpec((tm, LANE), lambda i: (i, 0)),
                  pl.BlockSpec((1, d), lambda i: (0, 0)),
                  pl.BlockSpec(memory_space=pl.ANY)],
        out_specs=pl.BlockSpec((tm, d), lambda i: (i, 0)),
        out_shape=jax.ShapeDtypeStruct((n, d), F32),
        scratch_shapes=[pltpu.VMEM((2, npos, d), F32), pltpu.SemaphoreType.DMA((2,))],
        compiler_params=_params("arbitrary"),
        name="moe_combine",
    )(tab, tab, h, meta, g, yb)


def _moe_final(h2, n3, logits, w1, w3, w2, g_final, tm, bm, chunk):
    n = h2.shape[0]
    nt = n // tm
    max_chunks = (TOP_K * tm + N_EXPERTS * chunk) // chunk
    assert max_chunks <= TAB_TOTAL and bm % chunk == 0
    meta, tiles, cnt = _router(logits, tm, chunk)
    rows_e = cnt[0, :N_EXPERTS].astype(I32)
    padded = (rows_e + bm - 1) // bm * bm
    pad_end = jnp.cumsum(padded)
    pad_start = pad_end - padded
    tiles = tiles.reshape(nt, SUBLANE, LANE)
    run_before = tiles[:, 0, :N_EXPERTS].astype(I32)
    nchunks = tiles[:, 1, :N_EXPERTS].astype(I32) // chunk
    last_chunk = jnp.cumsum(nchunks, axis=1)
    c = jnp.arange(max_chunks, dtype=I32)[None, :]
    e_of_c = jnp.minimum(jnp.sum((last_chunk[:, None, :] <= c[:, :, None]).astype(I32), axis=2), N_EXPERTS - 1)
    pick = lambda a: jnp.take_along_axis(a, e_of_c, axis=1)
    chunk_row = pick(pad_start[None, :] + run_before) + (c - pick(last_chunk - nchunks)) * chunk
    pad_lanes = lambda a: jnp.pad(a, ((0, 0), (0, LANE - a.shape[1])))
    tab = jnp.concatenate([pad_lanes(chunk_row)[:, :TAB_TOTAL], last_chunk[:, -1:]], axis=1).reshape(nt, 1, LANE)
    max_rows = n * TOP_K + (chunk - 1) * min(n * TOP_K, nt * N_EXPERTS) + N_EXPERTS * (bm - 1)
    nb = -(-max_rows // bm)
    assert bm % FILL_ROWS == 0
    fill = pad_lanes(jnp.concatenate([pad_start + rows_e, (padded - rows_e) // chunk,
                                      pad_end[-1:], (nb * bm - pad_end[-1:]) // FILL_ROWS])[None, :])
    block_start = jnp.arange(nb, dtype=I32) * bm
    block_e = jnp.minimum(jnp.sum((pad_end[None, :] <= block_start[:, None]).astype(I32), axis=1), N_EXPERTS - 1)
    n_used = (pad_end[-1:] // bm).astype(I32)
    xs = _dispatch(n3, meta, tab, fill, nb * bm, tm, chunk)
    yb = _experts(xs, block_e, n_used, w1, w3, w2, bm)
    return _combine(h2, meta, tab, yb, g_final, tm, chunk)


def _group(x3, mk, mv, w, keep, *, cache=None, state=None):
    batch, seq, d = x3.shape
    n = batch * seq
    x = x3.reshape(n, d)
    tm = _tile(n, TOKEN_TILE)
    sample = cache is not None
    pa, pb, k_new, v_new = _norm_proj(x, w["g1"], w["w_in"], batch, seq, keep, not sample, tm)
    if sample:
        assert keep == seq
        oa, st = _gla_sample(pa, _state_to_blockdiag_t(state), w["wgk"], w["bgk"], w["gg"], batch, seq,
                             rows=GLA_CHUNK)
        ob = _dil_sample(pb, k_new, v_new, cache[0].reshape(batch, -1, DIL_W), cache[1].reshape(batch, -1, DIL_W),
                         batch, seq)
    else:
        zeros = jnp.zeros((batch, GLA_V, GLA_QK), F32)
        oa, st = _gla_prompt(pa, zeros, w["wgk"], w["bgk"], w["gg"], batch, seq, tb=_tile(seq, TOKEN_TILE))
        ob = _dil_prompt(pb, batch, seq)
    h1, qc = _out_proj(oa, ob, x, w["w_out"], w["g2"], w["w_cq"], F32 if sample else BF16, tm)
    oc = _cross(qc, mk, mv, batch, seq, _tile(seq, TOKEN_TILE), F32 if sample else BF16)
    h2, n3, logits = _co_proj(oc, h1, w["w_co"], w["g3"], w["wr_hi"], w["wr_lo"], w["br"], tm)
    y = _moe_final(h2, n3, logits, w["w_e1"], w["w_e3"], w["w_e2"], w["g_final"], _tile(n, MOE_TILE), MOE_BLOCK,
                   MOE_CHUNK)
    return (y.reshape(batch, seq, d), _blockdiag_t_to_state(st),
            k_new.reshape(batch, keep, DIL_HEADS, DIL_HD), v_new.reshape(batch, keep, DIL_HEADS, DIL_HD))


def _pack_weights(l, g_norm1, w_in, w_gk2, b_gk, g_gla_out, w_out, g_norm2, w_cq, w_co, g_norm3, w_gr, b_gr,
                  w_er, b_er, w_e1, w_e3, w_e2, g_final):
    d = w_in.shape[1]
    sp = np.cumsum([GLA_QK, GLA_QK, GLA_V, GATE_RANK, GLA_V, DIL_W, DIL_W, DIL_W])
    wi = w_in[l]
    q_a, k_a, v_a, glr, r_a, q_b, k_b, v_b = (wi[:, a:b] for a, b in zip([0, *sp[:-1]], sp))
    glr = jnp.pad(glr, ((0, 0), (0, LANE - GATE_RANK)))
    wr = jnp.pad(jnp.concatenate([w_gr[l], w_er[l]], axis=1), ((0, 0), (0, LANE - N_GROUPS - N_EXPERTS)))
    wr_hi = wr.astype(BF16)
    return dict(
        g1=g_norm1[l].reshape(1, d), g2=g_norm2[l].reshape(1, d), g3=g_norm3[l].reshape(1, d),
        g_final=g_final.reshape(1, d),
        w_in=jnp.concatenate([q_a, k_a, v_a, r_a, glr, q_b, k_b, v_b], axis=1).astype(BF16),
        wgk=jnp.pad(w_gk2[l], ((0, LANE - GATE_RANK), (0, 0))).astype(BF16),
        bgk=b_gk[l].reshape(1, GLA_QK), gg=g_gla_out[l].reshape(1, GLA_DV),
        w_out=w_out[l].astype(BF16), w_cq=w_cq[l].astype(BF16), w_co=w_co[l].astype(BF16),
        wr_hi=wr_hi, wr_lo=(wr - wr_hi.astype(F32)).astype(BF16),
        br=jnp.pad(jnp.concatenate([b_gr[l], b_er[l]]), (0, LANE - N_GROUPS - N_EXPERTS)).reshape(1, LANE),
        w_e1=w_e1[l].astype(BF16), w_e3=w_e3[l].astype(BF16), w_e2=w_e2[l].astype(BF16))


def kernel(x_prompt, x_sample, cache_swa_k, cache_swa_v, state_gla, cache_mem_k, cache_mem_v, mem_prompt,
           g_norm1, w_in, w_gk2, b_gk, g_gla_out, w_out, g_norm2, g_mem, w_cq, w_mk, w_mv, w_co,
           g_norm3, w_gr, b_gr, w_er, b_er, w_e1, w_e3, w_e2, g_final):
    depth = w_in.shape[0]
    assert depth == 1, "the final norm is fused into the last MoE stage; stacked layers are not supported"
    batch, seq, d = x_prompt.shape
    sb, sseq, _ = x_sample.shape
    nm = mem_prompt.shape[1]
    keep = min(DIL_PATTERNS[-1][0], seq)
    l = 0
    w = _pack_weights(l, g_norm1, w_in, w_gk2, b_gk, g_gla_out, w_out, g_norm2, w_cq, w_co, g_norm3,
                      w_gr, b_gr, w_er, b_er, w_e1, w_e3, w_e2, g_final)
    mkv = _norm_matmul(mem_prompt.reshape(batch * nm, d), g_mem[l].reshape(1, d),
                       jnp.concatenate([w_mk[l], w_mv[l]], axis=1).astype(BF16), _tile(batch * nm, TOKEN_TILE))
    mk = mkv[:, :d].reshape(batch, nm, d)
    mv = mkv[:, d:].reshape(batch, nm, d)
    yp, sp, kp, vp = _group(x_prompt, mk, mv, w, keep)
    past = cache_swa_k.shape[2]
    ys, ss, kn, vn = _group(x_sample, cache_mem_k.reshape(sb, nm, d), cache_mem_v.reshape(sb, nm, d), w, sseq,
                            cache=(cache_swa_k.reshape(sb, past, DIL_W), cache_swa_v.reshape(sb, past, DIL_W)),
                            state=state_gla[l])
    heads = lambda m: m.reshape(batch, nm, MEM_HEADS, d // MEM_HEADS)
    return (yp, ys, *(o[None] for o in (kp, vp, sp, heads(mk), heads(mv), kn, vn, ss)))
```

```python
import functools
import math

import numpy as np
import jax
import jax.numpy as jnp
from jax import lax
from jax.experimental import pallas as pl
from jax.experimental.pallas import tpu as pltpu

F32, BF16, I32 = jnp.float32, jnp.bfloat16, jnp.int32

GLA_HEADS, GLA_DK, GLA_DV = 4, 64, 128
GLA_QK, GLA_V = GLA_HEADS * GLA_DK, GLA_HEADS * GLA_DV
GATE_RANK, GATE_NORM, GLA_CHUNK = 16, 16.0, 64
DIL_HEADS, DIL_HD = 8, 64
DIL_W = DIL_HEADS * DIL_HD
DIL_PATTERNS = ((128, 1), (512, 4), (2048, 16))
DIL_KEYS = 128
MEM_HEADS = 4
N_GROUPS, EXP_PER_GROUP, TOP_K = 4, 8, 2
N_EXPERTS = N_GROUPS * EXP_PER_GROUP
EPS = 1e-6

LANE = 128
SUBLANE = 8
VMEM_LIMIT_BYTES = 56 * 1024 * 1024

PA_W = 2 * GLA_QK + 2 * GLA_V + LANE
PB_W = 3 * DIL_W
HALF_W = DIL_W // 2
HI_MASK = -65536

TOKEN_TILE = 512
MOE_TILE = 256
MOE_BLOCK = 512
MOE_CHUNK = SUBLANE

NT = (((1,), (1,)), ((), ()))
TN = (((0,), (0,)), ((), ()))


def _params(*sem):
    return pltpu.CompilerParams(dimension_semantics=sem, vmem_limit_bytes=VMEM_LIMIT_BYTES)


def _tile(n, want):
    t = min(n, want)
    assert n % t == 0
    return t


def _rms(x, g):
    y = x * lax.rsqrt(jnp.mean(x * x, axis=-1, keepdims=True) + EPS)
    return y * g


def _silu(x):
    return x / (1.0 + jnp.exp(-x))


def _iota_div(shape, dim, n):
    assert n & (n - 1) == 0
    return lax.broadcasted_iota(I32, shape, dim) >> int(math.log2(n))


def _pack_halves(r):
    u = lax.bitcast_convert_type(r.astype(BF16).astype(F32), I32)
    lo = u[:, :HALF_W]
    return lax.shift_right_logical(lo, jnp.full_like(lo, 16)) | (u[:, HALF_W:] & HI_MASK)


def _unpack_halves(w):
    lo = lax.bitcast_convert_type(w << 16, F32).astype(BF16)
    hi = lax.bitcast_convert_type(w & HI_MASK, F32).astype(BF16)
    return lo, hi


def _norm_proj_kernel(x_ref, g_ref, w_ref, pa_ref, pb_ref, k_ref, v_ref, *, pack):
    n = _rms(x_ref[...], g_ref[...]).astype(BF16)
    for c0 in range(0, PA_W, DIL_W):
        c1 = min(c0 + DIL_W, PA_W)
        pa_ref[:, c0:c1] = jnp.dot(n, w_ref[:, c0:c1], preferred_element_type=F32).astype(pa_ref.dtype)
    for j, kv_ref in enumerate((None, k_ref, v_ref)):
        r = jnp.dot(n, w_ref[:, PA_W + j * DIL_W:PA_W + (j + 1) * DIL_W], preferred_element_type=F32)
        if pack:
            words = _pack_halves(r)
            for part in range(HALF_W // LANE):
                pb_ref[j * (HALF_W // LANE) + part] = words[:, part * LANE:(part + 1) * LANE]
        else:
            pb_ref[:, j * DIL_W:(j + 1) * DIL_W] = r
        if kv_ref is not None:
            kv_ref[...] = r


def _norm_proj(x, g, w, batch, seq, keep, pack, tm):
    n, d = x.shape
    if keep == seq:
        kv_map = lambda i: (i, 0)
    else:
        nt, nk = seq // tm, keep // tm
        assert keep % tm == 0
        kv_map = lambda i: ((i // nt) * nk + jnp.maximum(i % nt - (nt - nk), 0), 0)
    if pack:
        nparts = 3 * HALF_W // LANE
        pb_spec = pl.BlockSpec((nparts, tm, LANE), lambda i: (0, i, 0))
        pb_shape = jax.ShapeDtypeStruct((nparts, n, LANE), I32)
    else:
        pb_spec = pl.BlockSpec((tm, PB_W), lambda i: (i, 0))
        pb_shape = jax.ShapeDtypeStruct((n, PB_W), F32)
    pa_dtype = BF16 if pack else F32
    return pl.pallas_call(
        functools.partial(_norm_proj_kernel, pack=pack),
        grid=(n // tm,),
        in_specs=[pl.BlockSpec((tm, d), lambda i: (i, 0)),
                  pl.BlockSpec((1, d), lambda i: (0, 0)),
                  pl.BlockSpec((d, PA_W + PB_W), lambda i: (0, 0))],
        out_specs=[pl.BlockSpec((tm, PA_W), lambda i: (i, 0)),
                   pb_spec,
                   pl.BlockSpec((tm, DIL_W), kv_map),
                   pl.BlockSpec((tm, DIL_W), kv_map)],
        out_shape=[jax.ShapeDtypeStruct((n, PA_W), pa_dtype),
                   pb_shape,
                   jax.ShapeDtypeStruct((batch * keep, DIL_W), F32),
                   jax.ShapeDtypeStruct((batch * keep, DIL_W), F32)],
        compiler_params=_params("arbitrary"),
        name="norm_proj",
    )(x, g, w)


def _norm_matmul_kernel(x_ref, g_ref, w_ref, o_ref):
    n = _rms(x_ref[...], g_ref[...]).astype(BF16)
    for c0 in range(0, o_ref.shape[1], DIL_W):
        o_ref[:, c0:c0 + DIL_W] = jnp.dot(n, w_ref[:, c0:c0 + DIL_W], preferred_element_type=F32)


def _norm_matmul(x, g, w, tm):
    n, d = x.shape
    m = w.shape[1]
    return pl.pallas_call(
        _norm_matmul_kernel,
        grid=(n // tm,),
        in_specs=[pl.BlockSpec((tm, d), lambda i: (i, 0)),
                  pl.BlockSpec((1, d), lambda i: (0, 0)),
                  pl.BlockSpec((d, m), lambda i: (0, 0))],
        out_specs=pl.BlockSpec((tm, m), lambda i: (i, 0)),
        out_shape=jax.ShapeDtypeStruct((n, m), F32),
        compiler_params=_params("parallel"),
        name="norm_matmul",
    )(x, g, w)


def _gla_tables(chunk, seg):
    idx = np.arange(chunk)
    tril = ((idx[None, :] <= idx[:, None]) & (idx[None, :] // seg == idx[:, None] // seg)).astype(np.float32)
    masks, levels = [], []
    s = seg // 2
    while s >= 1:
        same = (idx[:, None] // (2 * s)) == (idx[None, :] // (2 * s))
        masks.append(same & ((idx[:, None] // s) % 2 == 1) & ((idx[None, :] // s) % 2 == 0))
        levels.append(s)
        s //= 2
    masks.append(idx[:, None] == idx[None, :])
    pm = np.tile(np.stack(masks).astype(np.float32), (1, 1, GLA_HEADS))
    return jnp.asarray(tril, BF16), jnp.asarray(pm), tuple(levels)


def _block_row(b, blk, idx):
    c, w = b.shape
    b3 = b.reshape(c // blk, blk, w)
    return jnp.broadcast_to(b3[:, idx:idx + 1, :], (c // blk, blk, w)).reshape(c, w)


def _level_ref(b, s, row):
    c = b.shape[0]
    if 2 * s >= SUBLANE:
        return _block_row(b, 2 * s, s - 1)
    down = lambda n: pltpu.roll(b, n, 0)
    if s == 2:
        m = row & 3
        return jnp.where(m == 0, pltpu.roll(b, c - 1, 0), jnp.where(m == 1, b, jnp.where(m == 2, down(1), down(2))))
    assert s == 1
    return jnp.where((row & 1) == 1, down(1), b)


def _gla_chunk(q, k, v, glr, wgk, bgk, tril_ref, pm_ref, levels, seg):
    c = q.shape[0]
    nl = len(levels)
    gk = jnp.dot(glr, wgk, preferred_element_type=F32) + bgk
    la = (jnp.minimum(gk, 0.0) - jnp.log(1.0 + jnp.exp(-jnp.abs(gk)))) * (1.0 / GATE_NORM)
    hi = la.astype(BF16)
    r1 = la - hi.astype(F32)
    mid = r1.astype(BF16)
    lo = (r1 - mid.astype(F32)).astype(BF16)
    b3 = jnp.dot(tril_ref[...], jnp.concatenate([hi, mid, lo], axis=1), preferred_element_type=F32)
    b = b3[:, :GLA_QK] + b3[:, GLA_QK:2 * GLA_QK] + b3[:, 2 * GLA_QK:]
    b_end = _block_row(b, seg, seg - 1)

    row = lax.broadcasted_iota(I32, (c, GLA_QK), 0)
    khead = _iota_div((c, GLA_QK), 1, GLA_DK)
    vhead = _iota_div((c, GLA_V), 1, GLA_DV)

    def by_head(x, head):
        return jnp.concatenate([jnp.where(head == h, x, jnp.zeros_like(x)) for h in range(GLA_HEADS)], axis=0)

    def level(qe, ke, pm):
        a = lax.dot_general(qe, by_head(ke, khead), NT, preferred_element_type=F32)
        return jnp.where(pm > 0.0, a, 0.0)

    acc = level(q.astype(BF16), k.astype(BF16), pm_ref[nl])
    for l, s in enumerate(levels):
        ref = _level_ref(b, s, row)
        right = ((row >> int(math.log2(s))) & 1) == 1
        qe = (q * jnp.exp(jnp.where(right, b - ref, 0.0))).astype(BF16)
        ke = (k * jnp.exp(jnp.where(right, 0.0, ref - b))).astype(BF16)
        acc = acc + level(qe, ke, pm_ref[l])

    o = jnp.dot(acc.astype(BF16), by_head(v, vhead), preferred_element_type=F32)
    qb = (q * jnp.exp(b)).astype(BF16)
    kd = (k * jnp.exp(b_end - b)).astype(BF16)
    return o, qb, kd, b_end


def _gla_finish(o, r, gg):
    outs = []
    for h in range(GLA_HEADS):
        sl = slice(h * GLA_DV, (h + 1) * GLA_DV)
        outs.append(_rms(o[:, sl], gg) * _silu(r[:, sl]))
    return jnp.concatenate(outs, axis=1)


def _state_mask():
    return _iota_div((GLA_V, GLA_QK), 0, GLA_DV) == _iota_div((GLA_V, GLA_QK), 1, GLA_DK)


def _gla_prompt_kernel(q_ref, k_ref, v_ref, r_ref, glr_ref, wgk_ref, bgk_ref, gg_ref, tril_ref, pm_ref, s0_ref,
                       o_ref, sout_ref, st_ref, *, chunk, levels):
    i = pl.program_id(1)

    @pl.when(i == 0)
    def _():
        st_ref[...] = s0_ref[...]

    smask = _state_mask()

    def body(c, carry):
        rows = pl.ds(pl.multiple_of(c * chunk, chunk), chunk)
        q = q_ref[rows, :].astype(F32) * (GLA_DK ** -0.5)
        k = k_ref[rows, :].astype(F32)
        v = v_ref[rows, :]
        o, qb, kd, b_end = _gla_chunk(q, k, v, glr_ref[rows, :], wgk_ref[...], bgk_ref[...],
                                      tril_ref, pm_ref, levels, chunk)
        st = st_ref[...]
        o = o + lax.dot_general(qb, st.astype(BF16), NT, preferred_element_type=F32)
        u = lax.dot_general(v, kd, TN, preferred_element_type=F32)
        st_ref[...] = st * jnp.exp(b_end[0:1, :]) + jnp.where(smask, u, 0.0)
        o_ref[rows, :] = _gla_finish(o, r_ref[rows, :].astype(F32), gg_ref[...]).astype(o_ref.dtype)
        return carry

    lax.fori_loop(0, q_ref.shape[0] // chunk, body, 0, unroll=4)

    @pl.when(i == pl.num_programs(1) - 1)
    def _():
        sout_ref[...] = st_ref[...]


def _gla_prompt(pa, s0t, wgk, bgk, gg, batch, seq, tb):
    chunk = math.gcd(seq, GLA_CHUNK)
    tril, pm, levels = _gla_tables(chunk, chunk)
    nt = seq // tb
    row = lambda b, i: (b * nt + i, 0)
    const2 = lambda b, i: (0, 0)
    return pl.pallas_call(
        functools.partial(_gla_prompt_kernel, chunk=chunk, levels=levels),
        grid=(batch, nt),
        in_specs=[pl.BlockSpec((tb, GLA_QK), row),
                  pl.BlockSpec((tb, GLA_QK), lambda b, i: (b * nt + i, 1)),
                  pl.BlockSpec((tb, GLA_V), lambda b, i: (b * nt + i, 1)),
                  pl.BlockSpec((tb, GLA_V), lambda b, i: (b * nt + i, 2)),
                  pl.BlockSpec((tb, LANE), lambda b, i: (b * nt + i, (PA_W - LANE) // LANE)),
                  pl.BlockSpec(wgk.shape, const2),
                  pl.BlockSpec(bgk.shape, const2),
                  pl.BlockSpec(gg.shape, const2),
                  pl.BlockSpec(tril.shape, const2),
                  pl.BlockSpec(pm.shape, lambda b, i: (0, 0, 0)),
                  pl.BlockSpec((None, GLA_V, GLA_QK), lambda b, i: (b, 0, 0))],
        out_specs=[pl.BlockSpec((tb, GLA_V), row),
                   pl.BlockSpec((None, GLA_V, GLA_QK), lambda b, i: (b, 0, 0))],
        out_shape=[jax.ShapeDtypeStruct((batch * seq, GLA_V), BF16),
                   jax.ShapeDtypeStruct((batch, GLA_V, GLA_QK), F32)],
        scratch_shapes=[pltpu.VMEM((GLA_V, GLA_QK), F32)],
        compiler_params=_params("parallel", "arbitrary"),
        name="gla_prompt",
    )(pa, pa, pa, pa, pa, wgk, bgk, gg, tril, pm, s0t)


def _gla_sample_kernel(q_ref, k_ref, v_ref, r_ref, glr_ref, wgk_ref, bgk_ref, gg_ref, tril_ref, pm_ref, s0_ref,
                       o_ref, sout_ref, *, seg, levels):
    rows = q_ref.shape[0]
    q = q_ref[...] * (GLA_DK ** -0.5)
    v = v_ref[...].astype(BF16)
    o, qb, kd, b_end = _gla_chunk(q, k_ref[...], v, glr_ref[...].astype(BF16), wgk_ref[...], bgk_ref[...],
                                  tril_ref, pm_ref, levels, seg)
    smask = _state_mask()
    seq_o = _iota_div((rows, GLA_V), 0, seg)
    for j in range(rows // seg):
        st = s0_ref[j]
        oj = lax.dot_general(qb, st.astype(BF16), NT, preferred_element_type=F32)
        o = o + jnp.where(seq_o == j, oj, 0.0)
        u = lax.dot_general(jnp.where(seq_o == j, v, jnp.zeros_like(v)), kd, TN, preferred_element_type=F32)
        sout_ref[j] = st * jnp.exp(b_end[j * seg:j * seg + 1, :]) + jnp.where(smask, u, 0.0)
    o_ref[...] = _gla_finish(o, r_ref[...], gg_ref[...])


def _gla_sample(pa, s0t, wgk, bgk, gg, batch, seq, rows):
    seg = math.gcd(seq, GLA_CHUNK)
    assert seg == seq and seg % SUBLANE == 0, "sample sequences must be one sublane-aligned chunk"
    tril, pm, levels = _gla_tables(rows, seg)
    nseq = rows // seg
    row = lambda i: (i, 0)
    const2 = lambda i: (0, 0)
    return pl.pallas_call(
        functools.partial(_gla_sample_kernel, seg=seg, levels=levels),
        grid=(batch * seq // rows,),
        in_specs=[pl.BlockSpec((rows, GLA_QK), row),
                  pl.BlockSpec((rows, GLA_QK), lambda i: (i, 1)),
                  pl.BlockSpec((rows, GLA_V), lambda i: (i, 1)),
                  pl.BlockSpec((rows, GLA_V), lambda i: (i, 2)),
                  pl.BlockSpec((rows, LANE), lambda i: (i, (PA_W - LANE) // LANE)),
                  pl.BlockSpec(wgk.shape, const2),
                  pl.BlockSpec(bgk.shape, const2),
                  pl.BlockSpec(gg.shape, const2),
                  pl.BlockSpec(tril.shape, const2),
                  pl.BlockSpec(pm.shape, lambda i: (0, 0, 0)),
                  pl.BlockSpec((nseq, GLA_V, GLA_QK), lambda i: (i, 0, 0))],
        out_specs=[pl.BlockSpec((rows, GLA_V), row),
                   pl.BlockSpec((nseq, GLA_V, GLA_QK), lambda i: (i, 0, 0))],
        out_shape=[jax.ShapeDtypeStruct((batch * seq, GLA_V), F32),
                   jax.ShapeDtypeStruct((batch, GLA_V, GLA_QK), F32)],
        compiler_params=_params("parallel"),
        name="gla_sample",
    )(pa, pa, pa, pa, pa, wgk, bgk, gg, tril, pm, s0t)


def _state_to_blockdiag_t(s):
    b = s.shape[0]
    st = jnp.swapaxes(s, 2, 3)
    eye = jnp.eye(GLA_HEADS, dtype=s.dtype)
    return (st[:, :, :, None, :] * eye[None, :, None, :, None]).reshape(b, GLA_V, GLA_QK)


def _blockdiag_t_to_state(st):
    b = st.shape[0]
    s5 = st.reshape(b, GLA_HEADS, GLA_DV, GLA_HEADS, GLA_DK)
    return jnp.stack([jnp.swapaxes(s5[:, h, :, h, :], 1, 2) for h in range(GLA_HEADS)], axis=1)


def _alibi_slopes():
    return np.asarray([2.0 ** (-8.0 * (h + 1) / DIL_HEADS) for h in range(DIL_HEADS)], np.float64)


DIL_GROUP = 4
DIL_GW = DIL_GROUP * DIL_HD


def _dil_bias():
    i = np.arange(DIL_KEYS)[:, None]
    c = np.arange(2 * DIL_KEYS)[None, :]
    dist = DIL_KEYS + i - c
    ok = (dist >= 0) & (dist <= DIL_KEYS)
    out = np.empty((len(DIL_PATTERNS), DIL_HEADS, DIL_KEYS, 2 * DIL_KEYS), np.float32)
    for p, (_, d) in enumerate(DIL_PATTERNS):
        for h, sl in enumerate(_alibi_slopes()):
            out[p, h] = np.where(ok, -sl * (dist * d), -np.inf)
    out = out.reshape(len(DIL_PATTERNS), DIL_HEADS // DIL_GROUP, DIL_GROUP * DIL_KEYS, 2 * DIL_KEYS)
    return jnp.asarray(np.ascontiguousarray(out.transpose(0, 1, 3, 2)))


def _dil_prompt_kernel(qkv_hbm, biast_ref, o_ref, qkv_ref, acc_ref, m_ref, l_ref, sem):
    seq = qkv_ref.shape[1]
    blk = DIL_KEYS
    ngroups = DIL_HEADS // DIL_GROUP
    wparts = HALF_W // LANE
    aparts = DIL_GW // LANE
    head = _iota_div((blk, DIL_GW), 1, DIL_HD)

    load = pltpu.make_async_copy(qkv_hbm.at[:, pl.ds(pl.multiple_of(pl.program_id(0) * seq, seq), seq), :],
                                 qkv_ref, sem)
    load.start()
    load.wait()

    def rows(start, n, d):
        return pl.ds(pl.multiple_of(start, blk), n) if d == 1 else pl.ds(start, n, stride=d)

    def unpack(which, sel):
        return _unpack_halves(jnp.concatenate([qkv_ref[which * wparts + j, sel, :] for j in range(wparts)], axis=1))

    def by_head(x, keep):
        return jnp.concatenate([jnp.where(keep(hh), x, jnp.zeros_like(x)) for hh in range(DIL_GROUP)], axis=0)

    def stacked_q(q2, g):
        return by_head(q2[g] * (DIL_HD ** -0.5), lambda hh: head == hh)

    def find_max(p, d, start, has_prev):
        qsel = rows(start, blk, d)
        ksel = rows(start - d * blk, 2 * blk, d) if has_prev else qsel
        q2, k2 = unpack(0, qsel), unpack(1, ksel)
        per_head = []
        for g in range(ngroups):
            bias = biast_ref[p, g] if has_prev else biast_ref[p, g, blk:2 * blk, :]
            st = lax.dot_general(k2[g], stacked_q(q2, g), NT, preferred_element_type=F32) + bias
            mg = jnp.max(st, axis=0, keepdims=True)
            per_head += [mg[:, hh * blk:(hh + 1) * blk] for hh in range(DIL_GROUP)]
        rest = jnp.full((blk - DIL_HEADS, blk), -jnp.inf, F32)
        mt = jnp.concatenate(per_head + [rest], axis=0).T
        m_ref[qsel, :] = mt if p == 0 else jnp.maximum(m_ref[qsel, :], mt)

    def accumulate(p, d, start, has_prev):
        qsel = rows(start, blk, d)
        ksel = rows(start - d * blk, 2 * blk, d) if has_prev else qsel
        q2, k2, v2 = unpack(0, qsel), unpack(1, ksel), unpack(2, ksel)
        m_rows = m_ref[qsel, :].T
        per_head = []
        for g in range(ngroups):
            bias = biast_ref[p, g] if has_prev else biast_ref[p, g, blk:2 * blk, :]
            st = lax.dot_general(k2[g], stacked_q(q2, g), NT, preferred_element_type=F32) + bias
            m_g = jnp.concatenate([m_rows[g * DIL_GROUP + hh:g * DIL_GROUP + hh + 1, :] for hh in range(DIL_GROUP)],
                                  axis=1)
            pt = jnp.exp(st - m_g)
            l_g = jnp.sum(pt, axis=0, keepdims=True)
            per_head += [l_g[:, hh * blk:(hh + 1) * blk] for hh in range(DIL_GROUP)]
            out = lax.dot_general(pt.astype(BF16), v2[g], TN, preferred_element_type=F32)
            new = jnp.zeros((blk, DIL_GW), F32)
            for hh in range(DIL_GROUP):
                new = jnp.where(head == hh, out[hh * blk:(hh + 1) * blk], new)
            for j in range(aparts):
                part = new[:, j * LANE:(j + 1) * LANE]
                acc_ref[g * aparts + j, qsel, :] = part if p == 0 else acc_ref[g * aparts + j, qsel, :] + part
        rest = jnp.zeros((blk - DIL_HEADS, blk), F32)
        l_new = jnp.concatenate(per_head + [rest], axis=0).T
        l_ref[qsel, :] = l_new if p == 0 else l_ref[qsel, :] + l_new

    def sweep(block):
        for p, (_, d) in enumerate(DIL_PATTERNS):
            nblk = seq // (d * blk)

            def first_block(r, carry, p=p, d=d):
                block(p, d, r, False)
                return carry

            def later_blocks(r, carry, p=p, d=d, nblk=nblk):
                def one(ib, carry):
                    block(p, d, ib * (d * blk) + r, True)
                    return carry
                return lax.fori_loop(1, nblk, one, carry, unroll=2)

            lax.fori_loop(0, d, first_block, 0, unroll=2 if d > 1 else 1)
            if nblk > 1:
                lax.fori_loop(0, d, later_blocks, 0, unroll=2 if nblk == 2 else 1)

    sweep(find_max)
    sweep(accumulate)

    half = _iota_div((blk, LANE), 1, DIL_HD)

    def normalise(i, carry):
        sel = pl.ds(pl.multiple_of(i * blk, blk), blk)
        inv = 1.0 / l_ref[sel, :]
        for j in range(DIL_W // LANE):
            scale = jnp.where(half == 0, inv[:, 2 * j:2 * j + 1], inv[:, 2 * j + 1:2 * j + 2])
            o_ref[sel, j * LANE:(j + 1) * LANE] = (acc_ref[j, sel, :] * scale).astype(o_ref.dtype)
        return carry

    lax.fori_loop(0, seq // blk, normalise, 0)


def _dil_prompt(qkv, batch, seq):
    for w, d in DIL_PATTERNS:
        assert w // d == DIL_KEYS and seq % (d * DIL_KEYS) == 0
    biast = _dil_bias()
    return pl.pallas_call(
        _dil_prompt_kernel,
        grid=(batch,),
        in_specs=[pl.BlockSpec(memory_space=pl.ANY),
                  pl.BlockSpec(biast.shape, lambda b: (0, 0, 0, 0))],
        out_specs=pl.BlockSpec((seq, DIL_W), lambda b: (b, 0)),
        out_shape=jax.ShapeDtypeStruct((batch * seq, DIL_W), BF16),
        scratch_shapes=[pltpu.VMEM((qkv.shape[0], seq, LANE), I32),
                        pltpu.VMEM((DIL_W // LANE, seq, LANE), F32),
                        pltpu.VMEM((seq, LANE), F32), pltpu.VMEM((seq, LANE), F32),
                        pltpu.SemaphoreType.DMA(())],
        compiler_params=_params("arbitrary"),
        name="dil_prompt",
    )(qkv, biast)


def _native_bias(positions, heads, past, seq):
    positions, heads = np.asarray(positions)[:, None], np.asarray(heads)[:, None]
    col_h = np.repeat(np.arange(DIL_HEADS), seq)[None, :]
    dist = past + np.tile(np.arange(seq), DIL_HEADS)[None, :] - positions
    mult = np.zeros(dist.shape, np.float64)
    for w, d in DIL_PATTERNS:
        mult += (heads == col_h) & (dist >= 0) & (dist <= w) & (dist % d == 0)
    with np.errstate(divide="ignore"):
        logm = np.log(mult)
    return jnp.asarray((-_alibi_slopes()[col_h] * dist + logm).astype(np.float32))


def _dil_native_kernel(q_ref, kn_ref, vn_ref, kf_ref, kr_ref, vf_ref, vr_ref, bf_ref, br_ref, bn_ref, o_ref, *, seq):
    def flat(ref):
        x = ref[...]
        return x.reshape(-1, DIL_HD).astype(BF16)

    def heads_to_rows(x):
        return jnp.concatenate([x[:, h * DIL_HD:(h + 1) * DIL_HD] for h in range(DIL_HEADS)], axis=0)

    qm = heads_to_rows(q_ref[...] * (DIL_HD ** -0.5)).astype(BF16)
    ks = [flat(kf_ref), flat(kr_ref), heads_to_rows(kn_ref[...]).astype(BF16)]
    vs = [flat(vf_ref), flat(vr_ref), heads_to_rows(vn_ref[...]).astype(BF16)]
    ss = [lax.dot_general(k, qm, NT, preferred_element_type=F32) + b[...]
          for k, b in zip(ks, (bf_ref, br_ref, bn_ref))]
    m = functools.reduce(jnp.maximum, [jnp.max(s, axis=0, keepdims=True) for s in ss])
    ps = [jnp.exp(s - m) for s in ss]
    l = sum(jnp.sum(p, axis=0, keepdims=True) for p in ps)
    ot = sum(lax.dot_general(v, p.astype(BF16), TN, preferred_element_type=F32) for p, v in zip(ps, vs))
    o = (ot * (1.0 / l)).T
    o_ref[...] = jnp.concatenate([o[h * seq:(h + 1) * seq] for h in range(DIL_HEADS)], axis=1)


def _dil_native(pb, k_new, v_new, cache_k, cache_v, batch, seq):
    past = cache_k.shape[1]
    w_far, d_far = max(DIL_PATTERNS, key=lambda wd: wd[1])
    cut = past - max(w for w, d in DIL_PATTERNS if d != d_far)
    nfar, nnear = cut // d_far, (past - cut) // d_far
    assert cut > 0 and past % d_far == 0 and cut % d_far == 0 and seq <= d_far and w_far <= past
    assert all(seq >= d for _, d in DIL_PATTERNS if d != d_far) and nfar % nnear == 0
    view = lambda c: c.reshape(batch, past // d_far, d_far, DIL_HEADS, DIL_HD)
    far_spec = pl.BlockSpec((None, nfar, seq, DIL_HEADS, DIL_HD), lambda b: (b, 0, 0, 0, 0))
    near_spec = pl.BlockSpec((None, nnear, d_far, DIL_HEADS, DIL_HD), lambda b: (b, nfar // nnear, 0, 0, 0))
    i, r, h = np.meshgrid(np.arange(nfar), np.arange(seq), np.arange(DIL_HEADS), indexing="ij")
    bias_far = _native_bias((i * d_far + r).reshape(-1), h.reshape(-1), past, seq)
    i, r, h = np.meshgrid(np.arange(nnear), np.arange(d_far), np.arange(DIL_HEADS), indexing="ij")
    bias_near = _native_bias((cut + i * d_far + r).reshape(-1), h.reshape(-1), past, seq)
    bias_new = _native_bias(past + np.tile(np.arange(seq), DIL_HEADS), np.repeat(np.arange(DIL_HEADS), seq), past, seq)
    row = pl.BlockSpec((seq, DIL_W), lambda b: (b, 0))
    const = lambda a: pl.BlockSpec(a.shape, lambda b: (0, 0))
    return pl.pallas_call(
        functools.partial(_dil_native_kernel, seq=seq),
        grid=(batch,),
        in_specs=[row, row, row, far_spec, near_spec, far_spec, near_spec,
                  const(bias_far), const(bias_near), const(bias_new)],
        out_specs=row,
        out_shape=jax.ShapeDtypeStruct((batch * seq, DIL_W), F32),
        compiler_params=_params("parallel"),
        name="dil_sample",
    )(pb, k_new, v_new, view(cache_k), view(cache_k), view(cache_v), view(cache_v), bias_far, bias_near, bias_new)


def _out_proj_kernel(oa_ref, ob_ref, x_ref, wo_ref, g_ref, wq_ref, h_ref, q_ref, *, qscale):
    o = jnp.concatenate([oa_ref[...].astype(BF16), ob_ref[...].astype(BF16)], axis=1)
    h = x_ref[...] + jnp.dot(o, wo_ref[...], preferred_element_type=F32)
    h_ref[...] = h
    n = _rms(h, g_ref[...]).astype(BF16)
    q_ref[...] = (jnp.dot(n, wq_ref[...], preferred_element_type=F32) * qscale).astype(q_ref.dtype)


def _out_proj(oa, ob, x, wo, g, wq, q_dtype, tm):
    n, d = x.shape
    row = lambda i: (i, 0)
    const = lambda i: (0, 0)
    return pl.pallas_call(
        functools.partial(_out_proj_kernel, qscale=(d // MEM_HEADS) ** -0.5),
        grid=(n // tm,),
        in_specs=[pl.BlockSpec((tm, GLA_V), row), pl.BlockSpec((tm, DIL_W), row), pl.BlockSpec((tm, d), row),
                  pl.BlockSpec(wo.shape, const), pl.BlockSpec((1, d), const), pl.BlockSpec(wq.shape, const)],
        out_specs=[pl.BlockSpec((tm, d), row), pl.BlockSpec((tm, d), row)],
        out_shape=[jax.ShapeDtypeStruct((n, d), F32), jax.ShapeDtypeStruct((n, d), q_dtype)],
        compiler_params=_params("parallel"),
        name="out_proj",
    )(oa, ob, x, wo, g, wq)


def _cross_kernel(q_ref, mk_ref, mv_ref, o_ref):
    hd = q_ref.shape[1] // MEM_HEADS
    for h in range(MEM_HEADS):
        sl = slice(h * hd, (h + 1) * hd)
        s = lax.dot_general(q_ref[:, sl].astype(BF16), mk_ref[:, sl].astype(BF16), NT, preferred_element_type=F32)
        p = jnp.exp(s - jnp.max(s, axis=-1, keepdims=True))
        l = jnp.sum(p, axis=-1, keepdims=True)
        o = jnp.dot(p.astype(BF16), mv_ref[:, sl].astype(BF16), preferred_element_type=F32) * (1.0 / l)
        o_ref[:, sl] = o.astype(o_ref.dtype)


def _cross(q, mk, mv, batch, seq, tq, out_dtype):
    n, d = q.shape
    nt = seq // tq
    nm = mk.shape[1]
    return pl.pallas_call(
        _cross_kernel,
        grid=(batch, nt),
        in_specs=[pl.BlockSpec((tq, d), lambda b, i: (b * nt + i, 0)),
                  pl.BlockSpec((None, nm, d), lambda b, i: (b, 0, 0)),
                  pl.BlockSpec((None, nm, d), lambda b, i: (b, 0, 0))],
        out_specs=pl.BlockSpec((tq, d), lambda b, i: (b * nt + i, 0)),
        out_shape=jax.ShapeDtypeStruct((n, d), out_dtype),
        compiler_params=_params("parallel", "parallel"),
        name="cross_attn",
    )(q, mk, mv)


def _co_proj_kernel(o_ref, h_ref, wo_ref, g_ref, wrh_ref, wrl_ref, br_ref, h2_ref, n3_ref, lg_ref):
    h2 = h_ref[...] + jnp.dot(o_ref[...].astype(BF16), wo_ref[...], preferred_element_type=F32)
    h2_ref[...] = h2
    n3 = _rms(h2, g_ref[...])
    hi = n3.astype(BF16)
    lo = (n3 - hi.astype(F32)).astype(BF16)
    n3_ref[...] = hi
    lg_ref[...] = (jnp.dot(hi, wrh_ref[...], preferred_element_type=F32)
                   + jnp.dot(lo, wrh_ref[...], preferred_element_type=F32)
                   + jnp.dot(hi, wrl_ref[...], preferred_element_type=F32) + br_ref[...])


def _co_proj(o, h, wo, g, wrh, wrl, br, tm):
    n, d = h.shape
    row = lambda i: (i, 0)
    const = lambda i: (0, 0)
    return pl.pallas_call(
        _co_proj_kernel,
        grid=(n // tm,),
        in_specs=[pl.BlockSpec((tm, d), row), pl.BlockSpec((tm, d), row), pl.BlockSpec(wo.shape, const),
                  pl.BlockSpec((1, d), const), pl.BlockSpec(wrh.shape, const), pl.BlockSpec(wrl.shape, const),
                  pl.BlockSpec((1, LANE), const)],
        out_specs=[pl.BlockSpec((tm, d), row), pl.BlockSpec((tm, d), row), pl.BlockSpec((tm, LANE), row)],
        out_shape=[jax.ShapeDtypeStruct((n, d), F32), jax.ShapeDtypeStruct((n, d), BF16),
                   jax.ShapeDtypeStruct((n, LANE), F32)],
        compiler_params=_params("parallel"),
        name="co_proj",
    )(o, h, wo, g, wrh, wrl, br)


META_E, META_POS, META_W = 0, TOP_K, 2 * TOP_K


def _router_kernel(lg_ref, tri_ref, meta_ref, tile_ref, cnt_ref, run_ref, *, chunk):
    @pl.when(pl.program_id(0) == 0)
    def _():
        run_ref[...] = jnp.zeros_like(run_ref)

    lg = lg_ref[...]
    tm = lg.shape[0]
    lane = lax.broadcasted_iota(I32, (tm, LANE), 1)
    lane_f = lane.astype(F32)
    ninf = -jnp.inf
    first = lambda hit: jnp.min(jnp.where(hit, lane_f, float(LANE)), axis=-1, keepdims=True).astype(I32)

    gl = jnp.where(lane < N_GROUPS, lg, ninf)
    gmax = jnp.max(gl, axis=-1, keepdims=True)
    gidx = first(gl == gmax)
    pg = 1.0 / jnp.sum(jnp.exp(gl - gmax), axis=-1, keepdims=True)

    ex = lane - N_GROUPS
    el = jnp.where((ex >= 0) & (ex < N_EXPERTS) & ((ex >> int(math.log2(EXP_PER_GROUP))) == gidx), lg, ninf)
    t1 = jnp.max(el, axis=-1, keepdims=True)
    i1 = first(el == t1)
    el2 = jnp.where(lane == i1, ninf, el)
    t2 = jnp.max(el2, axis=-1, keepdims=True)
    i2 = first(el2 == t2)
    e = jnp.exp(t2 - t1)
    gates = (pg / (1.0 + e), pg * e / (1.0 + e))
    experts = (i1 - N_GROUPS, i2 - N_GROUPS)

    hits = [lane == ex_j for ex_j in experts]
    onehot = jnp.where(hits[0] | hits[1], 1.0, 0.0)
    before = jnp.dot(tri_ref[...], onehot.astype(BF16), preferred_element_type=F32)
    count = jnp.sum(onehot, axis=0, keepdims=True)
    slots = jnp.floor((count + (chunk - 1)) * (1.0 / chunk)) * chunk
    pos = [jnp.sum(jnp.where(lane < ex_j, slots, 0.0) + jnp.where(hit, before, 0.0), axis=-1, keepdims=True)
           for ex_j, hit in zip(experts, hits)]

    meta = jnp.zeros((tm, LANE), F32)
    for base, vals in ((META_E, [x.astype(F32) for x in experts]), (META_POS, pos), (META_W, gates)):
        for j, val in enumerate(vals):
            meta = jnp.where(lane == base + j, val, meta)
    meta_ref[...] = meta

    sub = lax.broadcasted_iota(I32, (SUBLANE, LANE), 0)
    tile_ref[...] = jnp.where(sub == 0, run_ref[...], jnp.where(sub == 1, slots, 0.0))
    run_ref[...] = run_ref[...] + slots
    cnt_ref[...] = run_ref[...]


def _router(logits, tm, chunk):
    n = logits.shape[0]
    tri = jnp.asarray(np.tril(np.ones((tm, tm), np.float32), -1), BF16)
    return pl.pallas_call(
        functools.partial(_router_kernel, chunk=chunk),
        grid=(n // tm,),
        in_specs=[pl.BlockSpec((tm, LANE), lambda i: (i, 0)), pl.BlockSpec((tm, tm), lambda i: (0, 0))],
        out_specs=[pl.BlockSpec((tm, LANE), lambda i: (i, 0)),
                   pl.BlockSpec((SUBLANE, LANE), lambda i: (i, 0)),
                   pl.BlockSpec((SUBLANE, LANE), lambda i: (0, 0))],
        out_shape=[jax.ShapeDtypeStruct((n, LANE), F32),
                   jax.ShapeDtypeStruct((n // tm * SUBLANE, LANE), F32),
                   jax.ShapeDtypeStruct((SUBLANE, LANE), F32)],
        scratch_shapes=[pltpu.VMEM((SUBLANE, LANE), F32)],
        compiler_params=_params("arbitrary"),
        name="router",
    )(logits, tri)


TAB_TOTAL = LANE - 1
FILL_BASE, FILL_N, FILL_TAIL = 0, N_EXPERTS, 2 * N_EXPERTS
FILL_ROWS = 64


def _one_hot(meta, lane_pos, slot):
    return jnp.where(lane_pos == meta[:, META_POS + slot:META_POS + slot + 1], 1.0, 0.0).astype(BF16)


def _chunk_copy(sorted_ref, hbm_ref, sem, chunk, to_hbm, off, base):
    src, dst = sorted_ref.at[pl.ds(off, chunk), :], hbm_ref.at[pl.ds(base, chunk), :]
    return pltpu.make_async_copy(src, dst, sem) if to_hbm else pltpu.make_async_copy(dst, src, sem)


def _start_chunks(tab_ref, sorted_ref, hbm_ref, sem, chunk, to_hbm):
    def one(c, carry):
        _chunk_copy(sorted_ref, hbm_ref, sem, chunk, to_hbm, pl.multiple_of(c * chunk, chunk),
                    pl.multiple_of(tab_ref[0, c], chunk)).start()
        return carry

    lax.fori_loop(0, tab_ref[0, TAB_TOTAL], one, 0)


def _wait_chunks(tab_ref, sorted_ref, hbm_ref, sem, chunk, to_hbm):
    def drain(c, carry):
        _chunk_copy(sorted_ref, hbm_ref, sem, chunk, to_hbm, 0, 0).wait()
        return carry

    lax.fori_loop(0, tab_ref[0, TAB_TOTAL], drain, 0)


def _dispatch_kernel(tab_ref, prev_ref, fill_ref, x_ref, meta_ref, xs_ref, sorted_ref, zero_ref, sems, *, chunk):
    i = pl.program_id(0)
    slot = lax.rem(i, 2)
    tm, npos = x_ref.shape[0], sorted_ref.shape[1]
    meta = meta_ref[...]
    lane_pos = lax.broadcasted_iota(I32, (tm, npos), 1).astype(F32)
    place = _one_hot(meta, lane_pos, 0) + _one_hot(meta, lane_pos, 1)
    sorted_ref[slot] = lax.dot_general(place, x_ref[...], TN, preferred_element_type=F32)

    @pl.when(i > 0)
    def _():
        _wait_chunks(prev_ref, sorted_ref.at[1 - slot], xs_ref, sems.at[1 - slot], chunk, True)

    _start_chunks(tab_ref, sorted_ref.at[slot], xs_ref, sems.at[slot], chunk, True)

    @pl.when(i == pl.num_programs(0) - 1)
    def _():
        _wait_chunks(tab_ref, sorted_ref.at[slot], xs_ref, sems.at[slot], chunk, True)
        zero_ref[...] = jnp.zeros_like(zero_ref)
        big = zero_ref.shape[0]
        sem = sems.at[0]

        def fill(row, size):
            return pltpu.make_async_copy(zero_ref.at[pl.ds(0, size), :],
                                         xs_ref.at[pl.ds(pl.multiple_of(row, size), size), :], sem)

        def per_expert(e, total):
            def one(c, carry):
                fill(fill_ref[0, FILL_BASE + e] + c * chunk, chunk).start()
                return carry
            lax.fori_loop(0, fill_ref[0, FILL_N + e], one, 0)
            return total + fill_ref[0, FILL_N + e]

        def drain(c, carry):
            fill(0, chunk).wait()
            return carry

        lax.fori_loop(0, lax.fori_loop(0, N_EXPERTS, per_expert, 0), drain, 0)

        def tail(c, carry):
            fill(fill_ref[0, FILL_TAIL] + c * big, big).start()
            return carry

        def drain_tail(c, carry):
            fill(0, big).wait()
            return carry

        lax.fori_loop(0, fill_ref[0, FILL_TAIL + 1], tail, 0)
        lax.fori_loop(0, fill_ref[0, FILL_TAIL + 1], drain_tail, 0)


def _dispatch(x, meta, tab, fill, rows, tm, chunk):
    n, d = x.shape
    npos = TOP_K * tm + N_EXPERTS * chunk
    return pl.pallas_call(
        functools.partial(_dispatch_kernel, chunk=chunk),
        grid=(n // tm,),
        in_specs=[pl.BlockSpec((None, 1, LANE), lambda i: (i, 0, 0), memory_space=pltpu.SMEM),
                  pl.BlockSpec((None, 1, LANE), lambda i: (jnp.maximum(i - 1, 0), 0, 0), memory_space=pltpu.SMEM),
                  pl.BlockSpec((1, LANE), lambda i: (0, 0), memory_space=pltpu.SMEM),
                  pl.BlockSpec((tm, d), lambda i: (i, 0)),
                  pl.BlockSpec((tm, LANE), lambda i: (i, 0))],
        out_specs=pl.BlockSpec(memory_space=pl.ANY),
        out_shape=jax.ShapeDtypeStruct((rows, d), F32),
        scratch_shapes=[pltpu.VMEM((2, npos, d), F32), pltpu.VMEM((FILL_ROWS, d), F32),
                        pltpu.SemaphoreType.DMA((2,))],
        compiler_params=_params("arbitrary"),
        name="moe_dispatch",
    )(tab, tab, fill, x, meta)


def _expert_kernel(be_ref, nu_ref, x_ref, w1_ref, w3_ref, w2_ref, y_ref):
    del be_ref
    live = pl.program_id(0) < nu_ref[0]

    @pl.when(live)
    def _():
        x = x_ref[...].astype(BF16)
        a = jnp.dot(x, w1_ref[...], preferred_element_type=F32)
        b = jnp.dot(x, w3_ref[...], preferred_element_type=F32)
        y_ref[...] = jnp.dot((_silu(a) * b).astype(BF16), w2_ref[...], preferred_element_type=F32)

    @pl.when(jnp.logical_not(live))
    def _():
        y_ref[...] = jnp.zeros_like(y_ref)


def _experts(xs, block_e, n_used, w1, w3, w2, bm):
    rows, d = xs.shape
    de = w1.shape[2]
    return pl.pallas_call(
        _expert_kernel,
        grid_spec=pltpu.PrefetchScalarGridSpec(
            num_scalar_prefetch=2,
            grid=(rows // bm,),
            in_specs=[pl.BlockSpec((bm, d), lambda i, be, nu: (i, 0)),
                      pl.BlockSpec((None, d, de), lambda i, be, nu: (be[i], 0, 0)),
                      pl.BlockSpec((None, d, de), lambda i, be, nu: (be[i], 0, 0)),
                      pl.BlockSpec((None, de, d), lambda i, be, nu: (be[i], 0, 0))],
            out_specs=pl.BlockSpec((bm, d), lambda i, be, nu: (i, 0))),
        out_shape=jax.ShapeDtypeStruct((rows, d), F32),
        compiler_params=_params("arbitrary"),
        name="moe_experts",
    )(block_e, n_used, xs, w1, w3, w2)


def _combine_kernel(tab_ref, next_ref, h_ref, meta_ref, g_ref, yb_ref, y_ref, sorted_ref, sems, *, chunk):
    i = pl.program_id(0)
    slot = lax.rem(i, 2)

    @pl.when(i == 0)
    def _():
        sorted_ref[...] = jnp.zeros_like(sorted_ref)
        _start_chunks(tab_ref, sorted_ref.at[0], yb_ref, sems.at[0], chunk, False)

    @pl.when(i + 1 < pl.num_programs(0))
    def _():
        _start_chunks(next_ref, sorted_ref.at[1 - slot], yb_ref, sems.at[1 - slot], chunk, False)

    _wait_chunks(tab_ref, sorted_ref.at[slot], yb_ref, sems.at[slot], chunk, False)
    tm, npos = h_ref.shape[0], sorted_ref.shape[1]
    meta = meta_ref[...]
    lane_pos = lax.broadcasted_iota(I32, (tm, npos), 1).astype(F32)
    yb = sorted_ref[slot].astype(BF16)
    moe = (meta[:, META_W:META_W + 1] * jnp.dot(_one_hot(meta, lane_pos, 0), yb, preferred_element_type=F32)
           + meta[:, META_W + 1:META_W + 2] * jnp.dot(_one_hot(meta, lane_pos, 1), yb, preferred_element_type=F32))
    y_ref[...] = _rms(h_ref[...] + moe, g_ref[...])


def _combine(h, meta, tab, yb, g, tm, chunk):
    n, d = h.shape
    npos = TOP_K * tm + N_EXPERTS * chunk
    return pl.pallas_call(
        functools.partial(_combine_kernel, chunk=chunk),
        grid=(n // tm,),
        in_specs=[pl.BlockSpec((None, 1, LANE), lambda i: (i, 0, 0), memory_space=pltpu.SMEM),
                  pl.BlockSpec((None, 1, LANE), lambda i: (jnp.minimum(i + 1, n // tm - 1), 0, 0),
                               memory_space=pltpu.SMEM),
                  pl.BlockSpec((tm, d), lambda i: (i, 0)),
                  pl.BlockSpec((tm, LANE), lambda i: (i, 0)),
                  pl.BlockSpec((1, d), lambda i: (0, 0)),
                  pl.BlockSpec(memory_space=pl.ANY)],
        out_specs=pl.BlockSpec((tm, d), lambda i: (i, 0)),
        out_shape=jax.ShapeDtypeStruct((n, d), F32),
        scratch_shapes=[pltpu.VMEM((2, npos, d), F32), pltpu.SemaphoreType.DMA((2,))],
        compiler_params=_params("arbitrary"),
        name="moe_combine",
    )(tab, tab, h, meta, g, yb)


def _moe_final(h2, n3, logits, w1, w3, w2, g_final, tm, bm, chunk):
    n = h2.shape[0]
    nt = n // tm
    max_chunks = (TOP_K * tm + N_EXPERTS * chunk) // chunk
    assert max_chunks <= TAB_TOTAL and bm % chunk == 0
    meta, tiles, cnt = _router(logits, tm, chunk)
    rows_e = cnt[0, :N_EXPERTS].astype(I32)
    padded = (rows_e + bm - 1) // bm * bm
    pad_end = jnp.cumsum(padded)
    pad_start = pad_end - padded
    tiles = tiles.reshape(nt, SUBLANE, LANE)
    run_before = tiles[:, 0, :N_EXPERTS].astype(I32)
    nchunks = tiles[:, 1, :N_EXPERTS].astype(I32) // chunk
    last_chunk = jnp.cumsum(nchunks, axis=1)
    c = jnp.arange(max_chunks, dtype=I32)[None, :]
    e_of_c = jnp.minimum(jnp.sum((last_chunk[:, None, :] <= c[:, :, None]).astype(I32), axis=2), N_EXPERTS - 1)
    is_e = e_of_c[:, :, None] == jnp.arange(N_EXPERTS, dtype=I32)[None, None, :]
    pick = lambda a: jnp.sum(jnp.where(is_e, a[:, None, :], 0), axis=2)
    chunk_row = pick(pad_start[None, :] + run_before) + (c - pick(last_chunk - nchunks)) * chunk
    pad_lanes = lambda a: jnp.pad(a, ((0, 0), (0, LANE - a.shape[1])))
    tab = jnp.concatenate([pad_lanes(chunk_row)[:, :TAB_TOTAL], last_chunk[:, -1:]], axis=1).reshape(nt, 1, LANE)
    max_rows = n * TOP_K + (chunk - 1) * min(n * TOP_K, nt * N_EXPERTS) + N_EXPERTS * (bm - 1)
    nb = -(-max_rows // bm)
    assert bm % FILL_ROWS == 0
    fill = pad_lanes(jnp.concatenate([pad_start + rows_e, (padded - rows_e) // chunk,
                                      pad_end[-1:], (nb * bm - pad_end[-1:]) // FILL_ROWS])[None, :])
    block_start = jnp.arange(nb, dtype=I32) * bm
    block_e = jnp.minimum(jnp.sum((pad_end[None, :] <= block_start[:, None]).astype(I32), axis=1), N_EXPERTS - 1)
    n_used = (pad_end[-1:] // bm).astype(I32)
    xs = _dispatch(n3, meta, tab, fill, nb * bm, tm, chunk)
    yb = _experts(xs, block_e, n_used, w1, w3, w2, bm)
    return _combine(h2, meta, tab, yb, g_final, tm, chunk)


def _group(x3, mk, mv, w, keep, *, cache=None, state=None):
    batch, seq, d = x3.shape
    n = batch * seq
    x = x3.reshape(n, d)
    tm = _tile(n, TOKEN_TILE)
    sample = cache is not None
    pa, pb, k_new, v_new = _norm_proj(x, w["g1"], w["w_in"], batch, seq, keep, not sample, tm)
    if sample:
        assert keep == seq
        oa, st = _gla_sample(pa, _state_to_blockdiag_t(state), w["wgk"], w["bgk"], w["gg"], batch, seq,
                             rows=GLA_CHUNK)
        ob = _dil_native(pb, k_new, v_new, cache[0].reshape(batch, -1, DIL_HEADS, DIL_HD),
                         cache[1].reshape(batch, -1, DIL_HEADS, DIL_HD), batch, seq)
    else:
        zeros = jnp.zeros((batch, GLA_V, GLA_QK), F32)
        oa, st = _gla_prompt(pa, zeros, w["wgk"], w["bgk"], w["gg"], batch, seq, tb=_tile(seq, TOKEN_TILE))
        ob = _dil_prompt(pb, batch, seq)
    h1, qc = _out_proj(oa, ob, x, w["w_out"], w["g2"], w["w_cq"], F32 if sample else BF16, tm)
    oc = _cross(qc, mk, mv, batch, seq, _tile(seq, TOKEN_TILE), F32 if sample else BF16)
    h2, n3, logits = _co_proj(oc, h1, w["w_co"], w["g3"], w["wr_hi"], w["wr_lo"], w["br"], tm)
    y = _moe_final(h2, n3, logits, w["w_e1"], w["w_e3"], w["w_e2"], w["g_final"], _tile(n, MOE_TILE), MOE_BLOCK,
                   MOE_CHUNK)
    return (y.reshape(batch, seq, d), _blockdiag_t_to_state(st),
            k_new.reshape(batch, keep, DIL_HEADS, DIL_HD), v_new.reshape(batch, keep, DIL_HEADS, DIL_HD))


def _pack_weights(l, g_norm1, w_in, w_gk2, b_gk, g_gla_out, w_out, g_norm2, w_cq, w_co, g_norm3, w_gr, b_gr,
                  w_er, b_er, w_e1, w_e3, w_e2, g_final):
    d = w_in.shape[1]
    sp = np.cumsum([GLA_QK, GLA_QK, GLA_V, GATE_RANK, GLA_V, DIL_W, DIL_W, DIL_W])
    wi = w_in[l]
    q_a, k_a, v_a, glr, r_a, q_b, k_b, v_b = (wi[:, a:b] for a, b in zip([0, *sp[:-1]], sp))
    glr = jnp.pad(glr, ((0, 0), (0, LANE - GATE_RANK)))
    wr = jnp.pad(jnp.concatenate([w_gr[l], w_er[l]], axis=1), ((0, 0), (0, LANE - N_GROUPS - N_EXPERTS)))
    wr_hi = wr.astype(BF16)
    return dict(
        g1=g_norm1[l].reshape(1, d), g2=g_norm2[l].reshape(1, d), g3=g_norm3[l].reshape(1, d),
        g_final=g_final.reshape(1, d),
        w_in=jnp.concatenate([q_a, k_a, v_a, r_a, glr, q_b, k_b, v_b], axis=1).astype(BF16),
        wgk=jnp.pad(w_gk2[l], ((0, LANE - GATE_RANK), (0, 0))).astype(BF16),
        bgk=b_gk[l].reshape(1, GLA_QK), gg=g_gla_out[l].reshape(1, GLA_DV),
        w_out=w_out[l].astype(BF16), w_cq=w_cq[l].astype(BF16), w_co=w_co[l].astype(BF16),
        wr_hi=wr_hi, wr_lo=(wr - wr_hi.astype(F32)).astype(BF16),
        br=jnp.pad(jnp.concatenate([b_gr[l], b_er[l]]), (0, LANE - N_GROUPS - N_EXPERTS)).reshape(1, LANE),
        w_e1=w_e1[l].astype(BF16), w_e3=w_e3[l].astype(BF16), w_e2=w_e2[l].astype(BF16))


def kernel(x_prompt, x_sample, cache_swa_k, cache_swa_v, state_gla, cache_mem_k, cache_mem_v, mem_prompt,
           g_norm1, w_in, w_gk2, b_gk, g_gla_out, w_out, g_norm2, g_mem, w_cq, w_mk, w_mv, w_co,
           g_norm3, w_gr, b_gr, w_er, b_er, w_e1, w_e3, w_e2, g_final):
    depth = w_in.shape[0]
    assert depth == 1, "the final norm is fused into the last MoE stage; stacked layers are not supported"
    batch, seq, d = x_prompt.shape
    sb, sseq, _ = x_sample.shape
    nm = mem_prompt.shape[1]
    keep = min(DIL_PATTERNS[-1][0], seq)
    l = 0
    w = _pack_weights(l, g_norm1, w_in, w_gk2, b_gk, g_gla_out, w_out, g_norm2, w_cq, w_co, g_norm3,
                      w_gr, b_gr, w_er, b_er, w_e1, w_e3, w_e2, g_final)
    mkv = _norm_matmul(mem_prompt.reshape(batch * nm, d), g_mem[l].reshape(1, d),
                       jnp.concatenate([w_mk[l], w_mv[l]], axis=1).astype(BF16), _tile(batch * nm, TOKEN_TILE))
    mk = mkv[:, :d].reshape(batch, nm, d)
    mv = mkv[:, d:].reshape(batch, nm, d)
    yp, sp, kp, vp = _group(x_prompt, mk, mv, w, keep)
    past = cache_swa_k.shape[2]
    ys, ss, kn, vn = _group(x_sample, cache_mem_k.reshape(sb, nm, d), cache_mem_v.reshape(sb, nm, d), w, sseq,
                            cache=(cache_swa_k.reshape(sb, past, DIL_HEADS, DIL_HD),
                                   cache_swa_v.reshape(sb, past, DIL_HEADS, DIL_HD)),
                            state=state_gla[l])
    heads = lambda m: m.reshape(batch, nm, MEM_HEADS, d // MEM_HEADS)
    return (yp, ys, *(o[None] for o in (kp, vp, sp, heads(mk), heads(mv), kn, vn, ss)))
```

```python
import functools
import math

import numpy as np
import jax
import jax.numpy as jnp
from jax import lax
from jax.experimental import pallas as pl
from jax.experimental.pallas import tpu as pltpu

F32, BF16, I32 = jnp.float32, jnp.bfloat16, jnp.int32

GLA_HEADS, GLA_DK, GLA_DV = 4, 64, 128
GLA_QK, GLA_V = GLA_HEADS * GLA_DK, GLA_HEADS * GLA_DV
GATE_RANK, GATE_NORM, GLA_CHUNK = 16, 16.0, 64
DIL_HEADS, DIL_HD = 8, 64
DIL_W = DIL_HEADS * DIL_HD
DIL_PATTERNS = ((128, 1), (512, 4), (2048, 16))
DIL_KEYS = 128
MEM_HEADS = 4
N_GROUPS, EXP_PER_GROUP, TOP_K = 4, 8, 2
N_EXPERTS = N_GROUPS * EXP_PER_GROUP
EPS = 1e-6

LANE = 128
SUBLANE = 8
VMEM_LIMIT_BYTES = 56 * 1024 * 1024

PA_W = 2 * GLA_QK + 2 * GLA_V + LANE
PB_W = 3 * DIL_W
HALF_W = DIL_W // 2
HI_MASK = -65536

TOKEN_TILE = 512
MOE_TILE = 256
MOE_BLOCK = 512
MOE_CHUNK = SUBLANE

NT = (((1,), (1,)), ((), ()))
TN = (((0,), (0,)), ((), ()))


def _params(*sem):
    return pltpu.CompilerParams(dimension_semantics=sem, vmem_limit_bytes=VMEM_LIMIT_BYTES)


def _tile(n, want):
    t = min(n, want)
    assert n % t == 0
    return t


def _rms(x, g):
    y = x * lax.rsqrt(jnp.mean(x * x, axis=-1, keepdims=True) + EPS)
    return y * g


def _silu(x):
    return x / (1.0 + jnp.exp(-x))


def _iota_div(shape, dim, n):
    assert n & (n - 1) == 0
    return lax.broadcasted_iota(I32, shape, dim) >> int(math.log2(n))


def _pack_halves(r):
    u = lax.bitcast_convert_type(r.astype(BF16).astype(F32), I32)
    lo = u[:, :HALF_W]
    return lax.shift_right_logical(lo, jnp.full_like(lo, 16)) | (u[:, HALF_W:] & HI_MASK)


def _unpack_halves(w):
    lo = lax.bitcast_convert_type(w << 16, F32).astype(BF16)
    hi = lax.bitcast_convert_type(w & HI_MASK, F32).astype(BF16)
    return lo, hi


def _norm_proj_kernel(x_ref, g_ref, w_ref, pa_ref, pb_ref, k_ref, v_ref, *, pack):
    n = _rms(x_ref[...], g_ref[...]).astype(BF16)
    for c0 in range(0, PA_W, DIL_W):
        c1 = min(c0 + DIL_W, PA_W)
        pa_ref[:, c0:c1] = jnp.dot(n, w_ref[:, c0:c1], preferred_element_type=F32).astype(pa_ref.dtype)
    for j, kv_ref in enumerate((None, k_ref, v_ref)):
        r = jnp.dot(n, w_ref[:, PA_W + j * DIL_W:PA_W + (j + 1) * DIL_W], preferred_element_type=F32)
        if pack:
            words = _pack_halves(r)
            for part in range(HALF_W // LANE):
                pb_ref[j * (HALF_W // LANE) + part] = words[:, part * LANE:(part + 1) * LANE]
        else:
            pb_ref[:, j * DIL_W:(j + 1) * DIL_W] = r
        if kv_ref is not None:
            kv_ref[...] = r


def _norm_proj(x, g, w, batch, seq, keep, pack, tm):
    n, d = x.shape
    if keep == seq:
        kv_map = lambda i: (i, 0)
    else:
        nt, nk = seq // tm, keep // tm
        assert keep % tm == 0
        kv_map = lambda i: ((i // nt) * nk + jnp.maximum(i % nt - (nt - nk), 0), 0)
    if pack:
        nparts = 3 * HALF_W // LANE
        pb_spec = pl.BlockSpec((nparts, tm, LANE), lambda i: (0, i, 0))
        pb_shape = jax.ShapeDtypeStruct((nparts, n, LANE), I32)
    else:
        pb_spec = pl.BlockSpec((tm, PB_W), lambda i: (i, 0))
        pb_shape = jax.ShapeDtypeStruct((n, PB_W), F32)
    pa_dtype = BF16 if pack else F32
    return pl.pallas_call(
        functools.partial(_norm_proj_kernel, pack=pack),
        grid=(n // tm,),
        in_specs=[pl.BlockSpec((tm, d), lambda i: (i, 0)),
                  pl.BlockSpec((1, d), lambda i: (0, 0)),
                  pl.BlockSpec((d, PA_W + PB_W), lambda i: (0, 0))],
        out_specs=[pl.BlockSpec((tm, PA_W), lambda i: (i, 0)),
                   pb_spec,
                   pl.BlockSpec((tm, DIL_W), kv_map),
                   pl.BlockSpec((tm, DIL_W), kv_map)],
        out_shape=[jax.ShapeDtypeStruct((n, PA_W), pa_dtype),
                   pb_shape,
                   jax.ShapeDtypeStruct((batch * keep, DIL_W), F32),
                   jax.ShapeDtypeStruct((batch * keep, DIL_W), F32)],
        compiler_params=_params("arbitrary"),
        name="norm_proj",
    )(x, g, w)


def _norm_matmul_kernel(x_ref, g_ref, w_ref, o_ref):
    n = _rms(x_ref[...], g_ref[...]).astype(BF16)
    for c0 in range(0, o_ref.shape[1], DIL_W):
        o_ref[:, c0:c0 + DIL_W] = jnp.dot(n, w_ref[:, c0:c0 + DIL_W], preferred_element_type=F32)


def _norm_matmul(x, g, w, tm):
    n, d = x.shape
    m = w.shape[1]
    return pl.pallas_call(
        _norm_matmul_kernel,
        grid=(n // tm,),
        in_specs=[pl.BlockSpec((tm, d), lambda i: (i, 0)),
                  pl.BlockSpec((1, d), lambda i: (0, 0)),
                  pl.BlockSpec((d, m), lambda i: (0, 0))],
        out_specs=pl.BlockSpec((tm, m), lambda i: (i, 0)),
        out_shape=jax.ShapeDtypeStruct((n, m), F32),
        compiler_params=_params("parallel"),
        name="norm_matmul",
    )(x, g, w)


def _gla_tables(chunk, seg):
    idx = np.arange(chunk)
    tril = ((idx[None, :] <= idx[:, None]) & (idx[None, :] // seg == idx[:, None] // seg)).astype(np.float32)
    masks, levels = [], []
    s = seg // 2
    while s >= 1:
        same = (idx[:, None] // (2 * s)) == (idx[None, :] // (2 * s))
        masks.append(same & ((idx[:, None] // s) % 2 == 1) & ((idx[None, :] // s) % 2 == 0))
        levels.append(s)
        s //= 2
    masks.append(idx[:, None] == idx[None, :])
    pm = np.tile(np.stack(masks).astype(np.float32), (1, 1, GLA_HEADS))
    return jnp.asarray(tril, BF16), jnp.asarray(pm), tuple(levels)


def _block_row(b, blk, idx):
    c, w = b.shape
    b3 = b.reshape(c // blk, blk, w)
    return jnp.broadcast_to(b3[:, idx:idx + 1, :], (c // blk, blk, w)).reshape(c, w)


def _level_ref(b, s, row):
    c = b.shape[0]
    if 2 * s >= SUBLANE:
        return _block_row(b, 2 * s, s - 1)
    down = lambda n: pltpu.roll(b, n, 0)
    if s == 2:
        m = row & 3
        return jnp.where(m == 0, pltpu.roll(b, c - 1, 0), jnp.where(m == 1, b, jnp.where(m == 2, down(1), down(2))))
    assert s == 1
    return jnp.where((row & 1) == 1, down(1), b)


def _gla_chunk(q, k, v, glr, wgk, bgk, tril_ref, pm_ref, levels, seg):
    c = q.shape[0]
    nl = len(levels)
    gk = jnp.dot(glr, wgk, preferred_element_type=F32) + bgk
    la = (jnp.minimum(gk, 0.0) - jnp.log(1.0 + jnp.exp(-jnp.abs(gk)))) * (1.0 / GATE_NORM)
    hi = la.astype(BF16)
    r1 = la - hi.astype(F32)
    mid = r1.astype(BF16)
    lo = (r1 - mid.astype(F32)).astype(BF16)
    b3 = jnp.dot(tril_ref[...], jnp.concatenate([hi, mid, lo], axis=1), preferred_element_type=F32)
    b = b3[:, :GLA_QK] + b3[:, GLA_QK:2 * GLA_QK] + b3[:, 2 * GLA_QK:]
    b_end = _block_row(b, seg, seg - 1)

    row = lax.broadcasted_iota(I32, (c, GLA_QK), 0)
    khead = _iota_div((c, GLA_QK), 1, GLA_DK)
    vhead = _iota_div((c, GLA_V), 1, GLA_DV)

    def by_head(x, head):
        return jnp.concatenate([jnp.where(head == h, x, jnp.zeros_like(x)) for h in range(GLA_HEADS)], axis=0)

    def level(qe, ke, pm):
        a = lax.dot_general(qe, by_head(ke, khead), NT, preferred_element_type=F32)
        return jnp.where(pm > 0.0, a, 0.0)

    acc = level(q.astype(BF16), k.astype(BF16), pm_ref[nl])
    for l, s in enumerate(levels):
        ref = _level_ref(b, s, row)
        right = ((row >> int(math.log2(s))) & 1) == 1
        qe = (q * jnp.exp(jnp.where(right, b - ref, 0.0))).astype(BF16)
        ke = (k * jnp.exp(jnp.where(right, 0.0, ref - b))).astype(BF16)
        acc = acc + level(qe, ke, pm_ref[l])

    o = jnp.dot(acc.astype(BF16), by_head(v, vhead), preferred_element_type=F32)
    qb = (q * jnp.exp(b)).astype(BF16)
    kd = (k * jnp.exp(b_end - b)).astype(BF16)
    return o, qb, kd, b_end


def _gla_finish(o, r, gg):
    outs = []
    for h in range(GLA_HEADS):
        sl = slice(h * GLA_DV, (h + 1) * GLA_DV)
        outs.append(_rms(o[:, sl], gg) * _silu(r[:, sl]))
    return jnp.concatenate(outs, axis=1)


def _state_mask():
    return _iota_div((GLA_V, GLA_QK), 0, GLA_DV) == _iota_div((GLA_V, GLA_QK), 1, GLA_DK)


def _gla_prompt_kernel(q_ref, k_ref, v_ref, r_ref, glr_ref, wgk_ref, bgk_ref, gg_ref, tril_ref, pm_ref, s0_ref,
                       o_ref, sout_ref, st_ref, *, chunk, levels):
    i = pl.program_id(1)

    @pl.when(i == 0)
    def _():
        st_ref[...] = s0_ref[...]

    smask = _state_mask()

    def body(c, carry):
        rows = pl.ds(pl.multiple_of(c * chunk, chunk), chunk)
        q = q_ref[rows, :].astype(F32) * (GLA_DK ** -0.5)
        k = k_ref[rows, :].astype(F32)
        v = v_ref[rows, :]
        o, qb, kd, b_end = _gla_chunk(q, k, v, glr_ref[rows, :], wgk_ref[...], bgk_ref[...],
                                      tril_ref, pm_ref, levels, chunk)
        st = st_ref[...]
        o = o + lax.dot_general(qb, st.astype(BF16), NT, preferred_element_type=F32)
        u = lax.dot_general(v, kd, TN, preferred_element_type=F32)
        st_ref[...] = st * jnp.exp(b_end[0:1, :]) + jnp.where(smask, u, 0.0)
        o_ref[rows, :] = _gla_finish(o, r_ref[rows, :].astype(F32), gg_ref[...]).astype(o_ref.dtype)
        return carry

    lax.fori_loop(0, q_ref.shape[0] // chunk, body, 0, unroll=4)

    @pl.when(i == pl.num_programs(1) - 1)
    def _():
        sout_ref[...] = st_ref[...]


def _gla_prompt(pa, s0t, wgk, bgk, gg, batch, seq, tb):
    chunk = math.gcd(seq, GLA_CHUNK)
    tril, pm, levels = _gla_tables(chunk, chunk)
    nt = seq // tb
    row = lambda b, i: (b * nt + i, 0)
    const2 = lambda b, i: (0, 0)
    return pl.pallas_call(
        functools.partial(_gla_prompt_kernel, chunk=chunk, levels=levels),
        grid=(batch, nt),
        in_specs=[pl.BlockSpec((tb, GLA_QK), row),
                  pl.BlockSpec((tb, GLA_QK), lambda b, i: (b * nt + i, 1)),
                  pl.BlockSpec((tb, GLA_V), lambda b, i: (b * nt + i, 1)),
                  pl.BlockSpec((tb, GLA_V), lambda b, i: (b * nt + i, 2)),
                  pl.BlockSpec((tb, LANE), lambda b, i: (b * nt + i, (PA_W - LANE) // LANE)),
                  pl.BlockSpec(wgk.shape, const2),
                  pl.BlockSpec(bgk.shape, const2),
                  pl.BlockSpec(gg.shape, const2),
                  pl.BlockSpec(tril.shape, const2),
                  pl.BlockSpec(pm.shape, lambda b, i: (0, 0, 0)),
                  pl.BlockSpec((None, GLA_V, GLA_QK), lambda b, i: (b, 0, 0))],
        out_specs=[pl.BlockSpec((tb, GLA_V), row),
                   pl.BlockSpec((None, GLA_V, GLA_QK), lambda b, i: (b, 0, 0))],
        out_shape=[jax.ShapeDtypeStruct((batch * seq, GLA_V), BF16),
                   jax.ShapeDtypeStruct((batch, GLA_V, GLA_QK), F32)],
        scratch_shapes=[pltpu.VMEM((GLA_V, GLA_QK), F32)],
        compiler_params=_params("parallel", "arbitrary"),
        name="gla_prompt",
    )(pa, pa, pa, pa, pa, wgk, bgk, gg, tril, pm, s0t)


def _gla_sample_kernel(q_ref, k_ref, v_ref, r_ref, glr_ref, wgk_ref, bgk_ref, gg_ref, tril_ref, pm_ref, s0_ref,
                       o_ref, sout_ref, *, seg, levels):
    rows = q_ref.shape[0]
    q = q_ref[...] * (GLA_DK ** -0.5)
    v = v_ref[...].astype(BF16)
    o, qb, kd, b_end = _gla_chunk(q, k_ref[...], v, glr_ref[...].astype(BF16), wgk_ref[...], bgk_ref[...],
                                  tril_ref, pm_ref, levels, seg)
    smask = _state_mask()
    seq_o = _iota_div((rows, GLA_V), 0, seg)
    for j in range(rows // seg):
        st = s0_ref[j]
        oj = lax.dot_general(qb, st.astype(BF16), NT, preferred_element_type=F32)
        o = o + jnp.where(seq_o == j, oj, 0.0)
        u = lax.dot_general(jnp.where(seq_o == j, v, jnp.zeros_like(v)), kd, TN, preferred_element_type=F32)
        sout_ref[j] = st * jnp.exp(b_end[j * seg:j * seg + 1, :]) + jnp.where(smask, u, 0.0)
    o_ref[...] = _gla_finish(o, r_ref[...], gg_ref[...])


def _gla_sample(pa, s0t, wgk, bgk, gg, batch, seq, rows):
    seg = math.gcd(seq, GLA_CHUNK)
    assert seg == seq and seg % SUBLANE == 0, "sample sequences must be one sublane-aligned chunk"
    tril, pm, levels = _gla_tables(rows, seg)
    nseq = rows // seg
    row = lambda i: (i, 0)
    const2 = lambda i: (0, 0)
    return pl.pallas_call(
        functools.partial(_gla_sample_kernel, seg=seg, levels=levels),
        grid=(batch * seq // rows,),
        in_specs=[pl.BlockSpec((rows, GLA_QK), row),
                  pl.BlockSpec((rows, GLA_QK), lambda i: (i, 1)),
                  pl.BlockSpec((rows, GLA_V), lambda i: (i, 1)),
                  pl.BlockSpec((rows, GLA_V), lambda i: (i, 2)),
                  pl.BlockSpec((rows, LANE), lambda i: (i, (PA_W - LANE) // LANE)),
                  pl.BlockSpec(wgk.shape, const2),
                  pl.BlockSpec(bgk.shape, const2),
                  pl.BlockSpec(gg.shape, const2),
                  pl.BlockSpec(tril.shape, const2),
                  pl.BlockSpec(pm.shape, lambda i: (0, 0, 0)),
                  pl.BlockSpec((nseq, GLA_V, GLA_QK), lambda i: (i, 0, 0))],
        out_specs=[pl.BlockSpec((rows, GLA_V), row),
                   pl.BlockSpec((nseq, GLA_V, GLA_QK), lambda i: (i, 0, 0))],
        out_shape=[jax.ShapeDtypeStruct((batch * seq, GLA_V), F32),
                   jax.ShapeDtypeStruct((batch, GLA_V, GLA_QK), F32)],
        compiler_params=_params("parallel"),
        name="gla_sample",
    )(pa, pa, pa, pa, pa, wgk, bgk, gg, tril, pm, s0t)


def _state_to_blockdiag_t(s):
    b = s.shape[0]
    st = jnp.swapaxes(s, 2, 3)
    eye = jnp.eye(GLA_HEADS, dtype=s.dtype)
    return (st[:, :, :, None, :] * eye[None, :, None, :, None]).reshape(b, GLA_V, GLA_QK)


def _blockdiag_t_to_state(st):
    b = st.shape[0]
    s5 = st.reshape(b, GLA_HEADS, GLA_DV, GLA_HEADS, GLA_DK)
    return jnp.stack([jnp.swapaxes(s5[:, h, :, h, :], 1, 2) for h in range(GLA_HEADS)], axis=1)


def _alibi_slopes():
    return np.asarray([2.0 ** (-8.0 * (h + 1) / DIL_HEADS) for h in range(DIL_HEADS)], np.float64)


DIL_GROUP = 4
DIL_GW = DIL_GROUP * DIL_HD


def _dil_bias():
    i = np.arange(DIL_KEYS)[:, None]
    c = np.arange(2 * DIL_KEYS)[None, :]
    dist = DIL_KEYS + i - c
    ok = (dist >= 0) & (dist <= DIL_KEYS)
    out = np.empty((len(DIL_PATTERNS), DIL_HEADS, DIL_KEYS, 2 * DIL_KEYS), np.float32)
    for p, (_, d) in enumerate(DIL_PATTERNS):
        for h, sl in enumerate(_alibi_slopes()):
            out[p, h] = np.where(ok, -sl * (dist * d), -np.inf)
    out = out.reshape(len(DIL_PATTERNS), DIL_HEADS // DIL_GROUP, DIL_GROUP * DIL_KEYS, 2 * DIL_KEYS)
    return jnp.asarray(np.ascontiguousarray(out.transpose(0, 1, 3, 2)))


def _dil_prompt_kernel(qkv_hbm, biast_ref, o_ref, qkv_ref, acc_ref, m_ref, l_ref, sem):
    seq = qkv_ref.shape[1]
    blk = DIL_KEYS
    ngroups = DIL_HEADS // DIL_GROUP
    wparts = HALF_W // LANE
    aparts = DIL_GW // LANE
    head = _iota_div((blk, DIL_GW), 1, DIL_HD)

    load = pltpu.make_async_copy(qkv_hbm.at[:, pl.ds(pl.multiple_of(pl.program_id(0) * seq, seq), seq), :],
                                 qkv_ref, sem)
    load.start()
    load.wait()

    def rows(start, n, d):
        return pl.ds(pl.multiple_of(start, blk), n) if d == 1 else pl.ds(start, n, stride=d)

    def unpack(which, sel):
        return _unpack_halves(jnp.concatenate([qkv_ref[which * wparts + j, sel, :] for j in range(wparts)], axis=1))

    def by_head(x, keep):
        return jnp.concatenate([jnp.where(keep(hh), x, jnp.zeros_like(x)) for hh in range(DIL_GROUP)], axis=0)

    def stacked_q(q2, g):
        return by_head(q2[g] * (DIL_HD ** -0.5), lambda hh: head == hh)

    def find_max(p, d, start, has_prev):
        qsel = rows(start, blk, d)
        ksel = rows(start - d * blk, 2 * blk, d) if has_prev else qsel
        q2, k2 = unpack(0, qsel), unpack(1, ksel)
        per_head = []
        for g in range(ngroups):
            bias = biast_ref[p, g] if has_prev else biast_ref[p, g, blk:2 * blk, :]
            st = lax.dot_general(k2[g], stacked_q(q2, g), NT, preferred_element_type=F32) + bias
            mg = jnp.max(st, axis=0, keepdims=True)
            per_head += [mg[:, hh * blk:(hh + 1) * blk] for hh in range(DIL_GROUP)]
        rest = jnp.full((blk - DIL_HEADS, blk), -jnp.inf, F32)
        mt = jnp.concatenate(per_head + [rest], axis=0).T
        m_ref[qsel, :] = mt if p == 0 else jnp.maximum(m_ref[qsel, :], mt)

    def accumulate(p, d, start, has_prev):
        qsel = rows(start, blk, d)
        ksel = rows(start - d * blk, 2 * blk, d) if has_prev else qsel
        q2, k2, v2 = unpack(0, qsel), unpack(1, ksel), unpack(2, ksel)
        m_rows = m_ref[qsel, :].T
        per_head = []
        for g in range(ngroups):
            bias = biast_ref[p, g] if has_prev else biast_ref[p, g, blk:2 * blk, :]
            st = lax.dot_general(k2[g], stacked_q(q2, g), NT, preferred_element_type=F32) + bias
            m_g = jnp.concatenate([m_rows[g * DIL_GROUP + hh:g * DIL_GROUP + hh + 1, :] for hh in range(DIL_GROUP)],
                                  axis=1)
            pt = jnp.exp(st - m_g)
            l_g = jnp.sum(pt, axis=0, keepdims=True)
            per_head += [l_g[:, hh * blk:(hh + 1) * blk] for hh in range(DIL_GROUP)]
            out = lax.dot_general(pt.astype(BF16), v2[g], TN, preferred_element_type=F32)
            new = jnp.zeros((blk, DIL_GW), F32)
            for hh in range(DIL_GROUP):
                new = jnp.where(head == hh, out[hh * blk:(hh + 1) * blk], new)
            for j in range(aparts):
                part = new[:, j * LANE:(j + 1) * LANE]
                acc_ref[g * aparts + j, qsel, :] = part if p == 0 else acc_ref[g * aparts + j, qsel, :] + part
        rest = jnp.zeros((blk - DIL_HEADS, blk), F32)
        l_new = jnp.concatenate(per_head + [rest], axis=0).T
        l_ref[qsel, :] = l_new if p == 0 else l_ref[qsel, :] + l_new

    def sweep(block):
        for p, (_, d) in enumerate(DIL_PATTERNS):
            nblk = seq // (d * blk)

            def first_block(r, carry, p=p, d=d):
                block(p, d, r, False)
                return carry

            def later_blocks(r, carry, p=p, d=d, nblk=nblk):
                def one(ib, carry):
                    block(p, d, ib * (d * blk) + r, True)
                    return carry
                return lax.fori_loop(1, nblk, one, carry, unroll=2)

            lax.fori_loop(0, d, first_block, 0, unroll=2 if d > 1 else 1)
            if nblk > 1:
                lax.fori_loop(0, d, later_blocks, 0, unroll=2 if nblk == 2 else 1)

    sweep(find_max)
    sweep(accumulate)

    half = _iota_div((blk, LANE), 1, DIL_HD)

    def normalise(i, carry):
        sel = pl.ds(pl.multiple_of(i * blk, blk), blk)
        inv = 1.0 / l_ref[sel, :]
        for j in range(DIL_W // LANE):
            scale = jnp.where(half == 0, inv[:, 2 * j:2 * j + 1], inv[:, 2 * j + 1:2 * j + 2])
            o_ref[sel, j * LANE:(j + 1) * LANE] = (acc_ref[j, sel, :] * scale).astype(o_ref.dtype)
        return carry

    lax.fori_loop(0, seq // blk, normalise, 0)


def _dil_prompt(qkv, batch, seq):
    for w, d in DIL_PATTERNS:
        assert w // d == DIL_KEYS and seq % (d * DIL_KEYS) == 0
    biast = _dil_bias()
    return pl.pallas_call(
        _dil_prompt_kernel,
        grid=(batch,),
        in_specs=[pl.BlockSpec(memory_space=pl.ANY),
                  pl.BlockSpec(biast.shape, lambda b: (0, 0, 0, 0))],
        out_specs=pl.BlockSpec((seq, DIL_W), lambda b: (b, 0)),
        out_shape=jax.ShapeDtypeStruct((batch * seq, DIL_W), BF16),
        scratch_shapes=[pltpu.VMEM((qkv.shape[0], seq, LANE), I32),
                        pltpu.VMEM((DIL_W // LANE, seq, LANE), F32),
                        pltpu.VMEM((seq, LANE), F32), pltpu.VMEM((seq, LANE), F32),
                        pltpu.SemaphoreType.DMA(())],
        compiler_params=_params("arbitrary"),
        name="dil_prompt",
    )(qkv, biast)


def _dil_sample_bias(positions, past, seq):
    positions = np.asarray(positions)[None, :]
    dist = past + np.arange(seq)[:, None] - positions
    mult = np.zeros(dist.shape, np.float64)
    for w, d in DIL_PATTERNS:
        mult += (positions >= 0) & (dist >= 0) & (dist <= w) & (dist % d == 0)
    with np.errstate(divide="ignore"):
        logm = np.log(mult)
    return jnp.asarray(np.concatenate([-sl * dist + logm for sl in _alibi_slopes()], axis=0).astype(np.float32))


def _dil_sample_kernel(q_ref, kn_ref, vn_ref, kc_ref, vc_ref, bc_ref, bn_ref, o_ref, *, seq):
    rows = DIL_HEADS * seq
    q = q_ref[...] * (DIL_HD ** -0.5)
    qrep = jnp.concatenate([q] * DIL_HEADS, axis=0)
    own = _iota_div((rows, DIL_W), 0, seq) == _iota_div((rows, DIL_W), 1, DIL_HD)
    qbd = jnp.where(own, qrep, 0.0).astype(BF16)
    pad = jnp.zeros((LANE - seq, DIL_W), F32)
    ks = [kc_ref[...].astype(BF16), jnp.concatenate([kn_ref[...], pad], axis=0).astype(BF16)]
    vs = [vc_ref[...].astype(BF16), jnp.concatenate([vn_ref[...], pad], axis=0).astype(BF16)]
    ss = [lax.dot_general(qbd, k, NT, preferred_element_type=F32) + b[...] for k, b in zip(ks, (bc_ref, bn_ref))]
    m = functools.reduce(jnp.maximum, [jnp.max(s, axis=-1, keepdims=True) for s in ss])
    ps = [jnp.exp(s - m) for s in ss]
    l = sum(jnp.sum(p, axis=-1, keepdims=True) for p in ps)
    o = sum(jnp.dot(p.astype(BF16), v, preferred_element_type=F32) for p, v in zip(ps, vs)) * (1.0 / l)
    o = jnp.where(own, o, 0.0)
    res = o[0:seq]
    for h in range(1, DIL_HEADS):
        res = res + o[h * seq:(h + 1) * seq]
    o_ref[...] = res


def _dil_sample(pb, k_new, v_new, cache_k, cache_v, batch, seq):
    past = cache_k.shape[1]
    bias_c = _dil_sample_bias(np.arange(past), past, seq)
    bias_n = _dil_sample_bias(np.where(np.arange(LANE) < seq, past + np.arange(LANE), -1), past, seq)
    row = pl.BlockSpec((seq, DIL_W), lambda b: (b, 0))
    cache_spec = pl.BlockSpec((None, past, DIL_W), lambda b: (b, 0, 0))
    const = lambda a: pl.BlockSpec(a.shape, lambda b: (0, 0))
    return pl.pallas_call(
        functools.partial(_dil_sample_kernel, seq=seq),
        grid=(batch,),
        in_specs=[row, row, row, cache_spec, cache_spec, const(bias_c), const(bias_n)],
        out_specs=row,
        out_shape=jax.ShapeDtypeStruct((batch * seq, DIL_W), F32),
        compiler_params=_params("parallel"),
        name="dil_sample",
    )(pb, k_new, v_new, cache_k, cache_v, bias_c, bias_n)


def _out_proj_kernel(oa_ref, ob_ref, x_ref, wo_ref, g_ref, wq_ref, h_ref, q_ref, *, qscale):
    o = jnp.concatenate([oa_ref[...].astype(BF16), ob_ref[...].astype(BF16)], axis=1)
    h = x_ref[...] + jnp.dot(o, wo_ref[...], preferred_element_type=F32)
    h_ref[...] = h
    n = _rms(h, g_ref[...]).astype(BF16)
    q_ref[...] = (jnp.dot(n, wq_ref[...], preferred_element_type=F32) * qscale).astype(q_ref.dtype)


def _out_proj(oa, ob, x, wo, g, wq, q_dtype, tm):
    n, d = x.shape
    row = lambda i: (i, 0)
    const = lambda i: (0, 0)
    return pl.pallas_call(
        functools.partial(_out_proj_kernel, qscale=(d // MEM_HEADS) ** -0.5),
        grid=(n // tm,),
        in_specs=[pl.BlockSpec((tm, GLA_V), row), pl.BlockSpec((tm, DIL_W), row), pl.BlockSpec((tm, d), row),
                  pl.BlockSpec(wo.shape, const), pl.BlockSpec((1, d), const), pl.BlockSpec(wq.shape, const)],
        out_specs=[pl.BlockSpec((tm, d), row), pl.BlockSpec((tm, d), row)],
        out_shape=[jax.ShapeDtypeStruct((n, d), F32), jax.ShapeDtypeStruct((n, d), q_dtype)],
        compiler_params=_params("parallel"),
        name="out_proj",
    )(oa, ob, x, wo, g, wq)


def _cross_kernel(q_ref, mk_ref, mv_ref, o_ref):
    hd = q_ref.shape[1] // MEM_HEADS
    for h in range(MEM_HEADS):
        sl = slice(h * hd, (h + 1) * hd)
        s = lax.dot_general(q_ref[:, sl].astype(BF16), mk_ref[:, sl].astype(BF16), NT, preferred_element_type=F32)
        p = jnp.exp(s - jnp.max(s, axis=-1, keepdims=True))
        l = jnp.sum(p, axis=-1, keepdims=True)
        o = jnp.dot(p.astype(BF16), mv_ref[:, sl].astype(BF16), preferred_element_type=F32) * (1.0 / l)
        o_ref[:, sl] = o.astype(o_ref.dtype)


def _cross(q, mk, mv, batch, seq, tq, out_dtype):
    n, d = q.shape
    nt = seq // tq
    nm = mk.shape[1]
    return pl.pallas_call(
        _cross_kernel,
        grid=(batch, nt),
        in_specs=[pl.BlockSpec((tq, d), lambda b, i: (b * nt + i, 0)),
                  pl.BlockSpec((None, nm, d), lambda b, i: (b, 0, 0)),
                  pl.BlockSpec((None, nm, d), lambda b, i: (b, 0, 0))],
        out_specs=pl.BlockSpec((tq, d), lambda b, i: (b * nt + i, 0)),
        out_shape=jax.ShapeDtypeStruct((n, d), out_dtype),
        compiler_params=_params("parallel", "parallel"),
        name="cross_attn",
    )(q, mk, mv)


def _co_proj_kernel(o_ref, h_ref, wo_ref, g_ref, wrh_ref, wrl_ref, br_ref, h2_ref, n3_ref, lg_ref):
    h2 = h_ref[...] + jnp.dot(o_ref[...].astype(BF16), wo_ref[...], preferred_element_type=F32)
    h2_ref[...] = h2
    n3 = _rms(h2, g_ref[...])
    hi = n3.astype(BF16)
    lo = (n3 - hi.astype(F32)).astype(BF16)
    n3_ref[...] = hi
    lg_ref[...] = (jnp.dot(hi, wrh_ref[...], preferred_element_type=F32)
                   + jnp.dot(lo, wrh_ref[...], preferred_element_type=F32)
                   + jnp.dot(hi, wrl_ref[...], preferred_element_type=F32) + br_ref[...])


def _co_proj(o, h, wo, g, wrh, wrl, br, tm):
    n, d = h.shape
    row = lambda i: (i, 0)
    const = lambda i: (0, 0)
    return pl.pallas_call(
        _co_proj_kernel,
        grid=(n // tm,),
        in_specs=[pl.BlockSpec((tm, d), row), pl.BlockSpec((tm, d), row), pl.BlockSpec(wo.shape, const),
                  pl.BlockSpec((1, d), const), pl.BlockSpec(wrh.shape, const), pl.BlockSpec(wrl.shape, const),
                  pl.BlockSpec((1, LANE), const)],
        out_specs=[pl.BlockSpec((tm, d), row), pl.BlockSpec((tm, d), row), pl.BlockSpec((tm, LANE), row)],
        out_shape=[jax.ShapeDtypeStruct((n, d), F32), jax.ShapeDtypeStruct((n, d), BF16),
                   jax.ShapeDtypeStruct((n, LANE), F32)],
        compiler_params=_params("parallel"),
        name="co_proj",
    )(o, h, wo, g, wrh, wrl, br)


META_E, META_POS, META_W = 0, TOP_K, 2 * TOP_K


def _router_kernel(lg_ref, tri_ref, meta_ref, tile_ref, cnt_ref, run_ref, *, chunk):
    @pl.when(pl.program_id(0) == 0)
    def _():
        run_ref[...] = jnp.zeros_like(run_ref)

    lg = lg_ref[...]
    tm = lg.shape[0]
    lane = lax.broadcasted_iota(I32, (tm, LANE), 1)
    lane_f = lane.astype(F32)
    ninf = -jnp.inf
    first = lambda hit: jnp.min(jnp.where(hit, lane_f, float(LANE)), axis=-1, keepdims=True).astype(I32)

    gl = jnp.where(lane < N_GROUPS, lg, ninf)
    gmax = jnp.max(gl, axis=-1, keepdims=True)
    gidx = first(gl == gmax)
    pg = 1.0 / jnp.sum(jnp.exp(gl - gmax), axis=-1, keepdims=True)

    ex = lane - N_GROUPS
    el = jnp.where((ex >= 0) & (ex < N_EXPERTS) & ((ex >> int(math.log2(EXP_PER_GROUP))) == gidx), lg, ninf)
    t1 = jnp.max(el, axis=-1, keepdims=True)
    i1 = first(el == t1)
    el2 = jnp.where(lane == i1, ninf, el)
    t2 = jnp.max(el2, axis=-1, keepdims=True)
    i2 = first(el2 == t2)
    e = jnp.exp(t2 - t1)
    gates = (pg / (1.0 + e), pg * e / (1.0 + e))
    experts = (i1 - N_GROUPS, i2 - N_GROUPS)

    hits = [lane == ex_j for ex_j in experts]
    onehot = jnp.where(hits[0] | hits[1], 1.0, 0.0)
    before = jnp.dot(tri_ref[...], onehot.astype(BF16), preferred_element_type=F32)
    count = jnp.sum(onehot, axis=0, keepdims=True)
    slots = jnp.floor((count + (chunk - 1)) * (1.0 / chunk)) * chunk
    pos = [jnp.sum(jnp.where(lane < ex_j, slots, 0.0) + jnp.where(hit, before, 0.0), axis=-1, keepdims=True)
           for ex_j, hit in zip(experts, hits)]

    meta = jnp.zeros((tm, LANE), F32)
    for base, vals in ((META_E, [x.astype(F32) for x in experts]), (META_POS, pos), (META_W, gates)):
        for j, val in enumerate(vals):
            meta = jnp.where(lane == base + j, val, meta)
    meta_ref[...] = meta

    sub = lax.broadcasted_iota(I32, (SUBLANE, LANE), 0)
    tile_ref[...] = jnp.where(sub == 0, run_ref[...], jnp.where(sub == 1, slots, 0.0))
    run_ref[...] = run_ref[...] + slots
    cnt_ref[...] = run_ref[...]


def _router(logits, tm, chunk):
    n = logits.shape[0]
    tri = jnp.asarray(np.tril(np.ones((tm, tm), np.float32), -1), BF16)
    return pl.pallas_call(
        functools.partial(_router_kernel, chunk=chunk),
        grid=(n // tm,),
        in_specs=[pl.BlockSpec((tm, LANE), lambda i: (i, 0)), pl.BlockSpec((tm, tm), lambda i: (0, 0))],
        out_specs=[pl.BlockSpec((tm, LANE), lambda i: (i, 0)),
                   pl.BlockSpec((SUBLANE, LANE), lambda i: (i, 0)),
                   pl.BlockSpec((SUBLANE, LANE), lambda i: (0, 0))],
        out_shape=[jax.ShapeDtypeStruct((n, LANE), F32),
                   jax.ShapeDtypeStruct((n // tm * SUBLANE, LANE), F32),
                   jax.ShapeDtypeStruct((SUBLANE, LANE), F32)],
        scratch_shapes=[pltpu.VMEM((SUBLANE, LANE), F32)],
        compiler_params=_params("arbitrary"),
        name="router",
    )(logits, tri)


TAB_TOTAL = LANE - 1
FILL_BASE, FILL_N, FILL_TAIL = 0, N_EXPERTS, 2 * N_EXPERTS
FILL_ROWS = 64


def _one_hot(meta, lane_pos, slot):
    return jnp.where(lane_pos == meta[:, META_POS + slot:META_POS + slot + 1], 1.0, 0.0).astype(BF16)


def _chunk_copy(sorted_ref, hbm_ref, sem, chunk, to_hbm, off, base):
    src, dst = sorted_ref.at[pl.ds(off, chunk), :], hbm_ref.at[pl.ds(base, chunk), :]
    return pltpu.make_async_copy(src, dst, sem) if to_hbm else pltpu.make_async_copy(dst, src, sem)


def _start_chunks(tab_ref, sorted_ref, hbm_ref, sem, chunk, to_hbm):
    def one(c, carry):
        _chunk_copy(sorted_ref, hbm_ref, sem, chunk, to_hbm, pl.multiple_of(c * chunk, chunk),
                    pl.multiple_of(tab_ref[0, c], chunk)).start()
        return carry

    lax.fori_loop(0, tab_ref[0, TAB_TOTAL], one, 0)


def _wait_chunks(tab_ref, sorted_ref, hbm_ref, sem, chunk, to_hbm):
    def drain(c, carry):
        _chunk_copy(sorted_ref, hbm_ref, sem, chunk, to_hbm, 0, 0).wait()
        return carry

    lax.fori_loop(0, tab_ref[0, TAB_TOTAL], drain, 0)


def _dispatch_kernel(tab_ref, prev_ref, fill_ref, x_ref, meta_ref, xs_ref, sorted_ref, zero_ref, sems, *, chunk):
    i = pl.program_id(0)
    slot = lax.rem(i, 2)
    tm, npos = x_ref.shape[0], sorted_ref.shape[1]
    meta = meta_ref[...]
    lane_pos = lax.broadcasted_iota(I32, (tm, npos), 1).astype(F32)
    place = _one_hot(meta, lane_pos, 0) + _one_hot(meta, lane_pos, 1)
    sorted_ref[slot] = lax.dot_general(place, x_ref[...], TN, preferred_element_type=F32)

    @pl.when(i > 0)
    def _():
        _wait_chunks(prev_ref, sorted_ref.at[1 - slot], xs_ref, sems.at[1 - slot], chunk, True)

    _start_chunks(tab_ref, sorted_ref.at[slot], xs_ref, sems.at[slot], chunk, True)

    @pl.when(i == pl.num_programs(0) - 1)
    def _():
        _wait_chunks(tab_ref, sorted_ref.at[slot], xs_ref, sems.at[slot], chunk, True)
        zero_ref[...] = jnp.zeros_like(zero_ref)
        big = zero_ref.shape[0]
        sem = sems.at[0]

        def fill(row, size):
            return pltpu.make_async_copy(zero_ref.at[pl.ds(0, size), :],
                                         xs_ref.at[pl.ds(pl.multiple_of(row, size), size), :], sem)

        def per_expert(e, total):
            def one(c, carry):
                fill(fill_ref[0, FILL_BASE + e] + c * chunk, chunk).start()
                return carry
            lax.fori_loop(0, fill_ref[0, FILL_N + e], one, 0)
            return total + fill_ref[0, FILL_N + e]

        def drain(c, carry):
            fill(0, chunk).wait()
            return carry

        lax.fori_loop(0, lax.fori_loop(0, N_EXPERTS, per_expert, 0), drain, 0)

        def tail(c, carry):
            fill(fill_ref[0, FILL_TAIL] + c * big, big).start()
            return carry

        def drain_tail(c, carry):
            fill(0, big).wait()
            return carry

        lax.fori_loop(0, fill_ref[0, FILL_TAIL + 1], tail, 0)
        lax.fori_loop(0, fill_ref[0, FILL_TAIL + 1], drain_tail, 0)


def _dispatch(x, meta, tab, fill, rows, tm, chunk):
    n, d = x.shape
    npos = TOP_K * tm + N_EXPERTS * chunk
    return pl.pallas_call(
        functools.partial(_dispatch_kernel, chunk=chunk),
        grid=(n // tm,),
        in_specs=[pl.BlockSpec((None, 1, LANE), lambda i: (i, 0, 0), memory_space=pltpu.SMEM),
                  pl.BlockSpec((None, 1, LANE), lambda i: (jnp.maximum(i - 1, 0), 0, 0), memory_space=pltpu.SMEM),
                  pl.BlockSpec((1, LANE), lambda i: (0, 0), memory_space=pltpu.SMEM),
                  pl.BlockSpec((tm, d), lambda i: (i, 0)),
                  pl.BlockSpec((tm, LANE), lambda i: (i, 0))],
        out_specs=pl.BlockSpec(memory_space=pl.ANY),
        out_shape=jax.ShapeDtypeStruct((rows, d), F32),
        scratch_shapes=[pltpu.VMEM((2, npos, d), F32), pltpu.VMEM((FILL_ROWS, d), F32),
                        pltpu.SemaphoreType.DMA((2,))],
        compiler_params=_params("arbitrary"),
        name="moe_dispatch",
    )(tab, tab, fill, x, meta)


def _expert_kernel(be_ref, nu_ref, x_ref, w1_ref, w3_ref, w2_ref, y_ref):
    del be_ref
    live = pl.program_id(0) < nu_ref[0]

    @pl.when(live)
    def _():
        x = x_ref[...].astype(BF16)
        a = jnp.dot(x, w1_ref[...], preferred_element_type=F32)
        b = jnp.dot(x, w3_ref[...], preferred_element_type=F32)
        y_ref[...] = jnp.dot((_silu(a) * b).astype(BF16), w2_ref[...], preferred_element_type=F32)

    @pl.when(jnp.logical_not(live))
    def _():
        y_ref[...] = jnp.zeros_like(y_ref)


def _experts(xs, block_e, n_used, w1, w3, w2, bm):
    rows, d = xs.shape
    de = w1.shape[2]
    return pl.pallas_call(
        _expert_kernel,
        grid_spec=pltpu.PrefetchScalarGridSpec(
            num_scalar_prefetch=2,
            grid=(rows // bm,),
            in_specs=[pl.BlockSpec((bm, d), lambda i, be, nu: (i, 0)),
                      pl.BlockSpec((None, d, de), lambda i, be, nu: (be[i], 0, 0)),
                      pl.BlockSpec((None, d, de), lambda i, be, nu: (be[i], 0, 0)),
                      pl.BlockSpec((None, de, d), lambda i, be, nu: (be[i], 0, 0))],
            out_specs=pl.BlockSpec((bm, d), lambda i, be, nu: (i, 0))),
        out_shape=jax.ShapeDtypeStruct((rows, d), F32),
        compiler_params=_params("arbitrary"),
        name="moe_experts",
    )(block_e, n_used, xs, w1, w3, w2)


def _combine_kernel(tab_ref, next_ref, h_ref, meta_ref, g_ref, yb_ref, y_ref, sorted_ref, sems, *, chunk):
    i = pl.program_id(0)
    slot = lax.rem(i, 2)

    @pl.when(i == 0)
    def _():
        sorted_ref[...] = jnp.zeros_like(sorted_ref)
        _start_chunks(tab_ref, sorted_ref.at[0], yb_ref, sems.at[0], chunk, False)

    @pl.when(i + 1 < pl.num_programs(0))
    def _():
        _start_chunks(next_ref, sorted_ref.at[1 - slot], yb_ref, sems.at[1 - slot], chunk, False)

    _wait_chunks(tab_ref, sorted_ref.at[slot], yb_ref, sems.at[slot], chunk, False)
    tm, npos = h_ref.shape[0], sorted_ref.shape[1]
    meta = meta_ref[...]
    lane_pos = lax.broadcasted_iota(I32, (tm, npos), 1).astype(F32)
    yb = sorted_ref[slot].astype(BF16)
    moe = (meta[:, META_W:META_W + 1] * jnp.dot(_one_hot(meta, lane_pos, 0), yb, preferred_element_type=F32)
           + meta[:, META_W + 1:META_W + 2] * jnp.dot(_one_hot(meta, lane_pos, 1), yb, preferred_element_type=F32))
    y_ref[...] = _rms(h_ref[...] + moe, g_ref[...])


def _combine(h, meta, tab, yb, g, tm, chunk):
    n, d = h.shape
    npos = TOP_K * tm + N_EXPERTS * chunk
    return pl.pallas_call(
        functools.partial(_combine_kernel, chunk=chunk),
        grid=(n // tm,),
        in_specs=[pl.BlockSpec((None, 1, LANE), lambda i: (i, 0, 0), memory_space=pltpu.SMEM),
                  pl.BlockSpec((None, 1, LANE), lambda i: (jnp.minimum(i + 1, n // tm - 1), 0, 0),
                               memory_space=pltpu.SMEM),
                  pl.BlockSpec((tm, d), lambda i: (i, 0)),
                  pl.BlockSpec((tm, LANE), lambda i: (i, 0)),
                  pl.BlockSpec((1, d), lambda i: (0, 0)),
                  pl.BlockSpec(memory_space=pl.ANY)],
        out_specs=pl.BlockSpec((tm, d), lambda i: (i, 0)),
        out_shape=jax.ShapeDtypeStruct((n, d), F32),
        scratch_shapes=[pltpu.VMEM((2, npos, d), F32), pltpu.SemaphoreType.DMA((2,))],
        compiler_params=_params("arbitrary"),
        name="moe_combine",
    )(tab, tab, h, meta, g, yb)


def _moe_final(h2, n3, logits, w1, w3, w2, g_final, tm, bm, chunk):
    n = h2.shape[0]
    nt = n // tm
    max_chunks = (TOP_K * tm + N_EXPERTS * chunk) // chunk
    assert max_chunks <= TAB_TOTAL and bm % chunk == 0
    meta, tiles, cnt = _router(logits, tm, chunk)
    rows_e = cnt[0, :N_EXPERTS].astype(I32)
    padded = (rows_e + bm - 1) // bm * bm
    pad_end = jnp.cumsum(padded)
    pad_start = pad_end - padded
    tiles = tiles.reshape(nt, SUBLANE, LANE)
    run_before = tiles[:, 0, :N_EXPERTS].astype(I32)
    nchunks = tiles[:, 1, :N_EXPERTS].astype(I32) // chunk
    last_chunk = jnp.cumsum(nchunks, axis=1)
    c = jnp.arange(max_chunks, dtype=I32)[None, :]
    e_of_c = jnp.minimum(jnp.sum((last_chunk[:, None, :] <= c[:, :, None]).astype(I32), axis=2), N_EXPERTS - 1)
    is_e = e_of_c[:, :, None] == jnp.arange(N_EXPERTS, dtype=I32)[None, None, :]
    pick = lambda a: jnp.sum(jnp.where(is_e, a[:, None, :], 0), axis=2)
    chunk_row = pick(pad_start[None, :] + run_before) + (c - pick(last_chunk - nchunks)) * chunk
    pad_lanes = lambda a: jnp.pad(a, ((0, 0), (0, LANE - a.shape[1])))
    tab = jnp.concatenate([pad_lanes(chunk_row)[:, :TAB_TOTAL], last_chunk[:, -1:]], axis=1).reshape(nt, 1, LANE)
    max_rows = n * TOP_K + (chunk - 1) * min(n * TOP_K, nt * N_EXPERTS) + N_EXPERTS * (bm - 1)
    nb = -(-max_rows // bm)
    assert bm % FILL_ROWS == 0
    fill = pad_lanes(jnp.concatenate([pad_start + rows_e, (padded - rows_e) // chunk,
                                      pad_end[-1:], (nb * bm - pad_end[-1:]) // FILL_ROWS])[None, :])
    block_start = jnp.arange(nb, dtype=I32) * bm
    block_e = jnp.minimum(jnp.sum((pad_end[None, :] <= block_start[:, None]).astype(I32), axis=1), N_EXPERTS - 1)
    n_used = (pad_end[-1:] // bm).astype(I32)
    xs = _dispatch(n3, meta, tab, fill, nb * bm, tm, chunk)
    yb = _experts(xs, block_e, n_used, w1, w3, w2, bm)
    return _combine(h2, meta, tab, yb, g_final, tm, chunk)


def _group(x3, mk, mv, w, keep, *, cache=None, state=None):
    batch, seq, d = x3.shape
    n = batch * seq
    x = x3.reshape(n, d)
    tm = _tile(n, TOKEN_TILE)
    sample = cache is not None
    pa, pb, k_new, v_new = _norm_proj(x, w["g1"], w["w_in"], batch, seq, keep, not sample, tm)
    if sample:
        assert keep == seq
        oa, st = _gla_sample(pa, _state_to_blockdiag_t(state), w["wgk"], w["bgk"], w["gg"], batch, seq,
                             rows=GLA_CHUNK)
        ob = _dil_sample(pb, k_new, v_new, cache[0].reshape(batch, -1, DIL_W), cache[1].reshape(batch, -1, DIL_W),
                         batch, seq)
    else:
        zeros = jnp.zeros((batch, GLA_V, GLA_QK), F32)
        oa, st = _gla_prompt(pa, zeros, w["wgk"], w["bgk"], w["gg"], batch, seq, tb=_tile(seq, TOKEN_TILE))
        ob = _dil_prompt(pb, batch, seq)
    h1, qc = _out_proj(oa, ob, x, w["w_out"], w["g2"], w["w_cq"], F32 if sample else BF16, tm)
    oc = _cross(qc, mk, mv, batch, seq, _tile(seq, TOKEN_TILE), F32 if sample else BF16)
    h2, n3, logits = _co_proj(oc, h1, w["w_co"], w["g3"], w["wr_hi"], w["wr_lo"], w["br"], tm)
    y = _moe_final(h2, n3, logits, w["w_e1"], w["w_e3"], w["w_e2"], w["g_final"], _tile(n, MOE_TILE), MOE_BLOCK,
                   MOE_CHUNK)
    return (y.reshape(batch, seq, d), _blockdiag_t_to_state(st),
            k_new.reshape(batch, keep, DIL_HEADS, DIL_HD), v_new.reshape(batch, keep, DIL_HEADS, DIL_HD))


def _pack_weights(l, g_norm1, w_in, w_gk2, b_gk, g_gla_out, w_out, g_norm2, w_cq, w_co, g_norm3, w_gr, b_gr,
                  w_er, b_er, w_e1, w_e3, w_e2, g_final):
    d = w_in.shape[1]
    sp = np.cumsum([GLA_QK, GLA_QK, GLA_V, GATE_RANK, GLA_V, DIL_W, DIL_W, DIL_W])
    wi = w_in[l]
    q_a, k_a, v_a, glr, r_a, q_b, k_b, v_b = (wi[:, a:b] for a, b in zip([0, *sp[:-1]], sp))
    glr = jnp.pad(glr, ((0, 0), (0, LANE - GATE_RANK)))
    wr = jnp.pad(jnp.concatenate([w_gr[l], w_er[l]], axis=1), ((0, 0), (0, LANE - N_GROUPS - N_EXPERTS)))
    wr_hi = wr.astype(BF16)
    return dict(
        g1=g_norm1[l].reshape(1, d), g2=g_norm2[l].reshape(1, d), g3=g_norm3[l].reshape(1, d),
        g_final=g_final.reshape(1, d),
        w_in=jnp.concatenate([q_a, k_a, v_a, r_a, glr, q_b, k_b, v_b], axis=1).astype(BF16),
        wgk=jnp.pad(w_gk2[l], ((0, LANE - GATE_RANK), (0, 0))).astype(BF16),
        bgk=b_gk[l].reshape(1, GLA_QK), gg=g_gla_out[l].reshape(1, GLA_DV),
        w_out=w_out[l].astype(BF16), w_cq=w_cq[l].astype(BF16), w_co=w_co[l].astype(BF16),
        wr_hi=wr_hi, wr_lo=(wr - wr_hi.astype(F32)).astype(BF16),
        br=jnp.pad(jnp.concatenate([b_gr[l], b_er[l]]), (0, LANE - N_GROUPS - N_EXPERTS)).reshape(1, LANE),
        w_e1=w_e1[l].astype(BF16), w_e3=w_e3[l].astype(BF16), w_e2=w_e2[l].astype(BF16))


def kernel(x_prompt, x_sample, cache_swa_k, cache_swa_v, state_gla, cache_mem_k, cache_mem_v, mem_prompt,
           g_norm1, w_in, w_gk2, b_gk, g_gla_out, w_out, g_norm2, g_mem, w_cq, w_mk, w_mv, w_co,
           g_norm3, w_gr, b_gr, w_er, b_er, w_e1, w_e3, w_e2, g_final):
    depth = w_in.shape[0]
    assert depth == 1, "the final norm is fused into the last MoE stage; stacked layers are not supported"
    batch, seq, d = x_prompt.shape
    sb, sseq, _ = x_sample.shape
    nm = mem_prompt.shape[1]
    keep = min(DIL_PATTERNS[-1][0], seq)
    l = 0
    w = _pack_weights(l, g_norm1, w_in, w_gk2, b_gk, g_gla_out, w_out, g_norm2, w_cq, w_co, g_norm3,
                      w_gr, b_gr, w_er, b_er, w_e1, w_e3, w_e2, g_final)
    mkv = _norm_matmul(mem_prompt.reshape(batch * nm, d), g_mem[l].reshape(1, d),
                       jnp.concatenate([w_mk[l], w_mv[l]], axis=1).astype(BF16), _tile(batch * nm, TOKEN_TILE))
    mk = mkv[:, :d].reshape(batch, nm, d)
    mv = mkv[:, d:].reshape(batch, nm, d)
    yp, sp, kp, vp = _group(x_prompt, mk, mv, w, keep)
    past = cache_swa_k.shape[2]
    ys, ss, kn, vn = _group(x_sample, cache_mem_k.reshape(sb, nm, d), cache_mem_v.reshape(sb, nm, d), w, sseq,
                            cache=(cache_swa_k.reshape(sb, past, DIL_W), cache_swa_v.reshape(sb, past, DIL_W)),
                            state=state_gla[l])
    heads = lambda m: m.reshape(batch, nm, MEM_HEADS, d // MEM_HEADS)
    return (yp, ys, *(o[None] for o in (kp, vp, sp, heads(mk), heads(mv), kn, vn, ss)))
```

```python
import functools
import math

import numpy as np
import jax
import jax.numpy as jnp
from jax import lax
from jax.experimental import pallas as pl
from jax.experimental.pallas import tpu as pltpu

F32, BF16, I32 = jnp.float32, jnp.bfloat16, jnp.int32

GLA_HEADS, GLA_DK, GLA_DV = 4, 64, 128
GLA_QK, GLA_V = GLA_HEADS * GLA_DK, GLA_HEADS * GLA_DV
GATE_RANK, GATE_NORM, GLA_CHUNK = 16, 16.0, 64
DIL_HEADS, DIL_HD = 8, 64
DIL_W = DIL_HEADS * DIL_HD
DIL_PATTERNS = ((128, 1), (512, 4), (2048, 16))
DIL_KEYS = 128
MEM_HEADS = 4
N_GROUPS, EXP_PER_GROUP, TOP_K = 4, 8, 2
N_EXPERTS = N_GROUPS * EXP_PER_GROUP
EPS = 1e-6

LANE = 128
SUBLANE = 8
VMEM_LIMIT_BYTES = 56 * 1024 * 1024

PA_W = 2 * GLA_QK + 2 * GLA_V + LANE
PB_W = 3 * DIL_W
HALF_W = DIL_W // 2
HI_MASK = -65536

TOKEN_TILE = 512
MOE_TILE = 256
MOE_BLOCK = 512
MOE_CHUNK = SUBLANE

NT = (((1,), (1,)), ((), ()))
TN = (((0,), (0,)), ((), ()))


def _params(*sem):
    return pltpu.CompilerParams(dimension_semantics=sem, vmem_limit_bytes=VMEM_LIMIT_BYTES)


def _tile(n, want):
    t = min(n, want)
    assert n % t == 0
    return t


def _rms(x, g):
    y = x * lax.rsqrt(jnp.mean(x * x, axis=-1, keepdims=True) + EPS)
    return y * g


def _silu(x):
    return x / (1.0 + jnp.exp(-x))


def _iota_div(shape, dim, n):
    assert n & (n - 1) == 0
    return lax.broadcasted_iota(I32, shape, dim) >> int(math.log2(n))


def _pack_halves(r):
    u = lax.bitcast_convert_type(r.astype(BF16).astype(F32), I32)
    lo = u[:, :HALF_W]
    return lax.shift_right_logical(lo, jnp.full_like(lo, 16)) | (u[:, HALF_W:] & HI_MASK)


def _unpack_halves(w):
    lo = lax.bitcast_convert_type(w << 16, F32).astype(BF16)
    hi = lax.bitcast_convert_type(w & HI_MASK, F32).astype(BF16)
    return lo, hi


def _norm_proj_kernel(x_ref, g_ref, w_ref, pa_ref, pb_ref, k_ref, v_ref, *, pack):
    n = _rms(x_ref[...], g_ref[...]).astype(BF16)
    for c0 in range(0, PA_W, DIL_W):
        c1 = min(c0 + DIL_W, PA_W)
        pa_ref[:, c0:c1] = jnp.dot(n, w_ref[:, c0:c1], preferred_element_type=F32).astype(pa_ref.dtype)
    for j, kv_ref in enumerate((None, k_ref, v_ref)):
        r = jnp.dot(n, w_ref[:, PA_W + j * DIL_W:PA_W + (j + 1) * DIL_W], preferred_element_type=F32)
        if pack:
            words = _pack_halves(r)
            for part in range(HALF_W // LANE):
                pb_ref[j * (HALF_W // LANE) + part] = words[:, part * LANE:(part + 1) * LANE]
        else:
            pb_ref[:, j * DIL_W:(j + 1) * DIL_W] = r
        if kv_ref is not None:
            kv_ref[...] = r


def _norm_proj(x, g, w, batch, seq, keep, pack, tm):
    n, d = x.shape
    if keep == seq:
        kv_map = lambda i: (i, 0)
    else:
        nt, nk = seq // tm, keep // tm
        assert keep % tm == 0
        kv_map = lambda i: ((i // nt) * nk + jnp.maximum(i % nt - (nt - nk), 0), 0)
    if pack:
        nparts = 3 * HALF_W // LANE
        pb_spec = pl.BlockSpec((nparts, tm, LANE), lambda i: (0, i, 0))
        pb_shape = jax.ShapeDtypeStruct((nparts, n, LANE), I32)
    else:
        pb_spec = pl.BlockSpec((tm, PB_W), lambda i: (i, 0))
        pb_shape = jax.ShapeDtypeStruct((n, PB_W), F32)
    pa_dtype = BF16 if pack else F32
    return pl.pallas_call(
        functools.partial(_norm_proj_kernel, pack=pack),
        grid=(n // tm,),
        in_specs=[pl.BlockSpec((tm, d), lambda i: (i, 0)),
                  pl.BlockSpec((1, d), lambda i: (0, 0)),
                  pl.BlockSpec((d, PA_W + PB_W), lambda i: (0, 0))],
        out_specs=[pl.BlockSpec((tm, PA_W), lambda i: (i, 0)),
                   pb_spec,
                   pl.BlockSpec((tm, DIL_W), kv_map),
                   pl.BlockSpec((tm, DIL_W), kv_map)],
        out_shape=[jax.ShapeDtypeStruct((n, PA_W), pa_dtype),
                   pb_shape,
                   jax.ShapeDtypeStruct((batch * keep, DIL_W), F32),
                   jax.ShapeDtypeStruct((batch * keep, DIL_W), F32)],
        compiler_params=_params("arbitrary"),
        name="norm_proj",
    )(x, g, w)


def _norm_matmul_kernel(x_ref, g_ref, w_ref, o_ref):
    n = _rms(x_ref[...], g_ref[...]).astype(BF16)
    for c0 in range(0, o_ref.shape[1], DIL_W):
        o_ref[:, c0:c0 + DIL_W] = jnp.dot(n, w_ref[:, c0:c0 + DIL_W], preferred_element_type=F32)


def _norm_matmul(x, g, w, tm):
    n, d = x.shape
    m = w.shape[1]
    return pl.pallas_call(
        _norm_matmul_kernel,
        grid=(n // tm,),
        in_specs=[pl.BlockSpec((tm, d), lambda i: (i, 0)),
                  pl.BlockSpec((1, d), lambda i: (0, 0)),
                  pl.BlockSpec((d, m), lambda i: (0, 0))],
        out_specs=pl.BlockSpec((tm, m), lambda i: (i, 0)),
        out_shape=jax.ShapeDtypeStruct((n, m), F32),
        compiler_params=_params("parallel"),
        name="norm_matmul",
    )(x, g, w)


def _gla_tables(chunk, seg):
    idx = np.arange(chunk)
    tril = ((idx[None, :] <= idx[:, None]) & (idx[None, :] // seg == idx[:, None] // seg)).astype(np.float32)
    masks, levels = [], []
    s = seg // 2
    while s >= 1:
        same = (idx[:, None] // (2 * s)) == (idx[None, :] // (2 * s))
        masks.append(same & ((idx[:, None] // s) % 2 == 1) & ((idx[None, :] // s) % 2 == 0))
        levels.append(s)
        s //= 2
    masks.append(idx[:, None] == idx[None, :])
    pm = np.tile(np.stack(masks).astype(np.float32), (1, 1, GLA_HEADS))
    return jnp.asarray(tril, BF16), jnp.asarray(pm), tuple(levels)


def _block_row(b, blk, idx):
    c, w = b.shape
    b3 = b.reshape(c // blk, blk, w)
    return jnp.broadcast_to(b3[:, idx:idx + 1, :], (c // blk, blk, w)).reshape(c, w)


def _level_ref(b, s, row):
    c = b.shape[0]
    if 2 * s >= SUBLANE:
        return _block_row(b, 2 * s, s - 1)
    down = lambda n: pltpu.roll(b, n, 0)
    if s == 2:
        m = row & 3
        return jnp.where(m == 0, pltpu.roll(b, c - 1, 0), jnp.where(m == 1, b, jnp.where(m == 2, down(1), down(2))))
    assert s == 1
    return jnp.where((row & 1) == 1, down(1), b)


def _gla_chunk(q, k, v, glr, wgk, bgk, tril_ref, pm_ref, levels, seg):
    c = q.shape[0]
    nl = len(levels)
    gk = jnp.dot(glr, wgk, preferred_element_type=F32) + bgk
    la = (jnp.minimum(gk, 0.0) - jnp.log(1.0 + jnp.exp(-jnp.abs(gk)))) * (1.0 / GATE_NORM)
    hi = la.astype(BF16)
    r1 = la - hi.astype(F32)
    mid = r1.astype(BF16)
    lo = (r1 - mid.astype(F32)).astype(BF16)
    b3 = jnp.dot(tril_ref[...], jnp.concatenate([hi, mid, lo], axis=1), preferred_element_type=F32)
    b = b3[:, :GLA_QK] + b3[:, GLA_QK:2 * GLA_QK] + b3[:, 2 * GLA_QK:]
    b_end = _block_row(b, seg, seg - 1)

    row = lax.broadcasted_iota(I32, (c, GLA_QK), 0)
    khead = _iota_div((c, GLA_QK), 1, GLA_DK)
    vhead = _iota_div((c, GLA_V), 1, GLA_DV)

    def by_head(x, head):
        return jnp.concatenate([jnp.where(head == h, x, jnp.zeros_like(x)) for h in range(GLA_HEADS)], axis=0)

    def level(qe, ke, pm):
        a = lax.dot_general(qe, by_head(ke, khead), NT, preferred_element_type=F32)
        return jnp.where(pm > 0.0, a, 0.0)

    acc = level(q.astype(BF16), k.astype(BF16), pm_ref[nl])
    for l, s in enumerate(levels):
        ref = _level_ref(b, s, row)
        right = ((row >> int(math.log2(s))) & 1) == 1
        qe = (q * jnp.exp(jnp.where(right, b - ref, 0.0))).astype(BF16)
        ke = (k * jnp.exp(jnp.where(right, 0.0, ref - b))).astype(BF16)
        acc = acc + level(qe, ke, pm_ref[l])

    o = jnp.dot(acc.astype(BF16), by_head(v, vhead), preferred_element_type=F32)
    qb = (q * jnp.exp(b)).astype(BF16)
    kd = (k * jnp.exp(b_end - b)).astype(BF16)
    return o, qb, kd, b_end


def _gla_finish(o, r, gg):
    outs = []
    for h in range(GLA_HEADS):
        sl = slice(h * GLA_DV, (h + 1) * GLA_DV)
        outs.append(_rms(o[:, sl], gg) * _silu(r[:, sl]))
    return jnp.concatenate(outs, axis=1)


def _state_mask():
    return _iota_div((GLA_V, GLA_QK), 0, GLA_DV) == _iota_div((GLA_V, GLA_QK), 1, GLA_DK)


def _gla_prompt_kernel(q_ref, k_ref, v_ref, r_ref, glr_ref, wgk_ref, bgk_ref, gg_ref, tril_ref, pm_ref, s0_ref,
                       o_ref, sout_ref, st_ref, *, chunk, levels):
    i = pl.program_id(1)

    @pl.when(i == 0)
    def _():
        st_ref[...] = s0_ref[...]

    smask = _state_mask()

    def body(c, carry):
        rows = pl.ds(pl.multiple_of(c * chunk, chunk), chunk)
        q = q_ref[rows, :].astype(F32) * (GLA_DK ** -0.5)
        k = k_ref[rows, :].astype(F32)
        v = v_ref[rows, :]
        o, qb, kd, b_end = _gla_chunk(q, k, v, glr_ref[rows, :], wgk_ref[...], bgk_ref[...],
                                      tril_ref, pm_ref, levels, chunk)
        st = st_ref[...]
        o = o + lax.dot_general(qb, st.astype(BF16), NT, preferred_element_type=F32)
        u = lax.dot_general(v, kd, TN, preferred_element_type=F32)
        st_ref[...] = st * jnp.exp(b_end[0:1, :]) + jnp.where(smask, u, 0.0)
        o_ref[rows, :] = _gla_finish(o, r_ref[rows, :].astype(F32), gg_ref[...]).astype(o_ref.dtype)
        return carry

    lax.fori_loop(0, q_ref.shape[0] // chunk, body, 0, unroll=4)

    @pl.when(i == pl.num_programs(1) - 1)
    def _():
        sout_ref[...] = st_ref[...]


def _gla_prompt(pa, s0t, wgk, bgk, gg, batch, seq, tb):
    chunk = math.gcd(seq, GLA_CHUNK)
    tril, pm, levels = _gla_tables(chunk, chunk)
    nt = seq // tb
    row = lambda b, i: (b * nt + i, 0)
    const2 = lambda b, i: (0, 0)
    return pl.pallas_call(
        functools.partial(_gla_prompt_kernel, chunk=chunk, levels=levels),
        grid=(batch, nt),
        in_specs=[pl.BlockSpec((tb, GLA_QK), row),
                  pl.BlockSpec((tb, GLA_QK), lambda b, i: (b * nt + i, 1)),
                  pl.BlockSpec((tb, GLA_V), lambda b, i: (b * nt + i, 1)),
                  pl.BlockSpec((tb, GLA_V), lambda b, i: (b * nt + i, 2)),
                  pl.BlockSpec((tb, LANE), lambda b, i: (b * nt + i, (PA_W - LANE) // LANE)),
                  pl.BlockSpec(wgk.shape, const2),
                  pl.BlockSpec(bgk.shape, const2),
                  pl.BlockSpec(gg.shape, const2),
                  pl.BlockSpec(tril.shape, const2),
                  pl.BlockSpec(pm.shape, lambda b, i: (0, 0, 0)),
                  pl.BlockSpec((None, GLA_V, GLA_QK), lambda b, i: (b, 0, 0))],
        out_specs=[pl.BlockSpec((tb, GLA_V), row),
                   pl.BlockSpec((None, GLA_V, GLA_QK), lambda b, i: (b, 0, 0))],
        out_shape=[jax.ShapeDtypeStruct((batch * seq, GLA_V), BF16),
                   jax.ShapeDtypeStruct((batch, GLA_V, GLA_QK), F32)],
        scratch_shapes=[pltpu.VMEM((GLA_V, GLA_QK), F32)],
        compiler_params=_params("parallel", "arbitrary"),
        name="gla_prompt",
    )(pa, pa, pa, pa, pa, wgk, bgk, gg, tril, pm, s0t)


def _gla_sample_kernel(q_ref, k_ref, v_ref, r_ref, glr_ref, wgk_ref, bgk_ref, gg_ref, tril_ref, pm_ref, s0_ref,
                       o_ref, sout_ref, *, seg, levels):
    rows = q_ref.shape[0]
    q = q_ref[...] * (GLA_DK ** -0.5)
    v = v_ref[...].astype(BF16)
    o, qb, kd, b_end = _gla_chunk(q, k_ref[...], v, glr_ref[...].astype(BF16), wgk_ref[...], bgk_ref[...],
                                  tril_ref, pm_ref, levels, seg)
    smask = _state_mask()
    seq_o = _iota_div((rows, GLA_V), 0, seg)
    for j in range(rows // seg):
        st = s0_ref[j]
        oj = lax.dot_general(qb, st.astype(BF16), NT, preferred_element_type=F32)
        o = o + jnp.where(seq_o == j, oj, 0.0)
        u = lax.dot_general(jnp.where(seq_o == j, v, jnp.zeros_like(v)), kd, TN, preferred_element_type=F32)
        sout_ref[j] = st * jnp.exp(b_end[j * seg:j * seg + 1, :]) + jnp.where(smask, u, 0.0)
    o_ref[...] = _gla_finish(o, r_ref[...], gg_ref[...])


def _gla_sample(pa, s0t, wgk, bgk, gg, batch, seq, rows):
    seg = math.gcd(seq, GLA_CHUNK)
    assert seg == seq and seg % SUBLANE == 0, "sample sequences must be one sublane-aligned chunk"
    tril, pm, levels = _gla_tables(rows, seg)
    nseq = rows // seg
    row = lambda i: (i, 0)
    const2 = lambda i: (0, 0)
    return pl.pallas_call(
        functools.partial(_gla_sample_kernel, seg=seg, levels=levels),
        grid=(batch * seq // rows,),
        in_specs=[pl.BlockSpec((rows, GLA_QK), row),
                  pl.BlockSpec((rows, GLA_QK), lambda i: (i, 1)),
                  pl.BlockSpec((rows, GLA_V), lambda i: (i, 1)),
                  pl.BlockSpec((rows, GLA_V), lambda i: (i, 2)),
                  pl.BlockSpec((rows, LANE), lambda i: (i, (PA_W - LANE) // LANE)),
                  pl.BlockSpec(wgk.shape, const2),
                  pl.BlockSpec(bgk.shape, const2),
                  pl.BlockSpec(gg.shape, const2),
                  pl.BlockSpec(tril.shape, const2),
                  pl.BlockSpec(pm.shape, lambda i: (0, 0, 0)),
                  pl.BlockSpec((nseq, GLA_V, GLA_QK), lambda i: (i, 0, 0))],
        out_specs=[pl.BlockSpec((rows, GLA_V), row),
                   pl.BlockSpec((nseq, GLA_V, GLA_QK), lambda i: (i, 0, 0))],
        out_shape=[jax.ShapeDtypeStruct((batch * seq, GLA_V), F32),
                   jax.ShapeDtypeStruct((batch, GLA_V, GLA_QK), F32)],
        compiler_params=_params("parallel"),
        name="gla_sample",
    )(pa, pa, pa, pa, pa, wgk, bgk, gg, tril, pm, s0t)


def _state_to_blockdiag_t(s):
    b = s.shape[0]
    st = jnp.swapaxes(s, 2, 3)
    eye = jnp.eye(GLA_HEADS, dtype=s.dtype)
    return (st[:, :, :, None, :] * eye[None, :, None, :, None]).reshape(b, GLA_V, GLA_QK)


def _blockdiag_t_to_state(st):
    b = st.shape[0]
    s5 = st.reshape(b, GLA_HEADS, GLA_DV, GLA_HEADS, GLA_DK)
    return jnp.stack([jnp.swapaxes(s5[:, h, :, h, :], 1, 2) for h in range(GLA_HEADS)], axis=1)


def _alibi_slopes():
    return np.asarray([2.0 ** (-8.0 * (h + 1) / DIL_HEADS) for h in range(DIL_HEADS)], np.float64)


DIL_GROUP = 4
DIL_GW = DIL_GROUP * DIL_HD


def _dil_bias():
    i = np.arange(DIL_KEYS)[:, None]
    c = np.arange(2 * DIL_KEYS)[None, :]
    dist = DIL_KEYS + i - c
    ok = (dist >= 0) & (dist <= DIL_KEYS)
    out = np.empty((len(DIL_PATTERNS), DIL_HEADS, DIL_KEYS, 2 * DIL_KEYS), np.float32)
    for p, (_, d) in enumerate(DIL_PATTERNS):
        for h, sl in enumerate(_alibi_slopes()):
            out[p, h] = np.where(ok, -sl * (dist * d), -np.inf)
    out = out.reshape(len(DIL_PATTERNS), DIL_HEADS // DIL_GROUP, DIL_GROUP * DIL_KEYS, 2 * DIL_KEYS)
    return jnp.asarray(np.ascontiguousarray(out.transpose(0, 1, 3, 2)))


def _dil_prompt_kernel(qkv_hbm, biast_ref, o_ref, qkv_ref, acc_ref, m_ref, l_ref, sem):
    seq = qkv_ref.shape[1]
    blk = DIL_KEYS
    ngroups = DIL_HEADS // DIL_GROUP
    wparts = HALF_W // LANE
    aparts = DIL_GW // LANE
    head = _iota_div((blk, DIL_GW), 1, DIL_HD)

    load = pltpu.make_async_copy(qkv_hbm.at[:, pl.ds(pl.multiple_of(pl.program_id(0) * seq, seq), seq), :],
                                 qkv_ref, sem)
    load.start()
    load.wait()

    def rows(start, n, d):
        return pl.ds(pl.multiple_of(start, blk), n) if d == 1 else pl.ds(start, n, stride=d)

    def unpack(which, sel):
        return _unpack_halves(jnp.concatenate([qkv_ref[which * wparts + j, sel, :] for j in range(wparts)], axis=1))

    def by_head(x, keep):
        return jnp.concatenate([jnp.where(keep(hh), x, jnp.zeros_like(x)) for hh in range(DIL_GROUP)], axis=0)

    def stacked_q(q2, g):
        return by_head(q2[g] * (DIL_HD ** -0.5), lambda hh: head == hh)

    def find_max(p, d, start, has_prev):
        qsel = rows(start, blk, d)
        ksel = rows(start - d * blk, 2 * blk, d) if has_prev else qsel
        q2, k2 = unpack(0, qsel), unpack(1, ksel)
        per_head = []
        for g in range(ngroups):
            bias = biast_ref[p, g] if has_prev else biast_ref[p, g, blk:2 * blk, :]
            st = lax.dot_general(k2[g], stacked_q(q2, g), NT, preferred_element_type=F32) + bias
            mg = jnp.max(st, axis=0, keepdims=True)
            per_head += [mg[:, hh * blk:(hh + 1) * blk] for hh in range(DIL_GROUP)]
        rest = jnp.full((blk - DIL_HEADS, blk), -jnp.inf, F32)
        mt = jnp.concatenate(per_head + [rest], axis=0).T
        m_ref[qsel, :] = mt if p == 0 else jnp.maximum(m_ref[qsel, :], mt)

    def accumulate(p, d, start, has_prev):
        qsel = rows(start, blk, d)
        ksel = rows(start - d * blk, 2 * blk, d) if has_prev else qsel
        q2, k2, v2 = unpack(0, qsel), unpack(1, ksel), unpack(2, ksel)
        m_rows = m_ref[qsel, :].T
        per_head = []
        for g in range(ngroups):
            bias = biast_ref[p, g] if has_prev else biast_ref[p, g, blk:2 * blk, :]
            st = lax.dot_general(k2[g], stacked_q(q2, g), NT, preferred_element_type=F32) + bias
            m_g = jnp.concatenate([m_rows[g * DIL_GROUP + hh:g * DIL_GROUP + hh + 1, :] for hh in range(DIL_GROUP)],
                                  axis=1)
            pt = jnp.exp(st - m_g)
            l_g = jnp.sum(pt, axis=0, keepdims=True)
            per_head += [l_g[:, hh * blk:(hh + 1) * blk] for hh in range(DIL_GROUP)]
            out = lax.dot_general(pt.astype(BF16), v2[g], TN, preferred_element_type=F32)
            new = jnp.zeros((blk, DIL_GW), F32)
            for hh in range(DIL_GROUP):
                new = jnp.where(head == hh, out[hh * blk:(hh + 1) * blk], new)
            for j in range(aparts):
                part = new[:, j * LANE:(j + 1) * LANE]
                acc_ref[g * aparts + j, qsel, :] = part if p == 0 else acc_ref[g * aparts + j, qsel, :] + part
        rest = jnp.zeros((blk - DIL_HEADS, blk), F32)
        l_new = jnp.concatenate(per_head + [rest], axis=0).T
        l_ref[qsel, :] = l_new if p == 0 else l_ref[qsel, :] + l_new

    def sweep(block):
        for p, (_, d) in enumerate(DIL_PATTERNS):
            nblk = seq // (d * blk)

            def first_block(r, carry, p=p, d=d):
                block(p, d, r, False)
                return carry

            def later_blocks(r, carry, p=p, d=d, nblk=nblk):
                def one(ib, carry):
                    block(p, d, ib * (d * blk) + r, True)
                    return carry
                return lax.fori_loop(1, nblk, one, carry, unroll=2)

            lax.fori_loop(0, d, first_block, 0, unroll=2 if d > 1 else 1)
            if nblk > 1:
                lax.fori_loop(0, d, later_blocks, 0, unroll=2 if nblk == 2 else 1)

    sweep(find_max)
    sweep(accumulate)

    half = _iota_div((blk, LANE), 1, DIL_HD)

    def normalise(i, carry):
        sel = pl.ds(pl.multiple_of(i * blk, blk), blk)
        inv = 1.0 / l_ref[sel, :]
        for j in range(DIL_W // LANE):
            scale = jnp.where(half == 0, inv[:, 2 * j:2 * j + 1], inv[:, 2 * j + 1:2 * j + 2])
            o_ref[sel, j * LANE:(j + 1) * LANE] = (acc_ref[j, sel, :] * scale).astype(o_ref.dtype)
        return carry

    lax.fori_loop(0, seq // blk, normalise, 0)


def _dil_prompt(qkv, batch, seq):
    for w, d in DIL_PATTERNS:
        assert w // d == DIL_KEYS and seq % (d * DIL_KEYS) == 0
    biast = _dil_bias()
    return pl.pallas_call(
        _dil_prompt_kernel,
        grid=(batch,),
        in_specs=[pl.BlockSpec(memory_space=pl.ANY),
                  pl.BlockSpec(biast.shape, lambda b: (0, 0, 0, 0))],
        out_specs=pl.BlockSpec((seq, DIL_W), lambda b: (b, 0)),
        out_shape=jax.ShapeDtypeStruct((batch * seq, DIL_W), BF16),
        scratch_shapes=[pltpu.VMEM((qkv.shape[0], seq, LANE), I32),
                        pltpu.VMEM((DIL_W // LANE, seq, LANE), F32),
                        pltpu.VMEM((seq, LANE), F32), pltpu.VMEM((seq, LANE), F32),
                        pltpu.SemaphoreType.DMA(())],
        compiler_params=_params("arbitrary"),
        name="dil_prompt",
    )(qkv, biast)


def _dil_sample_bias(positions, past, seq):
    positions = np.asarray(positions)[None, :]
    dist = past + np.arange(seq)[:, None] - positions
    mult = np.zeros(dist.shape, np.float64)
    for w, d in DIL_PATTERNS:
        mult += (positions >= 0) & (dist >= 0) & (dist <= w) & (dist % d == 0)
    with np.errstate(divide="ignore"):
        logm = np.log(mult)
    return jnp.asarray(np.concatenate([-sl * dist + logm for sl in _alibi_slopes()], axis=0).astype(np.float32))


def _dil_sample_kernel(q_ref, kn_ref, vn_ref, kc_ref, vc_ref, bc_ref, bn_ref, o_ref, *, seq):
    rows = DIL_HEADS * seq
    q = q_ref[...] * (DIL_HD ** -0.5)
    qrep = jnp.concatenate([q] * DIL_HEADS, axis=0)
    own = _iota_div((rows, DIL_W), 0, seq) == _iota_div((rows, DIL_W), 1, DIL_HD)
    qbd = jnp.where(own, qrep, 0.0).astype(BF16)
    pad = jnp.zeros((LANE - seq, DIL_W), F32)
    ks = [kc_ref[...].astype(BF16), jnp.concatenate([kn_ref[...], pad], axis=0).astype(BF16)]
    vs = [vc_ref[...].astype(BF16), jnp.concatenate([vn_ref[...], pad], axis=0).astype(BF16)]
    ss = [lax.dot_general(qbd, k, NT, preferred_element_type=F32) + b[...] for k, b in zip(ks, (bc_ref, bn_ref))]
    m = functools.reduce(jnp.maximum, [jnp.max(s, axis=-1, keepdims=True) for s in ss])
    ps = [jnp.exp(s - m) for s in ss]
    l = sum(jnp.sum(p, axis=-1, keepdims=True) for p in ps)
    o = sum(jnp.dot(p.astype(BF16), v, preferred_element_type=F32) for p, v in zip(ps, vs)) * (1.0 / l)
    o = jnp.where(own, o, 0.0)
    res = o[0:seq]
    for h in range(1, DIL_HEADS):
        res = res + o[h * seq:(h + 1) * seq]
    o_ref[...] = res


def _dil_sample(pb, k_new, v_new, cache_k, cache_v, batch, seq):
    past = cache_k.shape[1]
    bias_c = _dil_sample_bias(np.arange(past), past, seq)
    bias_n = _dil_sample_bias(np.where(np.arange(LANE) < seq, past + np.arange(LANE), -1), past, seq)
    row = pl.BlockSpec((seq, DIL_W), lambda b: (b, 0))
    cache_spec = pl.BlockSpec((None, past, DIL_W), lambda b: (b, 0, 0))
    const = lambda a: pl.BlockSpec(a.shape, lambda b: (0, 0))
    return pl.pallas_call(
        functools.partial(_dil_sample_kernel, seq=seq),
        grid=(batch,),
        in_specs=[row, row, row, cache_spec, cache_spec, const(bias_c), const(bias_n)],
        out_specs=row,
        out_shape=jax.ShapeDtypeStruct((batch * seq, DIL_W), F32),
        compiler_params=_params("parallel"),
        name="dil_sample",
    )(pb, k_new, v_new, cache_k, cache_v, bias_c, bias_n)


def _out_proj_kernel(oa_ref, ob_ref, x_ref, wo_ref, g_ref, wq_ref, h_ref, q_ref, *, qscale):
    o = jnp.concatenate([oa_ref[...].astype(BF16), ob_ref[...].astype(BF16)], axis=1)
    h = x_ref[...] + jnp.dot(o, wo_ref[...], preferred_element_type=F32)
    h_ref[...] = h
    n = _rms(h, g_ref[...]).astype(BF16)
    q_ref[...] = (jnp.dot(n, wq_ref[...], preferred_element_type=F32) * qscale).astype(q_ref.dtype)


def _out_proj(oa, ob, x, wo, g, wq, q_dtype, tm):
    n, d = x.shape
    row = lambda i: (i, 0)
    const = lambda i: (0, 0)
    return pl.pallas_call(
        functools.partial(_out_proj_kernel, qscale=(d // MEM_HEADS) ** -0.5),
        grid=(n // tm,),
        in_specs=[pl.BlockSpec((tm, GLA_V), row), pl.BlockSpec((tm, DIL_W), row), pl.BlockSpec((tm, d), row),
                  pl.BlockSpec(wo.shape, const), pl.BlockSpec((1, d), const), pl.BlockSpec(wq.shape, const)],
        out_specs=[pl.BlockSpec((tm, d), row), pl.BlockSpec((tm, d), row)],
        out_shape=[jax.ShapeDtypeStruct((n, d), F32), jax.ShapeDtypeStruct((n, d), q_dtype)],
        compiler_params=_params("parallel"),
        name="out_proj",
    )(oa, ob, x, wo, g, wq)


def _cross_kernel(q_ref, mk_ref, mv_ref, o_ref):
    hd = q_ref.shape[1] // MEM_HEADS
    for h in range(MEM_HEADS):
        sl = slice(h * hd, (h + 1) * hd)
        s = lax.dot_general(q_ref[:, sl].astype(BF16), mk_ref[:, sl].astype(BF16), NT, preferred_element_type=F32)
        p = jnp.exp(s - jnp.max(s, axis=-1, keepdims=True))
        l = jnp.sum(p, axis=-1, keepdims=True)
        o = jnp.dot(p.astype(BF16), mv_ref[:, sl].astype(BF16), preferred_element_type=F32) * (1.0 / l)
        o_ref[:, sl] = o.astype(o_ref.dtype)


def _cross(q, mk, mv, batch, seq, tq, out_dtype):
    n, d = q.shape
    nt = seq // tq
    nm = mk.shape[1]
    return pl.pallas_call(
        _cross_kernel,
        grid=(batch, nt),
        in_specs=[pl.BlockSpec((tq, d), lambda b, i: (b * nt + i, 0)),
                  pl.BlockSpec((None, nm, d), lambda b, i: (b, 0, 0)),
                  pl.BlockSpec((None, nm, d), lambda b, i: (b, 0, 0))],
        out_specs=pl.BlockSpec((tq, d), lambda b, i: (b * nt + i, 0)),
        out_shape=jax.ShapeDtypeStruct((n, d), out_dtype),
        compiler_params=_params("parallel", "parallel"),
        name="cross_attn",
    )(q, mk, mv)


def _co_proj_kernel(o_ref, h_ref, wo_ref, g_ref, wrh_ref, wrl_ref, br_ref, h2_ref, n3_ref, lg_ref):
    h2 = h_ref[...] + jnp.dot(o_ref[...].astype(BF16), wo_ref[...], preferred_element_type=F32)
    h2_ref[...] = h2
    n3 = _rms(h2, g_ref[...])
    hi = n3.astype(BF16)
    lo = (n3 - hi.astype(F32)).astype(BF16)
    n3_ref[...] = hi
    lg_ref[...] = (jnp.dot(hi, wrh_ref[...], preferred_element_type=F32)
                   + jnp.dot(lo, wrh_ref[...], preferred_element_type=F32)
                   + jnp.dot(hi, wrl_ref[...], preferred_element_type=F32) + br_ref[...])


def _co_proj(o, h, wo, g, wrh, wrl, br, tm):
    n, d = h.shape
    row = lambda i: (i, 0)
    const = lambda i: (0, 0)
    return pl.pallas_call(
        _co_proj_kernel,
        grid=(n // tm,),
        in_specs=[pl.BlockSpec((tm, d), row), pl.BlockSpec((tm, d), row), pl.BlockSpec(wo.shape, const),
                  pl.BlockSpec((1, d), const), pl.BlockSpec(wrh.shape, const), pl.BlockSpec(wrl.shape, const),
                  pl.BlockSpec((1, LANE), const)],
        out_specs=[pl.BlockSpec((tm, d), row), pl.BlockSpec((tm, d), row), pl.BlockSpec((tm, LANE), row)],
        out_shape=[jax.ShapeDtypeStruct((n, d), F32), jax.ShapeDtypeStruct((n, d), BF16),
                   jax.ShapeDtypeStruct((n, LANE), F32)],
        compiler_params=_params("parallel"),
        name="co_proj",
    )(o, h, wo, g, wrh, wrl, br)


META_E, META_POS, META_W = 0, TOP_K, 2 * TOP_K


def _router_kernel(lg_ref, tri_ref, meta_ref, tile_ref, cnt_ref, run_ref, *, chunk):
    @pl.when(pl.program_id(0) == 0)
    def _():
        run_ref[...] = jnp.zeros_like(run_ref)

    lg = lg_ref[...]
    tm = lg.shape[0]
    lane = lax.broadcasted_iota(I32, (tm, LANE), 1)
    lane_f = lane.astype(F32)
    ninf = -jnp.inf
    first = lambda hit: jnp.min(jnp.where(hit, lane_f, float(LANE)), axis=-1, keepdims=True).astype(I32)

    gl = jnp.where(lane < N_GROUPS, lg, ninf)
    gmax = jnp.max(gl, axis=-1, keepdims=True)
    gidx = first(gl == gmax)
    pg = 1.0 / jnp.sum(jnp.exp(gl - gmax), axis=-1, keepdims=True)

    ex = lane - N_GROUPS
    el = jnp.where((ex >= 0) & (ex < N_EXPERTS) & ((ex >> int(math.log2(EXP_PER_GROUP))) == gidx), lg, ninf)
    t1 = jnp.max(el, axis=-1, keepdims=True)
    i1 = first(el == t1)
    el2 = jnp.where(lane == i1, ninf, el)
    t2 = jnp.max(el2, axis=-1, keepdims=True)
    i2 = first(el2 == t2)
    e = jnp.exp(t2 - t1)
    gates = (pg / (1.0 + e), pg * e / (1.0 + e))
    experts = (i1 - N_GROUPS, i2 - N_GROUPS)

    hits = [lane == ex_j for ex_j in experts]
    onehot = jnp.where(hits[0] | hits[1], 1.0, 0.0)
    before = jnp.dot(tri_ref[...], onehot.astype(BF16), preferred_element_type=F32)
    count = jnp.sum(onehot, axis=0, keepdims=True)
    slots = jnp.floor((count + (chunk - 1)) * (1.0 / chunk)) * chunk
    pos = [jnp.sum(jnp.where(lane < ex_j, slots, 0.0) + jnp.where(hit, before, 0.0), axis=-1, keepdims=True)
           for ex_j, hit in zip(experts, hits)]

    meta = jnp.zeros((tm, LANE), F32)
    for base, vals in ((META_E, [x.astype(F32) for x in experts]), (META_POS, pos), (META_W, gates)):
        for j, val in enumerate(vals):
            meta = jnp.where(lane == base + j, val, meta)
    meta_ref[...] = meta

    sub = lax.broadcasted_iota(I32, (SUBLANE, LANE), 0)
    tile_ref[...] = jnp.where(sub == 0, run_ref[...], jnp.where(sub == 1, slots, 0.0))
    run_ref[...] = run_ref[...] + slots
    cnt_ref[...] = run_ref[...]


def _router(logits, tm, chunk):
    n = logits.shape[0]
    tri = jnp.asarray(np.tril(np.ones((tm, tm), np.float32), -1), BF16)
    return pl.pallas_call(
        functools.partial(_router_kernel, chunk=chunk),
        grid=(n // tm,),
        in_specs=[pl.BlockSpec((tm, LANE), lambda i: (i, 0)), pl.BlockSpec((tm, tm), lambda i: (0, 0))],
        out_specs=[pl.BlockSpec((tm, LANE), lambda i: (i, 0)),
                   pl.BlockSpec((SUBLANE, LANE), lambda i: (i, 0)),
                   pl.BlockSpec((SUBLANE, LANE), lambda i: (0, 0))],
        out_shape=[jax.ShapeDtypeStruct((n, LANE), F32),
                   jax.ShapeDtypeStruct((n // tm * SUBLANE, LANE), F32),
                   jax.ShapeDtypeStruct((SUBLANE, LANE), F32)],
        scratch_shapes=[pltpu.VMEM((SUBLANE, LANE), F32)],
        compiler_params=_params("arbitrary"),
        name="router",
    )(logits, tri)


TAB_TOTAL = LANE - 1
FILL_BASE, FILL_N, FILL_TAIL = 0, N_EXPERTS, 2 * N_EXPERTS
FILL_ROWS = 64


def _one_hot(meta, lane_pos, slot):
    return jnp.where(lane_pos == meta[:, META_POS + slot:META_POS + slot + 1], 1.0, 0.0).astype(BF16)


def _chunk_copy(sorted_ref, hbm_ref, sem, chunk, to_hbm, off, base):
    src, dst = sorted_ref.at[pl.ds(off, chunk), :], hbm_ref.at[pl.ds(base, chunk), :]
    return pltpu.make_async_copy(src, dst, sem) if to_hbm else pltpu.make_async_copy(dst, src, sem)


def _start_chunks(tab_ref, sorted_ref, hbm_ref, sem, chunk, to_hbm):
    def one(c, carry):
        _chunk_copy(sorted_ref, hbm_ref, sem, chunk, to_hbm, pl.multiple_of(c * chunk, chunk),
                    pl.multiple_of(tab_ref[0, c], chunk)).start()
        return carry

    lax.fori_loop(0, tab_ref[0, TAB_TOTAL], one, 0)


def _wait_chunks(tab_ref, sorted_ref, hbm_ref, sem, chunk, to_hbm):
    def drain(c, carry):
        _chunk_copy(sorted_ref, hbm_ref, sem, chunk, to_hbm, 0, 0).wait()
        return carry

    lax.fori_loop(0, tab_ref[0, TAB_TOTAL], drain, 0)


def _dispatch_kernel(tab_ref, prev_ref, fill_ref, x_ref, meta_ref, xs_ref, sorted_ref, zero_ref, sems, *, chunk):
    i = pl.program_id(0)
    slot = lax.rem(i, 2)
    tm, npos = x_ref.shape[0], sorted_ref.shape[1]
    meta = meta_ref[...]
    lane_pos = lax.broadcasted_iota(I32, (tm, npos), 1).astype(F32)
    place = _one_hot(meta, lane_pos, 0) + _one_hot(meta, lane_pos, 1)
    sorted_ref[slot] = lax.dot_general(place, x_ref[...], TN, preferred_element_type=F32)

    @pl.when(i > 0)
    def _():
        _wait_chunks(prev_ref, sorted_ref.at[1 - slot], xs_ref, sems.at[1 - slot], chunk, True)

    _start_chunks(tab_ref, sorted_ref.at[slot], xs_ref, sems.at[slot], chunk, True)

    @pl.when(i == pl.num_programs(0) - 1)
    def _():
        _wait_chunks(tab_ref, sorted_ref.at[slot], xs_ref, sems.at[slot], chunk, True)
        zero_ref[...] = jnp.zeros_like(zero_ref)
        big = zero_ref.shape[0]
        sem = sems.at[0]

        def fill(row, size):
            return pltpu.make_async_copy(zero_ref.at[pl.ds(0, size), :],
                                         xs_ref.at[pl.ds(pl.multiple_of(row, size), size), :], sem)

        def per_expert(e, total):
            def one(c, carry):
                fill(fill_ref[0, FILL_BASE + e] + c * chunk, chunk).start()
                return carry
            lax.fori_loop(0, fill_ref[0, FILL_N + e], one, 0)
            return total + fill_ref[0, FILL_N + e]

        def drain(c, carry):
            fill(0, chunk).wait()
            return carry

        lax.fori_loop(0, lax.fori_loop(0, N_EXPERTS, per_expert, 0), drain, 0)

        def tail(c, carry):
            fill(fill_ref[0, FILL_TAIL] + c * big, big).start()
            return carry

        def drain_tail(c, carry):
            fill(0, big).wait()
            return carry

        lax.fori_loop(0, fill_ref[0, FILL_TAIL + 1], tail, 0)
        lax.fori_loop(0, fill_ref[0, FILL_TAIL + 1], drain_tail, 0)


def _dispatch(x, meta, tab, fill, rows, tm, chunk):
    n, d = x.shape
    npos = TOP_K * tm + N_EXPERTS * chunk
    return pl.pallas_call(
        functools.partial(_dispatch_kernel, chunk=chunk),
        grid=(n // tm,),
        in_specs=[pl.BlockSpec((None, 1, LANE), lambda i: (i, 0, 0), memory_space=pltpu.SMEM),
                  pl.BlockSpec((None, 1, LANE), lambda i: (jnp.maximum(i - 1, 0), 0, 0), memory_space=pltpu.SMEM),
                  pl.BlockSpec((1, LANE), lambda i: (0, 0), memory_space=pltpu.SMEM),
                  pl.BlockSpec((tm, d), lambda i: (i, 0)),
                  pl.BlockSpec((tm, LANE), lambda i: (i, 0))],
        out_specs=pl.BlockSpec(memory_space=pl.ANY),
        out_shape=jax.ShapeDtypeStruct((rows, d), F32),
        scratch_shapes=[pltpu.VMEM((2, npos, d), F32), pltpu.VMEM((FILL_ROWS, d), F32),
                        pltpu.SemaphoreType.DMA((2,))],
        compiler_params=_params("arbitrary"),
        name="moe_dispatch",
    )(tab, tab, fill, x, meta)


def _expert_kernel(be_ref, nu_ref, x_ref, w1_ref, w3_ref, w2_ref, y_ref):
    del be_ref
    live = pl.program_id(0) < nu_ref[0]

    @pl.when(live)
    def _():
        x = x_ref[...].astype(BF16)
        a = jnp.dot(x, w1_ref[...], preferred_element_type=F32)
        b = jnp.dot(x, w3_ref[...], preferred_element_type=F32)
        y_ref[...] = jnp.dot((_silu(a) * b).astype(BF16), w2_ref[...], preferred_element_type=F32)

    @pl.when(jnp.logical_not(live))
    def _():
        y_ref[...] = jnp.zeros_like(y_ref)


def _experts(xs, block_e, n_used, w1, w3, w2, bm):
    rows, d = xs.shape
    de = w1.shape[2]
    return pl.pallas_call(
        _expert_kernel,
        grid_spec=pltpu.PrefetchScalarGridSpec(
            num_scalar_prefetch=2,
            grid=(rows // bm,),
            in_specs=[pl.BlockSpec((bm, d), lambda i, be, nu: (i, 0)),
                      pl.BlockSpec((None, d, de), lambda i, be, nu: (be[i], 0, 0)),
                      pl.BlockSpec((None, d, de), lambda i, be, nu: (be[i], 0, 0)),
                      pl.BlockSpec((None, de, d), lambda i, be, nu: (be[i], 0, 0))],
            out_specs=pl.BlockSpec((bm, d), lambda i, be, nu: (i, 0))),
        out_shape=jax.ShapeDtypeStruct((rows, d), F32),
        compiler_params=_params("arbitrary"),
        name="moe_experts",
    )(block_e, n_used, xs, w1, w3, w2)


def _combine_kernel(tab_ref, next_ref, h_ref, meta_ref, g_ref, yb_ref, y_ref, sorted_ref, sems, *, chunk):
    i = pl.program_id(0)
    slot = lax.rem(i, 2)

    @pl.when(i == 0)
    def _():
        sorted_ref[...] = jnp.zeros_like(sorted_ref)
        _start_chunks(tab_ref, sorted_ref.at[0], yb_ref, sems.at[0], chunk, False)

    @pl.when(i + 1 < pl.num_programs(0))
    def _():
        _start_chunks(next_ref, sorted_ref.at[1 - slot], yb_ref, sems.at[1 - slot], chunk, False)

    _wait_chunks(tab_ref, sorted_ref.at[slot], yb_ref, sems.at[slot], chunk, False)
    tm, npos = h_ref.shape[0], sorted_ref.shape[1]
    meta = meta_ref[...]
    lane_pos = lax.broadcasted_iota(I32, (tm, npos), 1).astype(F32)
    yb = sorted_ref[slot].astype(BF16)
    gate = sum(jnp.where(lane_pos == meta[:, META_POS + j:META_POS + j + 1], meta[:, META_W + j:META_W + j + 1], 0.0)
               for j in range(TOP_K))
    moe = jnp.dot(gate.astype(BF16), yb, preferred_element_type=F32)
    y_ref[...] = _rms(h_ref[...] + moe, g_ref[...])


def _combine(h, meta, tab, yb, g, tm, chunk):
    n, d = h.shape
    npos = TOP_K * tm + N_EXPERTS * chunk
    return pl.pallas_call(
        functools.partial(_combine_kernel, chunk=chunk),
        grid=(n // tm,),
        in_specs=[pl.BlockSpec((None, 1, LANE), lambda i: (i, 0, 0), memory_space=pltpu.SMEM),
                  pl.BlockSpec((None, 1, LANE), lambda i: (jnp.minimum(i + 1, n // tm - 1), 0, 0),
                               memory_space=pltpu.SMEM),
                  pl.BlockSpec((tm, d), lambda i: (i, 0)),
                  pl.BlockSpec((tm, LANE), lambda i: (i, 0)),
                  pl.BlockSpec((1, d), lambda i: (0, 0)),
                  pl.BlockSpec(memory_space=pl.ANY)],
        out_specs=pl.BlockSpec((tm, d), lambda i: (i, 0)),
        out_shape=jax.ShapeDtypeStruct((n, d), F32),
        scratch_shapes=[pltpu.VMEM((2, npos, d), F32), pltpu.SemaphoreType.DMA((2,))],
        compiler_params=_params("arbitrary"),
        name="moe_combine",
    )(tab, tab, h, meta, g, yb)


def _moe_final(h2, n3, logits, w1, w3, w2, g_final, tm, bm, chunk):
    n = h2.shape[0]
    nt = n // tm
    max_chunks = (TOP_K * tm + N_EXPERTS * chunk) // chunk
    assert max_chunks <= TAB_TOTAL and bm % chunk == 0
    meta, tiles, cnt = _router(logits, tm, chunk)
    rows_e = cnt[0, :N_EXPERTS].astype(I32)
    padded = (rows_e + bm - 1) // bm * bm
    pad_end = jnp.cumsum(padded)
    pad_start = pad_end - padded
    tiles = tiles.reshape(nt, SUBLANE, LANE)
    run_before = tiles[:, 0, :N_EXPERTS].astype(I32)
    nchunks = tiles[:, 1, :N_EXPERTS].astype(I32) // chunk
    last_chunk = jnp.cumsum(nchunks, axis=1)
    c = jnp.arange(max_chunks, dtype=I32)[None, :]
    e_of_c = jnp.minimum(jnp.sum((last_chunk[:, None, :] <= c[:, :, None]).astype(I32), axis=2), N_EXPERTS - 1)
    is_e = e_of_c[:, :, None] == jnp.arange(N_EXPERTS, dtype=I32)[None, None, :]
    pick = lambda a: jnp.sum(jnp.where(is_e, a[:, None, :], 0), axis=2)
    chunk_row = pick(pad_start[None, :] + run_before) + (c - pick(last_chunk - nchunks)) * chunk
    pad_lanes = lambda a: jnp.pad(a, ((0, 0), (0, LANE - a.shape[1])))
    tab = jnp.concatenate([pad_lanes(chunk_row)[:, :TAB_TOTAL], last_chunk[:, -1:]], axis=1).reshape(nt, 1, LANE)
    max_rows = n * TOP_K + (chunk - 1) * min(n * TOP_K, nt * N_EXPERTS) + N_EXPERTS * (bm - 1)
    nb = -(-max_rows // bm)
    assert bm % FILL_ROWS == 0
    fill = pad_lanes(jnp.concatenate([pad_start + rows_e, (padded - rows_e) // chunk,
                                      pad_end[-1:], (nb * bm - pad_end[-1:]) // FILL_ROWS])[None, :])
    block_start = jnp.arange(nb, dtype=I32) * bm
    block_e = jnp.minimum(jnp.sum((pad_end[None, :] <= block_start[:, None]).astype(I32), axis=1), N_EXPERTS - 1)
    n_used = (pad_end[-1:] // bm).astype(I32)
    xs = _dispatch(n3, meta, tab, fill, nb * bm, tm, chunk)
    yb = _experts(xs, block_e, n_used, w1, w3, w2, bm)
    return _combine(h2, meta, tab, yb, g_final, tm, chunk)


def _group(x3, mk, mv, w, keep, *, cache=None, state=None):
    batch, seq, d = x3.shape
    n = batch * seq
    x = x3.reshape(n, d)
    tm = _tile(n, TOKEN_TILE)
    sample = cache is not None
    pa, pb, k_new, v_new = _norm_proj(x, w["g1"], w["w_in"], batch, seq, keep, not sample, tm)
    if sample:
        assert keep == seq
        oa, st = _gla_sample(pa, _state_to_blockdiag_t(state), w["wgk"], w["bgk"], w["gg"], batch, seq,
                             rows=GLA_CHUNK)
        ob = _dil_sample(pb, k_new, v_new, cache[0].reshape(batch, -1, DIL_W), cache[1].reshape(batch, -1, DIL_W),
                         batch, seq)
    else:
        zeros = jnp.zeros((batch, GLA_V, GLA_QK), F32)
        oa, st = _gla_prompt(pa, zeros, w["wgk"], w["bgk"], w["gg"], batch, seq, tb=_tile(seq, TOKEN_TILE))
        ob = _dil_prompt(pb, batch, seq)
    h1, qc = _out_proj(oa, ob, x, w["w_out"], w["g2"], w["w_cq"], F32 if sample else BF16, tm)
    oc = _cross(qc, mk, mv, batch, seq, _tile(seq, TOKEN_TILE), F32 if sample else BF16)
    h2, n3, logits = _co_proj(oc, h1, w["w_co"], w["g3"], w["wr_hi"], w["wr_lo"], w["br"], tm)
    y = _moe_final(h2, n3, logits, w["w_e1"], w["w_e3"], w["w_e2"], w["g_final"], _tile(n, MOE_TILE), MOE_BLOCK,
                   MOE_CHUNK)
    return (y.reshape(batch, seq, d), _blockdiag_t_to_state(st),
            k_new.reshape(batch, keep, DIL_HEADS, DIL_HD), v_new.reshape(batch, keep, DIL_HEADS, DIL_HD))


def _pack_weights(l, g_norm1, w_in, w_gk2, b_gk, g_gla_out, w_out, g_norm2, w_cq, w_co, g_norm3, w_gr, b_gr,
                  w_er, b_er, w_e1, w_e3, w_e2, g_final):
    d = w_in.shape[1]
    sp = np.cumsum([GLA_QK, GLA_QK, GLA_V, GATE_RANK, GLA_V, DIL_W, DIL_W, DIL_W])
    wi = w_in[l]
    q_a, k_a, v_a, glr, r_a, q_b, k_b, v_b = (wi[:, a:b] for a, b in zip([0, *sp[:-1]], sp))
    glr = jnp.pad(glr, ((0, 0), (0, LANE - GATE_RANK)))
    wr = jnp.pad(jnp.concatenate([w_gr[l], w_er[l]], axis=1), ((0, 0), (0, LANE - N_GROUPS - N_EXPERTS)))
    wr_hi = wr.astype(BF16)
    return dict(
        g1=g_norm1[l].reshape(1, d), g2=g_norm2[l].reshape(1, d), g3=g_norm3[l].reshape(1, d),
        g_final=g_final.reshape(1, d),
        w_in=jnp.concatenate([q_a, k_a, v_a, r_a, glr, q_b, k_b, v_b], axis=1).astype(BF16),
        wgk=jnp.pad(w_gk2[l], ((0, LANE - GATE_RANK), (0, 0))).astype(BF16),
        bgk=b_gk[l].reshape(1, GLA_QK), gg=g_gla_out[l].reshape(1, GLA_DV),
        w_out=w_out[l].astype(BF16), w_cq=w_cq[l].astype(BF16), w_co=w_co[l].astype(BF16),
        wr_hi=wr_hi, wr_lo=(wr - wr_hi.astype(F32)).astype(BF16),
        br=jnp.pad(jnp.concatenate([b_gr[l], b_er[l]]), (0, LANE - N_GROUPS - N_EXPERTS)).reshape(1, LANE),
        w_e1=w_e1[l].astype(BF16), w_e3=w_e3[l].astype(BF16), w_e2=w_e2[l].astype(BF16))


def kernel(x_prompt, x_sample, cache_swa_k, cache_swa_v, state_gla, cache_mem_k, cache_mem_v, mem_prompt,
           g_norm1, w_in, w_gk2, b_gk, g_gla_out, w_out, g_norm2, g_mem, w_cq, w_mk, w_mv, w_co,
           g_norm3, w_gr, b_gr, w_er, b_er, w_e1, w_e3, w_e2, g_final):
    depth = w_in.shape[0]
    assert depth == 1, "the final norm is fused into the last MoE stage; stacked layers are not supported"
    batch, seq, d = x_prompt.shape
    sb, sseq, _ = x_sample.shape
    nm = mem_prompt.shape[1]
    keep = min(DIL_PATTERNS[-1][0], seq)
    l = 0
    w = _pack_weights(l, g_norm1, w_in, w_gk2, b_gk, g_gla_out, w_out, g_norm2, w_cq, w_co, g_norm3,
                      w_gr, b_gr, w_er, b_er, w_e1, w_e3, w_e2, g_final)
    mkv = _norm_matmul(mem_prompt.reshape(batch * nm, d), g_mem[l].reshape(1, d),
                       jnp.concatenate([w_mk[l], w_mv[l]], axis=1).astype(BF16), _tile(batch * nm, TOKEN_TILE))
    mk = mkv[:, :d].reshape(batch, nm, d)
    mv = mkv[:, d:].reshape(batch, nm, d)
    yp, sp, kp, vp = _group(x_prompt, mk, mv, w, keep)
    past = cache_swa_k.shape[2]
    ys, ss, kn, vn = _group(x_sample, cache_mem_k.reshape(sb, nm, d), cache_mem_v.reshape(sb, nm, d), w, sseq,
                            cache=(cache_swa_k.reshape(sb, past, DIL_W), cache_swa_v.reshape(sb, past, DIL_W)),
                            state=state_gla[l])
    heads = lambda m: m.reshape(batch, nm, MEM_HEADS, d // MEM_HEADS)
    return (yp, ys, *(o[None] for o in (kp, vp, sp, heads(mk), heads(mv), kn, vn, ss)))
```
